```python
import jax
import jax.numpy as jnp
from jax import lax
import numpy as np

D_MODEL = 2048
BATCH = 2
SEQ = 16384
DEPTH = 1
DEC_BATCH = 2
DEC_SEQ = 4096
PAST_LEN = 128

HEAD_DIM = 128
N_Q_HEADS = 8
N_KV_HEADS = 2
ATTN_WIDTH = N_Q_HEADS * HEAD_DIM
KV_WIDTH = N_KV_HEADS * HEAD_DIM
Q_BLOCK = 128
ROPE_THETA = 10000.0
GRID_W = 64
N_A_HEADS = 8
A_HEAD_DIM = 128
A_WIDTH = N_A_HEADS * A_HEAD_DIM
CHUNK = 128
MIX_WIDTH = ATTN_WIDTH + A_WIDTH
IN_WIDTH = ATTN_WIDTH + 2 * KV_WIDTH + 2 * A_WIDTH
N_EXPERTS = 64
TOP_K = 8
N_GROUPS = 8
TOPK_GROUPS = 4
EXPERT_DIM = 512
SHARED_DIM = 512
ROUTED_SCALE = 2.5
MOE_BLOCK = 256
N_MOD = 6
EPS = 1e-6

kernel_name = "hybrid_sgu_gqa_moe_encoder"


def rms_norm(x, g):
    xf = x.astype(jnp.float32)
    y = xf * lax.rsqrt(jnp.mean(xf * xf, axis=-1, keepdims=True) + EPS)
    return (y * g.astype(jnp.float32)).astype(x.dtype)


def layer_norm(x, g, b):
    xf = x.astype(jnp.float32)
    xc = xf - jnp.mean(xf, axis=-1, keepdims=True)
    y = xc * lax.rsqrt(jnp.mean(xc * xc, axis=-1, keepdims=True) + EPS)
    return (y * g.astype(jnp.float32) + b.astype(jnp.float32)).astype(x.dtype)


def axial_rope_tables(seq_len):
    rows = seq_len // GRID_W
    row_idx = jnp.repeat(jnp.arange(rows, dtype=jnp.float32), GRID_W)
    col_idx = jnp.tile(jnp.arange(GRID_W, dtype=jnp.float32), rows)
    n_pairs = HEAD_DIM // 4
    freqs = ROPE_THETA ** (-jnp.arange(n_pairs, dtype=jnp.float32) / n_pairs)
    ang = jnp.concatenate([row_idx[:, None] * freqs, col_idx[:, None] * freqs], axis=-1)
    return jnp.cos(ang), jnp.sin(ang)


def apply_rope(x, cos, sin):
    B, S, H, Dh = x.shape
    xf = x.astype(jnp.float32).reshape(B, S, H, Dh // 2, 2)
    x0, x1 = xf[..., 0], xf[..., 1]
    c = cos[None, :, None, :]
    s = sin[None, :, None, :]
    out = jnp.stack([x0 * c - x1 * s, x0 * s + x1 * c], axis=-1)
    return out.reshape(B, S, H, Dh).astype(x.dtype)


def gqa_attention(q, k, v):
    B, S = q.shape[0], q.shape[1]
    G = N_Q_HEADS // N_KV_HEADS
    nblk = S // Q_BLOCK
    qb = q.reshape(B, nblk, Q_BLOCK, N_KV_HEADS, G, HEAD_DIM).transpose(1, 0, 2, 3, 4, 5)
    kf = k.astype(jnp.float32)
    vf = v.astype(jnp.float32)
    scale = HEAD_DIM ** -0.5

    def block(qi):
        s = jnp.einsum('bqkgd,bskd->bkgqs', qi.astype(jnp.float32), kf) * scale
        p = jax.nn.softmax(s, axis=-1)
        return jnp.einsum('bkgqs,bskd->bqkgd', p, vf).astype(q.dtype)

    o = lax.map(block, qb)
    return o.transpose(1, 0, 2, 3, 4, 5).reshape(B, S, ATTN_WIDTH)


def spatial_gating(u, v, ln_g, ln_b, w_s, b_s):
    B, S = u.shape[0], u.shape[1]
    vn = layer_norm(v, ln_g, ln_b)
    vc = vn.reshape(B, S // CHUNK, CHUNK, N_A_HEADS, A_HEAD_DIM)
    mixed = jnp.einsum('hpq,bcqhd->bcphd', w_s, vc) + b_s.T[None, None, :, :, None]
    return u * mixed.reshape(B, S, A_WIDTH)


def route(h, w_router, router_bias):
    N = h.shape[0]
    scores = jax.nn.sigmoid(jnp.dot(h.astype(jnp.float32), w_router.astype(jnp.float32)))
    biased = scores + router_bias.astype(jnp.float32)
    grp = biased.reshape(N, N_GROUPS, N_EXPERTS // N_GROUPS)
    grp_score = lax.top_k(grp, 2)[0].sum(axis=-1)
    _, grp_idx = lax.top_k(grp_score, TOPK_GROUPS)
    grp_mask = jnp.any(grp_idx[..., None] == jnp.arange(N_GROUPS), axis=1)
    expert_mask = jnp.repeat(grp_mask, N_EXPERTS // N_GROUPS, axis=1)
    masked = jnp.where(expert_mask, biased, -jnp.inf)
    _, idx = lax.top_k(masked, TOP_K)
    w = jnp.take_along_axis(scores, idx, axis=1)
    w = w / (jnp.sum(w, axis=-1, keepdims=True) + 1e-20) * ROUTED_SCALE
    return idx, w


def routed_experts(h, idx, w, w_gate, w_up, w_down):
    N, D = h.shape
    NK = N * TOP_K
    e_flat = idx.reshape(NK)
    tok_flat = jnp.arange(NK, dtype=jnp.int32) // TOP_K
    w_flat = w.reshape(NK)
    order = jnp.argsort(e_flat)
    e_sorted = e_flat[order]
    counts = jnp.zeros((N_EXPERTS,), jnp.int32).at[e_flat].add(1)
    starts = jnp.cumsum(counts) - counts
    padded = (counts + MOE_BLOCK - 1) // MOE_BLOCK * MOE_BLOCK
    pad_ends = jnp.cumsum(padded)
    pad_starts = pad_ends - padded
    rank = jnp.arange(NK, dtype=jnp.int32) - starts[e_sorted]
    dest = pad_starts[e_sorted] + rank
    n_blocks = NK // MOE_BLOCK + N_EXPERTS
    P = n_blocks * MOE_BLOCK
    buf_tok = jnp.full((P,), N, jnp.int32).at[dest].set(tok_flat[order])
    buf_w = jnp.zeros((P,), jnp.float32).at[dest].set(w_flat[order])
    block_start = jnp.arange(n_blocks, dtype=jnp.int32) * MOE_BLOCK
    block_exp = jnp.minimum(jnp.searchsorted(pad_ends, block_start, side='right'), N_EXPERTS - 1)
    h_pad = jnp.concatenate([h, jnp.zeros((1, D), h.dtype)], axis=0)

    def step(y, blk):
        e, toks, gw = blk
        xb = h_pad[toks]
        a = jnp.dot(xb, w_gate[e])
        b = jnp.dot(xb, w_up[e])
        out = jnp.dot(jax.nn.silu(a) * b, w_down[e]) * gw[:, None].astype(h.dtype)
        return y.at[toks].add(out), None

    y0 = jnp.zeros((N + 1, D), h.dtype)
    y, _ = lax.scan(step, y0, (block_exp, buf_tok.reshape(n_blocks, MOE_BLOCK), buf_w.reshape(n_blocks, MOE_BLOCK)))
    return y[:N]


def encoder_layer(x, c, norm1_g, w_mod, b_mod, w_in, q_norm_g, k_norm_g, sgu_ln_g, sgu_ln_b, w_s, b_s,
                  attn_out_g, sgu_out_g, w_out, norm2_g, w_router, router_bias, w_gate, w_up, w_down,
                  ws_gate, ws_up, ws_down):
    B, S, D = x.shape
    mod = (jnp.dot(jax.nn.silu(c), w_mod) + b_mod).reshape(B, N_MOD, 1, D)
    shift1, scale1, gate1 = mod[:, 0], mod[:, 1], mod[:, 2]
    shift2, scale2, gate2 = mod[:, 3], mod[:, 4], mod[:, 5]

    h = rms_norm(x, norm1_g) * (1 + scale1) + shift1
    proj = jnp.dot(h, w_in)
    cuts = [ATTN_WIDTH, ATTN_WIDTH + KV_WIDTH, ATTN_WIDTH + 2 * KV_WIDTH, ATTN_WIDTH + 2 * KV_WIDTH + A_WIDTH]
    q, k, v, u, gv = jnp.split(proj, cuts, axis=-1)
    q = rms_norm(q.reshape(B, S, N_Q_HEADS, HEAD_DIM), q_norm_g)
    k = rms_norm(k.reshape(B, S, N_KV_HEADS, HEAD_DIM), k_norm_g)
    cos, sin = axial_rope_tables(S)
    q = apply_rope(q, cos, sin)
    k = apply_rope(k, cos, sin)
    attn = gqa_attention(q, k, v.reshape(B, S, N_KV_HEADS, HEAD_DIM))
    sgu = spatial_gating(u, gv, sgu_ln_g, sgu_ln_b, w_s, b_s)
    mixed = jnp.concatenate([rms_norm(attn, attn_out_g), rms_norm(sgu, sgu_out_g)], axis=-1)
    x = x + gate1 * jnp.dot(mixed, w_out)

    h2 = rms_norm(x, norm2_g) * (1 + scale2) + shift2
    flat = h2.reshape(B * S, D)
    shared = jnp.dot(jax.nn.silu(jnp.dot(flat, ws_gate)) * jnp.dot(flat, ws_up), ws_down)
    idx, w = route(flat, w_router, router_bias)
    routed = routed_experts(flat, idx, w, w_gate, w_up, w_down)
    x = x + gate2 * (shared + routed).reshape(B, S, D)
    return x


def setup_inputs(seed: int = 0) -> dict:
    key = jax.random.key(seed)
    ks = jax.random.split(key, 27)
    f32 = jnp.float32
    D = D_MODEL
    nrm = lambda k, shape, s: jax.random.normal(k, shape, f32) * s
    gain = lambda k, shape: 1.0 + 0.02 * jax.random.normal(k, shape, f32)
    return {
        "x_prompt": nrm(ks[0], (BATCH, SEQ, D), 1.0),
        "x_sample": nrm(ks[1], (DEC_BATCH, DEC_SEQ, D), 1.0),
        "c_prompt": nrm(ks[2], (BATCH, D), 1.0),
        "c_sample": nrm(ks[3], (DEC_BATCH, D), 1.0),
        "norm1_g": gain(ks[4], (DEPTH, D)),
        "w_mod": nrm(ks[5], (DEPTH, D, N_MOD * D), 0.5 * D ** -0.5),
        "b_mod": nrm(ks[6], (DEPTH, N_MOD * D), 0.01),
        "w_in": nrm(ks[7], (DEPTH, D, IN_WIDTH), D ** -0.5),
        "q_norm_g": gain(ks[8], (DEPTH, HEAD_DIM)),
        "k_norm_g": gain(ks[9], (DEPTH, HEAD_DIM)),
        "sgu_ln_g": gain(ks[10], (DEPTH, A_WIDTH)),
        "sgu_ln_b": nrm(ks[11], (DEPTH, A_WIDTH), 0.01),
        "w_s": nrm(ks[12], (DEPTH, N_A_HEADS, CHUNK, CHUNK), CHUNK ** -0.5),
        "b_s": gain(ks[13], (DEPTH, N_A_HEADS, CHUNK)),
        "attn_out_g": gain(ks[14], (DEPTH, ATTN_WIDTH)),
        "sgu_out_g": gain(ks[15], (DEPTH, A_WIDTH)),
        "w_out": nrm(ks[16], (DEPTH, MIX_WIDTH, D), MIX_WIDTH ** -0.5),
        "norm2_g": gain(ks[17], (DEPTH, D)),
        "w_router": nrm(ks[18], (DEPTH, D, N_EXPERTS), D ** -0.5),
        "router_bias": nrm(ks[19], (DEPTH, N_EXPERTS), 0.01),
        "w_gate": nrm(ks[20], (DEPTH, N_EXPERTS, D, EXPERT_DIM), D ** -0.5),
        "w_up": nrm(ks[21], (DEPTH, N_EXPERTS, D, EXPERT_DIM), D ** -0.5),
        "w_down": nrm(ks[22], (DEPTH, N_EXPERTS, EXPERT_DIM, D), EXPERT_DIM ** -0.5),
        "ws_gate": nrm(ks[23], (DEPTH, D, SHARED_DIM), D ** -0.5),
        "ws_up": nrm(ks[24], (DEPTH, D, SHARED_DIM), D ** -0.5),
        "ws_down": nrm(ks[25], (DEPTH, SHARED_DIM, D), SHARED_DIM ** -0.5),
    }


def reference(x_prompt, x_sample, c_prompt, c_sample, norm1_g, w_mod, b_mod, w_in, q_norm_g, k_norm_g,
              sgu_ln_g, sgu_ln_b, w_s, b_s, attn_out_g, sgu_out_g, w_out, norm2_g, w_router, router_bias,
              w_gate, w_up, w_down, ws_gate, ws_up, ws_down):
    y_prompt = x_prompt
    y_sample = x_sample
    for l in range(DEPTH):
        layer_params = (norm1_g[l], w_mod[l], b_mod[l], w_in[l], q_norm_g[l], k_norm_g[l], sgu_ln_g[l],
                        sgu_ln_b[l], w_s[l], b_s[l], attn_out_g[l], sgu_out_g[l], w_out[l], norm2_g[l],
                        w_router[l], router_bias[l], w_gate[l], w_up[l], w_down[l], ws_gate[l], ws_up[l],
                        ws_down[l])
        y_prompt = encoder_layer(y_prompt, c_prompt, *layer_params)
        y_sample = encoder_layer(y_sample, c_sample, *layer_params)
    return (y_prompt, y_sample)
```

```python
import functools
import math

import numpy as np
import jax
import jax.numpy as jnp
from jax import lax
from jax.experimental import pallas as pl
from jax.experimental.pallas import tpu as pltpu

F32 = jnp.float32
BF16 = jnp.bfloat16
I32 = jnp.int32

HEAD_DIM = 128
N_Q_HEADS = 8
N_KV_HEADS = 2
Q_PER_KV = N_Q_HEADS // N_KV_HEADS
ATTN_WIDTH = N_Q_HEADS * HEAD_DIM
KV_WIDTH = N_KV_HEADS * HEAD_DIM
ROPE_THETA = 10000.0
GRID_W = 64
N_A_HEADS = 8
A_HEAD_DIM = 128
A_WIDTH = N_A_HEADS * A_HEAD_DIM
CHUNK = 128
N_EXPERTS = 64
TOP_K = 8
N_GROUPS = 8
GROUP_SIZE = N_EXPERTS // N_GROUPS
TOPK_GROUPS = 4
ROUTED_SCALE = 2.5
N_MOD = 6
EPS = 1e-6

VMEM_LIMIT_BYTES = 56 * 1024 * 1024
NT_DIMS = (((1,), (1,)), ((), ()))


def _cparams(n_axes):
    return pltpu.CompilerParams(
        dimension_semantics=("arbitrary",) * n_axes, vmem_limit_bytes=VMEM_LIMIT_BYTES)


def _tile(n, pref):
    t = min(n, pref)
    while n % t:
        t //= 2
    return t


def _const_spec(shape):
    nd = len(shape)
    return pl.BlockSpec(shape, lambda *a: (0,) * nd)


def _mod_kernel(c_ref, w_ref, b_ref, o_ref):
    c = c_ref[...]
    a = (c * jax.nn.sigmoid(c)).astype(BF16)
    o_ref[...] = jnp.dot(a, w_ref[...].astype(BF16), preferred_element_type=F32) + b_ref[...]


def _mod(c8, w_mod, b_mod):
    d, n = w_mod.shape
    tn = _tile(n, 1024)
    return pl.pallas_call(
        _mod_kernel,
        grid=(n // tn,),
        in_specs=[_const_spec((8, d)),
                  pl.BlockSpec((d, tn), lambda j: (0, j)),
                  pl.BlockSpec((1, tn), lambda j: (0, j))],
        out_specs=pl.BlockSpec((8, tn), lambda j: (0, j)),
        out_shape=jax.ShapeDtypeStruct((8, n), F32),
        compiler_params=_cparams(1),
        name="mod",
    )(c8, w_mod, b_mod.reshape(1, n))


class _Groups:
    def __init__(self, groups):
        self.groups = groups
        self.n_tokens = sum(b * s for b, s in groups)

    def tile_maps(self, t):
        (b0, s0), (b1, s1) = self.groups
        n0 = b0 * s0 // t
        t0, t1 = s0 // t, s1 // t

        def bidx(i):
            return jnp.where(i < n0, i // t0, b0 + (i - n0) // t1)

        def pidx(i):
            return jnp.where(i < n0, i % t0, (i - n0) % t1)

        return bidx, pidx


def _rope_tables(seq_len):
    pos = np.arange(seq_len)
    row = (pos // GRID_W).astype(np.float64)
    col = (pos % GRID_W).astype(np.float64)
    n_pairs = HEAD_DIM // 4
    freqs = ROPE_THETA ** (-np.arange(n_pairs, dtype=np.float64) / n_pairs)
    ang = np.concatenate([row[:, None] * freqs, col[:, None] * freqs], axis=-1)
    c, s = np.cos(ang), np.sin(ang)
    cos2 = np.concatenate([c, c], axis=-1)
    sin2 = np.concatenate([-s, s], axis=-1)
    return cos2, sin2


def _inproj_kernel(x_ref, mod_ref, g1_ref, wqt_ref, wk_ref, wvt_ref, wu_ref, wgv_ref,
                   qg_ref, kg_ref, cost_ref, sint_ref, cos_ref, sin_ref,
                   qt_ref, k_ref, vt_ref, u_ref, gv_ref):
    x = x_ref[...]
    shift1 = mod_ref[0:1, :]
    scale1 = mod_ref[1:2, :]
    ms = jnp.mean(x * x, axis=-1, keepdims=True)
    h = (x * lax.rsqrt(ms + EPS) * g1_ref[...]) * (1.0 + scale1) + shift1
    hb = h.astype(BF16)
    half = HEAD_DIM // 2

    qt = lax.dot_general(wqt_ref[...], hb, NT_DIMS, preferred_element_type=F32)
    cost = cost_ref[...]
    sint = sint_ref[...]
    qg = qg_ref[...]
    for hh in range(N_Q_HEADS):
        qh = qt[hh * HEAD_DIM:(hh + 1) * HEAD_DIM, :]
        msq = jnp.mean(qh * qh, axis=0, keepdims=True)
        qn = qh * lax.rsqrt(msq + EPS) * qg
        rot = jnp.concatenate([qn[half:, :], qn[:half, :]], axis=0)
        qt_ref[hh * HEAD_DIM:(hh + 1) * HEAD_DIM, :] = (qn * cost + rot * sint).astype(BF16)

    vt_ref[...] = lax.dot_general(wvt_ref[...], hb, NT_DIMS, preferred_element_type=F32).astype(BF16)

    kk = jnp.dot(hb, wk_ref[...], preferred_element_type=F32)
    cos = cos_ref[...]
    sin = sin_ref[...]
    kg = kg_ref[...]
    for j in range(N_KV_HEADS):
        kh = kk[:, j * HEAD_DIM:(j + 1) * HEAD_DIM]
        msk = jnp.mean(kh * kh, axis=-1, keepdims=True)
        kn = kh * lax.rsqrt(msk + EPS) * kg
        rot = jnp.concatenate([kn[:, half:], kn[:, :half]], axis=1)
        k_ref[:, j * HEAD_DIM:(j + 1) * HEAD_DIM] = (kn * cos + rot * sin).astype(BF16)

    u_ref[...] = jnp.dot(hb, wu_ref[...], preferred_element_type=F32).astype(BF16)
    gv_ref[...] = jnp.dot(hb, wgv_ref[...], preferred_element_type=F32).astype(BF16)


def _inproj(x, modp, g1, wqt, wk, wvt, wu, wgv, qg, kg, tabs, grp, tm):
    n, d = x.shape
    bidx, pidx = grp.tile_maps(tm)
    cost, sint, cos, sin = tabs
    tok = lambda w: pl.BlockSpec((tm, w), lambda i: (i, 0))
    tokt = lambda w: pl.BlockSpec((w, tm), lambda i: (0, i))
    return pl.pallas_call(
        _inproj_kernel,
        grid=(n // tm,),
        in_specs=[tok(d),
                  pl.BlockSpec((None, 8, d), lambda i: (bidx(i), 0, 0)),
                  _const_spec((1, d)),
                  _const_spec(wqt.shape), _const_spec(wk.shape), _const_spec(wvt.shape),
                  _const_spec(wu.shape), _const_spec(wgv.shape),
                  _const_spec((HEAD_DIM, 1)), _const_spec((1, HEAD_DIM)),
                  pl.BlockSpec((HEAD_DIM, tm), lambda i: (0, pidx(i))),
                  pl.BlockSpec((HEAD_DIM, tm), lambda i: (0, pidx(i))),
                  pl.BlockSpec((tm, HEAD_DIM), lambda i: (pidx(i), 0)),
                  pl.BlockSpec((tm, HEAD_DIM), lambda i: (pidx(i), 0))],
        out_specs=[tokt(ATTN_WIDTH), tok(KV_WIDTH), tokt(KV_WIDTH), tok(A_WIDTH), tok(A_WIDTH)],
        out_shape=[jax.ShapeDtypeStruct((ATTN_WIDTH, n), BF16),
                   jax.ShapeDtypeStruct((n, KV_WIDTH), BF16),
                   jax.ShapeDtypeStruct((KV_WIDTH, n), BF16),
                   jax.ShapeDtypeStruct((n, A_WIDTH), BF16),
                   jax.ShapeDtypeStruct((n, A_WIDTH), BF16)],
        compiler_params=_cparams(1),
        name="inproj",
    )(x, modp, g1, wqt, wk, wvt, wu, wgv, qg, kg, cost, sint, cos, sin)


def _flash_kernel(qt_s, kt_s, fl_s, q_ref, k_ref, v_ref, o_ref, qs_sc, m_sc, l_sc, acc_sc, *, bq):
    s_id = pl.program_id(1)
    flag = fl_s[s_id]

    @pl.when((flag & 1) != 0)
    def _init():
        m_sc[...] = jnp.full(m_sc.shape, -jnp.inf, F32)
        l_sc[...] = jnp.zeros(l_sc.shape, F32)
        acc_sc[...] = jnp.zeros(acc_sc.shape, F32)
        for hh in range(Q_PER_KV):
            qs_sc[:, hh * bq:(hh + 1) * bq] = q_ref[hh * HEAD_DIM:(hh + 1) * HEAD_DIM, :]

    s = jnp.dot(k_ref[...], qs_sc[...], preferred_element_type=F32)
    m_prev = m_sc[...]
    m_new = jnp.maximum(m_prev, jnp.max(s, axis=0, keepdims=True))
    alpha = jnp.exp(m_prev - m_new)
    p = jnp.exp(s - m_new)
    l_sc[...] = alpha * l_sc[...] + jnp.sum(p, axis=0, keepdims=True)
    acc_sc[...] = alpha * acc_sc[...] + jnp.dot(v_ref[...], p.astype(BF16), preferred_element_type=F32)
    m_sc[...] = m_new

    @pl.when((flag & 2) != 0)
    def _fin():
        o = acc_sc[...] / l_sc[...]
        for hh in range(Q_PER_KV):
            o_ref[:, hh * HEAD_DIM:(hh + 1) * HEAD_DIM] = o[:, hh * bq:(hh + 1) * bq].T.astype(BF16)


def _flash_schedule(grp, bq, bkv):
    qt, kt, fl = [], [], []
    off = 0
    for b, s in grp.groups:
        for bi in range(b):
            base = off + bi * s
            nk = s // bkv
            for qi in range(s // bq):
                for ki in range(nk):
                    qt.append(base // bq + qi)
                    kt.append(base // bkv + ki)
                    fl.append((1 if ki == 0 else 0) | (2 if ki == nk - 1 else 0))
        off += b * s
    return (np.asarray(qt, np.int32), np.asarray(kt, np.int32), np.asarray(fl, np.int32))


def _flash(qt, k, vt, grp, bq, bkv):
    n = k.shape[0]
    qt_a, kt_a, fl_a = _flash_schedule(grp, bq, bkv)
    n_steps = len(qt_a)
    w = Q_PER_KV * HEAD_DIM
    gs = pltpu.PrefetchScalarGridSpec(
        num_scalar_prefetch=3,
        grid=(N_KV_HEADS, n_steps),
        in_specs=[pl.BlockSpec((w, bq), lambda h, s, qa, ka, fa: (h, qa[s])),
                  pl.BlockSpec((bkv, HEAD_DIM), lambda h, s, qa, ka, fa: (ka[s], h)),
                  pl.BlockSpec((HEAD_DIM, bkv), lambda h, s, qa, ka, fa: (h, ka[s]))],
        out_specs=pl.BlockSpec((bq, w), lambda h, s, qa, ka, fa: (qa[s], h)),
        scratch_shapes=[pltpu.VMEM((HEAD_DIM, Q_PER_KV * bq), BF16),
                        pltpu.VMEM((1, Q_PER_KV * bq), F32),
                        pltpu.VMEM((1, Q_PER_KV * bq), F32),
                        pltpu.VMEM((HEAD_DIM, Q_PER_KV * bq), F32)])
    return pl.pallas_call(
        functools.partial(_flash_kernel, bq=bq),
        grid_spec=gs,
        out_shape=jax.ShapeDtypeStruct((n, ATTN_WIDTH), BF16),
        compiler_params=_cparams(2),
        name="flash",
    )(jnp.asarray(qt_a), jnp.asarray(kt_a), jnp.asarray(fl_a), qt, k, vt)


def _sgu_kernel(u_ref, gv_ref, lng_ref, lnb_ref, ws_ref, bst_ref, og_ref, o_ref, *, n_chunks):
    v = gv_ref[...].astype(F32)
    mean = jnp.mean(v, axis=-1, keepdims=True)
    vc = v - mean
    var = jnp.mean(vc * vc, axis=-1, keepdims=True)
    vn = (vc * lax.rsqrt(var + EPS) * lng_ref[...] + lnb_ref[...]).astype(BF16)
    u = u_ref[...].astype(F32)
    heads = []
    for hh in range(N_A_HEADS):
        cols = slice(hh * A_HEAD_DIM, (hh + 1) * A_HEAD_DIM)
        rhs = jnp.concatenate([vn[c * CHUNK:(c + 1) * CHUNK, cols] for c in range(n_chunks)], axis=1)
        mixed = jnp.dot(ws_ref[hh], rhs, preferred_element_type=F32) + bst_ref[:, hh:hh + 1]
        heads.append(jnp.concatenate(
            [mixed[:, c * A_HEAD_DIM:(c + 1) * A_HEAD_DIM] for c in range(n_chunks)], axis=0))
    sg = u * jnp.concatenate(heads, axis=1)
    ms = jnp.mean(sg * sg, axis=-1, keepdims=True)
    o_ref[...] = (sg * lax.rsqrt(ms + EPS) * og_ref[...]).astype(BF16)


def _sgu(u, gv, lng, lnb, ws, bst, og, tm):
    n = u.shape[0]
    tok = pl.BlockSpec((tm, A_WIDTH), lambda i: (i, 0))
    return pl.pallas_call(
        functools.partial(_sgu_kernel, n_chunks=tm // CHUNK),
        grid=(n // tm,),
        in_specs=[tok, tok, _const_spec((1, A_WIDTH)), _const_spec((1, A_WIDTH)),
                  _const_spec(ws.shape), _const_spec(bst.shape), _const_spec((1, A_WIDTH))],
        out_specs=tok,
        out_shape=jax.ShapeDtypeStruct((n, A_WIDTH), BF16),
        compiler_params=_cparams(1),
        name="sgu",
    )(u, gv, lng, lnb, ws, bst, og)


def _outproj_kernel(attn_ref, sgu_ref, x_ref, mod_ref, ag_ref, wo_ref, g2_ref, wrt_ref,
                    x1_ref, h2_ref, lgt_ref):
    a = attn_ref[...].astype(F32)
    ms = jnp.mean(a * a, axis=-1, keepdims=True)
    an = (a * lax.rsqrt(ms + EPS) * ag_ref[...]).astype(BF16)
    o = jnp.dot(an, wo_ref[:ATTN_WIDTH, :], preferred_element_type=F32)
    o = o + jnp.dot(sgu_ref[...], wo_ref[ATTN_WIDTH:, :], preferred_element_type=F32)
    gate1 = mod_ref[2:3, :]
    shift2 = mod_ref[3:4, :]
    scale2 = mod_ref[4:5, :]
    x1 = x_ref[...] + gate1 * o
    x1_ref[...] = x1
    ms2 = jnp.mean(x1 * x1, axis=-1, keepdims=True)
    h2 = (x1 * lax.rsqrt(ms2 + EPS) * g2_ref[...]) * (1.0 + scale2) + shift2
    h2_ref[...] = h2
    lgt_ref[...] = lax.dot_general(wrt_ref[...], h2.astype(BF16), NT_DIMS, preferred_element_type=F32)


def _outproj(attn, sgun, x, modp, ag, wo, g2, wrt, grp, tm):
    n, d = x.shape
    bidx, _ = grp.tile_maps(tm)
    tok = lambda w: pl.BlockSpec((tm, w), lambda i: (i, 0))
    return pl.pallas_call(
        _outproj_kernel,
        grid=(n // tm,),
        in_specs=[tok(ATTN_WIDTH), tok(A_WIDTH), tok(d),
                  pl.BlockSpec((None, 8, d), lambda i: (bidx(i), 0, 0)),
                  _const_spec((1, ATTN_WIDTH)), _const_spec(wo.shape), _const_spec((1, d)),
                  _const_spec(wrt.shape)],
        out_specs=[tok(d), tok(d), pl.BlockSpec((N_EXPERTS, tm), lambda i: (0, i))],
        out_shape=[jax.ShapeDtypeStruct((n, d), F32), jax.ShapeDtypeStruct((n, d), F32),
                   jax.ShapeDtypeStruct((N_EXPERTS, n), F32)],
        compiler_params=_cparams(1),
        name="outproj",
    )(attn, sgun, x, modp, ag, wo, g2, wrt)


def _first_index(hit, iota, sentinel):
    return jnp.min(jnp.where(hit, iota, sentinel), axis=0, keepdims=True)


def _router_kernel(lg_ref, bias_ref, idx_ref, w_ref, rank_ref, cnt_ref, run_sc):
    i = pl.program_id(0)
    tt = lg_ref.shape[1]

    @pl.when(i == 0)
    def _init():
        run_sc[...] = jnp.zeros(run_sc.shape, F32)

    scores = jax.nn.sigmoid(lg_ref[...])
    biased = scores + bias_ref[...]
    neg = -jnp.inf
    iota_g = lax.broadcasted_iota(I32, (GROUP_SIZE, tt), 0).astype(F32)

    gs_rows = []
    for g in range(N_GROUPS):
        xg = biased[g * GROUP_SIZE:(g + 1) * GROUP_SIZE, :]
        m1 = jnp.max(xg, axis=0, keepdims=True)
        f1 = _first_index(xg == m1, iota_g, float(GROUP_SIZE))
        m2 = jnp.max(jnp.where(iota_g == f1, neg, xg), axis=0, keepdims=True)
        gs_rows.append(m1 + m2)
    gs = jnp.concatenate(gs_rows, axis=0)

    iota_n = lax.broadcasted_iota(I32, (N_GROUPS, tt), 0).astype(F32)
    gsel = jnp.zeros((N_GROUPS, tt), F32)
    cur = gs
    for _ in range(TOPK_GROUPS):
        m = jnp.max(cur, axis=0, keepdims=True)
        f = _first_index(cur == m, iota_n, float(N_GROUPS))
        hit = iota_n == f
        gsel = jnp.where(hit, 1.0, gsel)
        cur = jnp.where(hit, neg, cur)

    masked = jnp.concatenate(
        [jnp.where(gsel[g:g + 1, :] > 0.5, biased[g * GROUP_SIZE:(g + 1) * GROUP_SIZE, :], neg)
         for g in range(N_GROUPS)], axis=0)

    iota_e = lax.broadcasted_iota(I32, (N_EXPERTS, tt), 0).astype(F32)
    sel = jnp.zeros((N_EXPERTS, tt), F32)
    idx_rows, w_rows = [], []
    for _ in range(TOP_K):
        m = jnp.max(masked, axis=0, keepdims=True)
        f = _first_index(masked == m, iota_e, float(N_EXPERTS))
        hit = iota_e == f
        idx_rows.append(f)
        w_rows.append(jnp.sum(jnp.where(hit, scores, 0.0), axis=0, keepdims=True))
        sel = jnp.where(hit, 1.0, sel)
        masked = jnp.where(hit, neg, masked)
    idx = jnp.concatenate(idx_rows, axis=0)
    wk = jnp.concatenate(w_rows, axis=0)
    wk = wk / (jnp.sum(wk, axis=0, keepdims=True) + 1e-20) * ROUTED_SCALE

    r = lax.broadcasted_iota(I32, (tt, tt), 0)
    c = lax.broadcasted_iota(I32, (tt, tt), 1)
    upper = jnp.where(r < c, 1.0, 0.0).astype(BF16)
    prefix = jnp.dot(sel.astype(BF16), upper, preferred_element_type=F32) + run_sc[...]
    rank_rows = [jnp.sum(jnp.where(iota_e == idx[k:k + 1, :], prefix, 0.0), axis=0, keepdims=True)
                 for k in range(TOP_K)]
    run_sc[...] = run_sc[...] + jnp.sum(sel, axis=1, keepdims=True)

    idx_ref[...] = idx.astype(I32)
    w_ref[...] = wk
    rank_ref[...] = jnp.concatenate(rank_rows, axis=0).astype(I32)
    cnt_ref[...] = jnp.broadcast_to(run_sc[...], cnt_ref.shape).astype(I32)


def _router(lgt, bias, tt):
    n = lgt.shape[1]
    tokk = pl.BlockSpec((TOP_K, tt), lambda i: (0, i))
    return pl.pallas_call(
        _router_kernel,
        grid=(n // tt,),
        in_specs=[pl.BlockSpec((N_EXPERTS, tt), lambda i: (0, i)), _const_spec((N_EXPERTS, 1))],
        out_specs=[tokk, tokk, tokk, _const_spec((N_EXPERTS, 128))],
        out_shape=[jax.ShapeDtypeStruct((TOP_K, n), I32), jax.ShapeDtypeStruct((TOP_K, n), F32),
                   jax.ShapeDtypeStruct((TOP_K, n), I32), jax.ShapeDtypeStruct((N_EXPERTS, 128), I32)],
        scratch_shapes=[pltpu.VMEM((N_EXPERTS, 1), F32)],
        compiler_params=_cparams(1),
        name="router",
    )(lgt, bias)


def _dest_kernel(idx_ref, rank_ref, start_ref, o_ref):
    tt = idx_ref.shape[1]
    iota_e = lax.broadcasted_iota(I32, (N_EXPERTS, tt), 0)
    start = start_ref[...]
    rows = []
    for k in range(TOP_K):
        hit = iota_e == idx_ref[k:k + 1, :]
        rows.append(jnp.sum(jnp.where(hit, start, 0), axis=0, keepdims=True))
    o_ref[...] = jnp.concatenate(rows, axis=0) + rank_ref[...]


def _dest(idx, rank, pad_start, tt):
    n = idx.shape[1]
    tokk = pl.BlockSpec((TOP_K, tt), lambda i: (0, i))
    return pl.pallas_call(
        _dest_kernel,
        grid=(n // tt,),
        in_specs=[tokk, tokk, _const_spec((N_EXPERTS, 1))],
        out_specs=tokk,
        out_shape=jax.ShapeDtypeStruct((TOP_K, n), I32),
        compiler_params=_cparams(1),
        name="dest",
    )(idx, rank, pad_start.reshape(N_EXPERTS, 1))


def _dispatch_kernel(cnt_s, start_s, pend_s, dest_ref, h_ref, xs_ref, zero_sc, sem, *, td):
    i = pl.program_id(0)

    def row_copy(t, k):
        return pltpu.make_async_copy(h_ref.at[pl.ds(t, 1)], xs_ref.at[pl.ds(dest_ref[k, t], 1)], sem)

    def issue(t, c):
        for k in range(TOP_K):
            row_copy(t, k).start()
        return c

    lax.fori_loop(0, td, issue, 0)

    def drain(t, c):
        for k in range(TOP_K):
            row_copy(t, k).wait()
        return c

    lax.fori_loop(0, td, drain, 0)

    @pl.when(i == pl.num_programs(0) - 1)
    def _pad():
        zero_sc[...] = jnp.zeros(zero_sc.shape, F32)

        def pad_copy(p):
            return pltpu.make_async_copy(zero_sc, xs_ref.at[pl.ds(p, 1)], sem)

        def per_expert(e, c):
            lo = start_s[e] + cnt_s[e]
            hi = pend_s[e]

            def go(p, c2):
                pad_copy(p).start()
                return c2

            lax.fori_loop(lo, hi, go, 0)

            def done(p, c2):
                pad_copy(p).wait()
                return c2

            lax.fori_loop(lo, hi, done, 0)
            return c

        lax.fori_loop(0, N_EXPERTS, per_expert, 0)


def _dispatch(counts, pad_start, pad_end, dest3, h2, p_rows, td):
    n, d = h2.shape
    gs = pltpu.PrefetchScalarGridSpec(
        num_scalar_prefetch=3,
        grid=(n // td,),
        in_specs=[pl.BlockSpec((None, TOP_K, td), lambda i, *_: (i, 0, 0), memory_space=pltpu.SMEM),
                  pl.BlockSpec((td, d), lambda i, *_: (i, 0))],
        out_specs=pl.BlockSpec(memory_space=pl.ANY),
        scratch_shapes=[pltpu.VMEM((1, d), F32), pltpu.SemaphoreType.DMA(())])
    return pl.pallas_call(
        functools.partial(_dispatch_kernel, td=td),
        grid_spec=gs,
        out_shape=jax.ShapeDtypeStruct((p_rows, d), F32),
        compiler_params=_cparams(1),
        name="dispatch",
    )(counts, pad_start, pad_end, dest3, h2)


def _swiglu(xb, wg, wu, wd):
    a = jnp.dot(xb, wg, preferred_element_type=F32)
    b = jnp.dot(xb, wu, preferred_element_type=F32)
    act = (a * jax.nn.sigmoid(a) * b).astype(BF16)
    return jnp.dot(act, wd, preferred_element_type=F32)


def _shared_kernel(h_ref, wg_ref, wu_ref, wd_ref, o_ref):
    o_ref[...] = _swiglu(h_ref[...].astype(BF16), wg_ref[...], wu_ref[...], wd_ref[...])


def _shared(h2, wg, wu, wd, tm):
    n, d = h2.shape
    tok = pl.BlockSpec((tm, d), lambda i: (i, 0))
    return pl.pallas_call(
        _shared_kernel,
        grid=(n // tm,),
        in_specs=[tok, _const_spec(wg.shape), _const_spec(wu.shape), _const_spec(wd.shape)],
        out_specs=tok,
        out_shape=jax.ShapeDtypeStruct((n, d), F32),
        compiler_params=_cparams(1),
        name="shared",
    )(h2, wg, wu, wd)


def _experts_kernel(be_s, nu_s, x_ref, wg_ref, wu_ref, wd_ref, o_ref, wg_sc, wu_sc, wd_sc):
    i = pl.program_id(0)
    used = i < nu_s[0]
    prev = be_s[jnp.maximum(i - 1, 0)]
    fresh = jnp.logical_or(i == 0, be_s[i] != prev)

    @pl.when(jnp.logical_and(used, fresh))
    def _cast():
        wg_sc[...] = wg_ref[...].astype(BF16)
        wu_sc[...] = wu_ref[...].astype(BF16)
        wd_sc[...] = wd_ref[...].astype(BF16)

    @pl.when(used)
    def _run():
        o_ref[...] = _swiglu(x_ref[...].astype(BF16), wg_sc[...], wu_sc[...], wd_sc[...])


def _experts(block_exp, n_used, xs, w_gate, w_up, w_down, bm):
    p_rows, d = xs.shape
    de = w_gate.shape[2]
    n_blocks = p_rows // bm
    row = lambda i, be, nu: (jnp.minimum(i, nu[0] - 1), 0)
    wsel = lambda i, be, nu: (be[jnp.minimum(i, nu[0] - 1)], 0, 0)
    gs = pltpu.PrefetchScalarGridSpec(
        num_scalar_prefetch=2,
        grid=(n_blocks,),
        in_specs=[pl.BlockSpec((bm, d), row),
                  pl.BlockSpec((None, d, de), wsel),
                  pl.BlockSpec((None, d, de), wsel),
                  pl.BlockSpec((None, de, d), wsel)],
        out_specs=pl.BlockSpec((bm, d), row),
        scratch_shapes=[pltpu.VMEM((d, de), BF16), pltpu.VMEM((d, de), BF16), pltpu.VMEM((de, d), BF16)])
    return pl.pallas_call(
        _experts_kernel,
        grid_spec=gs,
        out_shape=jax.ShapeDtypeStruct((p_rows, d), F32),
        compiler_params=_cparams(1),
        name="experts",
    )(block_exp, n_used, xs, w_gate, w_up, w_down)


def _combine_kernel(dest_ref, w_ref, x1_ref, sh_ref, mod_ref, ys_ref, o_ref, buf, sem, *, tc):
    def row_copy(t, k):
        return pltpu.make_async_copy(ys_ref.at[pl.ds(dest_ref[k, t], 1)], buf.at[k, pl.ds(t, 1)], sem)

    def issue(t, c):
        for k in range(TOP_K):
            row_copy(t, k).start()
        return c

    lax.fori_loop(0, tc, issue, 0)

    def drain(t, c):
        for k in range(TOP_K):
            row_copy(t, k).wait()
        return c

    lax.fori_loop(0, tc, drain, 0)

    w = w_ref[...]
    routed = w[:, 0:1] * buf[0]
    for k in range(1, TOP_K):
        routed = routed + w[:, k:k + 1] * buf[k]
    gate2 = mod_ref[5:6, :]
    o_ref[...] = x1_ref[...] + gate2 * (sh_ref[...] + routed)


def _combine(dest3, wt, x1, shared, modp, ys, grp, tc):
    n, d = x1.shape
    bidx, _ = grp.tile_maps(tc)
    tok = pl.BlockSpec((tc, d), lambda i: (i, 0))
    return pl.pallas_call(
        functools.partial(_combine_kernel, tc=tc),
        grid=(n // tc,),
        in_specs=[pl.BlockSpec((None, TOP_K, tc), lambda i: (i, 0, 0), memory_space=pltpu.SMEM),
                  pl.BlockSpec((tc, TOP_K), lambda i: (i, 0)),
                  tok, tok,
                  pl.BlockSpec((None, 8, d), lambda i: (bidx(i), 0, 0)),
                  pl.BlockSpec(memory_space=pl.ANY)],
        out_specs=tok,
        out_shape=jax.ShapeDtypeStruct((n, d), F32),
        scratch_shapes=[pltpu.VMEM((TOP_K, tc, d), F32), pltpu.SemaphoreType.DMA(())],
        compiler_params=_cparams(1),
        name="combine",
    )(dest3, wt, x1, shared, modp, ys)


def _tok3(a, t):
    k, n = a.shape
    return a.reshape(k, n // t, t).transpose(1, 0, 2)


def _layer(x, c, grp, norm1_g, w_mod, b_mod, w_in, q_norm_g, k_norm_g, sgu_ln_g, sgu_ln_b, w_s, b_s,
           attn_out_g, sgu_out_g, w_out, norm2_g, w_router, router_bias, w_gate, w_up, w_down,
           ws_gate, ws_up, ws_down):
    n, d = x.shape
    nb = c.shape[0]
    s_min = min(s for _, s in grp.groups)
    s_max = max(s for _, s in grp.groups)

    c8 = jnp.pad(c, ((0, 8 - nb), (0, 0)))
    mod = _mod(c8, w_mod, b_mod)[:nb].reshape(nb, N_MOD, d)
    modp = jnp.pad(mod, ((0, 0), (0, 8 - N_MOD), (0, 0)))

    perm = np.concatenate([np.arange(0, HEAD_DIM, 2), np.arange(1, HEAD_DIM, 2)])
    c0, c1, c2, c3 = ATTN_WIDTH, ATTN_WIDTH + KV_WIDTH, ATTN_WIDTH + 2 * KV_WIDTH, ATTN_WIDTH + 2 * KV_WIDTH + A_WIDTH
    wq = w_in[:, :c0].reshape(d, N_Q_HEADS, HEAD_DIM)[:, :, perm].reshape(d, ATTN_WIDTH)
    wk = w_in[:, c0:c1].reshape(d, N_KV_HEADS, HEAD_DIM)[:, :, perm].reshape(d, KV_WIDTH)
    wqt = wq.T.astype(BF16)
    wkb = wk.astype(BF16)
    wvt = w_in[:, c1:c2].T.astype(BF16)
    wu = w_in[:, c2:c3].astype(BF16)
    wgv = w_in[:, c3:].astype(BF16)
    qg = q_norm_g[perm].reshape(HEAD_DIM, 1)
    kg = k_norm_g[perm].reshape(1, HEAD_DIM)

    cos2, sin2 = _rope_tables(s_max)
    scale = HEAD_DIM ** -0.5
    tabs = (jnp.asarray((cos2 * scale).T, F32), jnp.asarray((sin2 * scale).T, F32),
            jnp.asarray(cos2, F32), jnp.asarray(sin2, F32))

    tm = _tile(s_min, 512)
    qt, k, vt, u, gv = _inproj(x, modp, norm1_g.reshape(1, d), wqt, wkb, wvt, wu, wgv, qg, kg, tabs, grp, tm)

    attn = _flash(qt, k, vt, grp, _tile(s_min, 256), _tile(s_min, 512))
    sgun = _sgu(u, gv, sgu_ln_g.reshape(1, A_WIDTH), sgu_ln_b.reshape(1, A_WIDTH), w_s.astype(BF16),
                b_s.T, sgu_out_g.reshape(1, A_WIDTH), _tile(s_min, 512))

    x1, h2, lgt = _outproj(attn, sgun, x, modp, attn_out_g.reshape(1, ATTN_WIDTH), w_out.astype(BF16),
                           norm2_g.reshape(1, d), w_router.T.astype(BF16), grp, _tile(s_min, 256))

    idx, wts, rank, cnt = _router(lgt, router_bias.reshape(N_EXPERTS, 1), _tile(s_min, 512))

    bm = 256
    counts = cnt[:, 0]
    padded = (counts + bm - 1) // bm * bm
    pad_end = jnp.cumsum(padded).astype(I32)
    pad_start = pad_end - padded
    n_blocks = n * TOP_K // bm + N_EXPERTS
    p_rows = n_blocks * bm
    n_used = (pad_end[-1] // bm).reshape(1).astype(I32)
    block_start = jnp.arange(n_blocks, dtype=I32) * bm
    block_exp = jnp.minimum(jnp.searchsorted(pad_end, block_start, side="right"), N_EXPERTS - 1).astype(I32)

    dest = _dest(idx, rank, pad_start, _tile(s_min, 2048))

    td = _tile(s_min, 256)
    xs = _dispatch(counts, pad_start, pad_end, _tok3(dest, td), h2, p_rows, td)
    shared = _shared(h2, ws_gate.astype(BF16), ws_up.astype(BF16), ws_down.astype(BF16), _tile(s_min, 512))
    ys = _experts(block_exp, n_used, xs, w_gate, w_up, w_down, bm)

    tc = _tile(s_min, 128)
    return _combine(_tok3(dest, tc), wts.T, x1, shared, modp, ys, grp, tc)


def kernel(x_prompt, x_sample, c_prompt, c_sample, norm1_g, w_mod, b_mod, w_in, q_norm_g, k_norm_g, sgu_ln_g, sgu_ln_b, w_s, b_s, attn_out_g, sgu_out_g, w_out, norm2_g, w_router, router_bias, w_gate, w_up, w_down, ws_gate, ws_up, ws_down):
    depth = norm1_g.shape[0]
    bp, sp, d = x_prompt.shape
    bs, ss, _ = x_sample.shape
    grp = _Groups([(bp, sp), (bs, ss)])
    x = jnp.concatenate([x_prompt.reshape(bp * sp, d), x_sample.reshape(bs * ss, d)], axis=0)
    c = jnp.concatenate([c_prompt, c_sample], axis=0)
    for l in range(depth):
        x = _layer(x, c, grp, norm1_g[l], w_mod[l], b_mod[l], w_in[l], q_norm_g[l], k_norm_g[l],
                   sgu_ln_g[l], sgu_ln_b[l], w_s[l], b_s[l], attn_out_g[l], sgu_out_g[l], w_out[l],
                   norm2_g[l], w_router[l], router_bias[l], w_gate[l], w_up[l], w_down[l],
                   ws_gate[l], ws_up[l], ws_down[l])
    return (x[:bp * sp].reshape(bp, sp, d), x[bp * sp:].reshape(bs, ss, d))
```

```python
import functools
import math

import numpy as np
import jax
import jax.numpy as jnp
from jax import lax
from jax.experimental import pallas as pl
from jax.experimental.pallas import tpu as pltpu

F32 = jnp.float32
BF16 = jnp.bfloat16
I32 = jnp.int32

HEAD_DIM = 128
N_Q_HEADS = 8
N_KV_HEADS = 2
Q_PER_KV = N_Q_HEADS // N_KV_HEADS
ATTN_WIDTH = N_Q_HEADS * HEAD_DIM
KV_WIDTH = N_KV_HEADS * HEAD_DIM
ROPE_THETA = 10000.0
GRID_W = 64
N_A_HEADS = 8
A_HEAD_DIM = 128
A_WIDTH = N_A_HEADS * A_HEAD_DIM
CHUNK = 128
N_EXPERTS = 64
TOP_K = 8
N_GROUPS = 8
GROUP_SIZE = N_EXPERTS // N_GROUPS
TOPK_GROUPS = 4
ROUTED_SCALE = 2.5
N_MOD = 6
EPS = 1e-6

VMEM_LIMIT_BYTES = 56 * 1024 * 1024
NT_DIMS = (((1,), (1,)), ((), ()))


def _cparams(n_axes):
    return pltpu.CompilerParams(
        dimension_semantics=("arbitrary",) * n_axes, vmem_limit_bytes=VMEM_LIMIT_BYTES)


def _tile(n, pref):
    t = min(n, pref)
    while n % t:
        t //= 2
    return t


def _const_spec(shape):
    nd = len(shape)
    return pl.BlockSpec(shape, lambda *a: (0,) * nd)


def _mod_kernel(c_ref, w_ref, b_ref, o_ref):
    c = c_ref[...]
    a = (c * jax.nn.sigmoid(c)).astype(BF16)
    o_ref[...] = jnp.dot(a, w_ref[...].astype(BF16), preferred_element_type=F32) + b_ref[...]


def _mod(c8, w_mod, b_mod):
    d, n = w_mod.shape
    tn = _tile(n, 1024)
    return pl.pallas_call(
        _mod_kernel,
        grid=(n // tn,),
        in_specs=[_const_spec((8, d)),
                  pl.BlockSpec((d, tn), lambda j: (0, j)),
                  pl.BlockSpec((1, tn), lambda j: (0, j))],
        out_specs=pl.BlockSpec((8, tn), lambda j: (0, j)),
        out_shape=jax.ShapeDtypeStruct((8, n), F32),
        compiler_params=_cparams(1),
        name="mod",
    )(c8, w_mod, b_mod.reshape(1, n))


class _Groups:
    def __init__(self, groups):
        self.groups = groups
        self.n_tokens = sum(b * s for b, s in groups)

    def tile_maps(self, t):
        (b0, s0), (b1, s1) = self.groups
        n0 = b0 * s0 // t
        t0, t1 = s0 // t, s1 // t

        def bidx(i):
            return jnp.where(i < n0, i // t0, b0 + (i - n0) // t1)

        def pidx(i):
            return jnp.where(i < n0, i % t0, (i - n0) % t1)

        return bidx, pidx


def _rope_tables(seq_len):
    pos = np.arange(seq_len)
    row = (pos // GRID_W).astype(np.float64)
    col = (pos % GRID_W).astype(np.float64)
    n_pairs = HEAD_DIM // 4
    freqs = ROPE_THETA ** (-np.arange(n_pairs, dtype=np.float64) / n_pairs)
    ang = np.concatenate([row[:, None] * freqs, col[:, None] * freqs], axis=-1)
    c, s = np.cos(ang), np.sin(ang)
    cos2 = np.concatenate([c, c], axis=-1)
    sin2 = np.concatenate([-s, s], axis=-1)
    return cos2, sin2


def _inproj_kernel(x_ref, mod_ref, g1_ref, wqt_ref, wk_ref, wvt_ref, wu_ref, wgv_ref,
                   qg_ref, kg_ref, cost_ref, sint_ref, cos_ref, sin_ref,
                   qt_ref, k_ref, vt_ref, u_ref, gv_ref, km_ref):
    x = x_ref[...]
    shift1 = mod_ref[0:1, :]
    scale1 = mod_ref[1:2, :]
    ms = jnp.mean(x * x, axis=-1, keepdims=True)
    h = (x * lax.rsqrt(ms + EPS) * g1_ref[...]) * (1.0 + scale1) + shift1
    hb = h.astype(BF16)
    half = HEAD_DIM // 2

    qt = lax.dot_general(wqt_ref[...], hb, NT_DIMS, preferred_element_type=F32)
    cost = cost_ref[...]
    sint = sint_ref[...]
    qg = qg_ref[...]
    for hh in range(N_Q_HEADS):
        qh = qt[hh * HEAD_DIM:(hh + 1) * HEAD_DIM, :]
        msq = jnp.mean(qh * qh, axis=0, keepdims=True)
        qn = qh * lax.rsqrt(msq + EPS) * qg
        rot = jnp.concatenate([qn[half:, :], qn[:half, :]], axis=0)
        qt_ref[hh * HEAD_DIM:(hh + 1) * HEAD_DIM, :] = (qn * cost + rot * sint).astype(BF16)

    vt_ref[...] = lax.dot_general(wvt_ref[...], hb, NT_DIMS, preferred_element_type=F32).astype(BF16)

    kk = jnp.dot(hb, wk_ref[...], preferred_element_type=F32)
    cos = cos_ref[...]
    sin = sin_ref[...]
    kg = kg_ref[...]
    kmax2 = None
    for j in range(N_KV_HEADS):
        kh = kk[:, j * HEAD_DIM:(j + 1) * HEAD_DIM]
        msk = jnp.mean(kh * kh, axis=-1, keepdims=True)
        kn = kh * lax.rsqrt(msk + EPS) * kg
        rot = jnp.concatenate([kn[:, half:], kn[:, :half]], axis=1)
        kb = (kn * cos + rot * sin).astype(BF16)
        k_ref[:, j * HEAD_DIM:(j + 1) * HEAD_DIM] = kb
        kf = kb.astype(F32)
        n2 = jnp.max(jnp.sum(kf * kf, axis=-1, keepdims=True), axis=0, keepdims=True)
        kmax2 = n2 if kmax2 is None else jnp.maximum(kmax2, n2)

    @pl.when(pl.program_id(0) == 0)
    def _():
        km_ref[...] = jnp.zeros(km_ref.shape, F32)

    km_ref[...] = jnp.maximum(km_ref[...], kmax2)

    u_ref[...] = jnp.dot(hb, wu_ref[...], preferred_element_type=F32).astype(BF16)
    gv_ref[...] = jnp.dot(hb, wgv_ref[...], preferred_element_type=F32).astype(BF16)


def _inproj(x, modp, g1, wqt, wk, wvt, wu, wgv, qg, kg, tabs, grp, tm):
    n, d = x.shape
    bidx, pidx = grp.tile_maps(tm)
    cost, sint, cos, sin = tabs
    tok = lambda w: pl.BlockSpec((tm, w), lambda i: (i, 0))
    tokt = lambda w: pl.BlockSpec((w, tm), lambda i: (0, i))
    return pl.pallas_call(
        _inproj_kernel,
        grid=(n // tm,),
        in_specs=[tok(d),
                  pl.BlockSpec((None, 8, d), lambda i: (bidx(i), 0, 0)),
                  _const_spec((1, d)),
                  _const_spec(wqt.shape), _const_spec(wk.shape), _const_spec(wvt.shape),
                  _const_spec(wu.shape), _const_spec(wgv.shape),
                  _const_spec((HEAD_DIM, 1)), _const_spec((1, HEAD_DIM)),
                  pl.BlockSpec((HEAD_DIM, tm), lambda i: (0, pidx(i))),
                  pl.BlockSpec((HEAD_DIM, tm), lambda i: (0, pidx(i))),
                  pl.BlockSpec((tm, HEAD_DIM), lambda i: (pidx(i), 0)),
                  pl.BlockSpec((tm, HEAD_DIM), lambda i: (pidx(i), 0))],
        out_specs=[tokt(ATTN_WIDTH), tok(KV_WIDTH), tokt(KV_WIDTH), tok(A_WIDTH), tok(A_WIDTH),
                   _const_spec((8, 128))],
        out_shape=[jax.ShapeDtypeStruct((ATTN_WIDTH, n), BF16),
                   jax.ShapeDtypeStruct((n, KV_WIDTH), BF16),
                   jax.ShapeDtypeStruct((KV_WIDTH, n), BF16),
                   jax.ShapeDtypeStruct((n, A_WIDTH), BF16),
                   jax.ShapeDtypeStruct((n, A_WIDTH), BF16),
                   jax.ShapeDtypeStruct((8, 128), F32)],
        compiler_params=_cparams(1),
        name="inproj",
    )(x, modp, g1, wqt, wk, wvt, wu, wgv, qg, kg, cost, sint, cos, sin)


LOG2E = 1.4426950408889634
SAFE_SCORE_BOUND = 60.0
BOUND_SLACK = 1.01


def _flash_kernel(qt_s, kt_s, vp_s, fl_s, safe_s, q_ref, k_ref, v_ref, vp_ref, km_ref, o_ref,
                  qs_sc, m_sc, l_sc, acc_sc, p_sc, *, bq, hb):
    s_id = pl.program_id(1)
    flag = fl_s[s_id]
    safe = safe_s[0] != 0
    last = (flag & 2) != 0

    @pl.when((flag & 1) != 0)
    def _init():
        l_sc[...] = jnp.zeros(l_sc.shape, F32)
        acc_sc[...] = jnp.zeros(acc_sc.shape, F32)
        p_sc[...] = jnp.zeros(p_sc.shape, BF16)
        for hh in range(Q_PER_KV):
            qs_sc[:, hh * bq:(hh + 1) * bq] = q_ref[hh * HEAD_DIM:(hh + 1) * HEAD_DIM, :]
        qf = qs_sc[...].astype(F32)
        qn2 = jnp.sum(qf * qf, axis=0, keepdims=True)
        bound = jnp.sqrt(qn2 * km_ref[0:1, 0:1]) * BOUND_SLACK
        m_sc[...] = jnp.where(safe, bound, -jnp.inf)

    @pl.when(safe)
    def _fast():
        q = qs_sc[...]
        m = m_sc[...]
        sa = jnp.dot(k_ref[:hb, :], q, preferred_element_type=F32)
        acc = acc_sc[...] + jnp.dot(vp_ref[...], p_sc[...], preferred_element_type=F32)
        pa = jnp.exp2(sa - m)
        la = jnp.sum(pa, axis=0, keepdims=True)
        sb = jnp.dot(k_ref[hb:, :], q, preferred_element_type=F32)
        acc_sc[...] = acc + jnp.dot(v_ref[:, :hb], pa.astype(BF16), preferred_element_type=F32)
        pb = jnp.exp2(sb - m)
        l_sc[...] += la + jnp.sum(pb, axis=0, keepdims=True)
        p_sc[...] = pb.astype(BF16)

    @pl.when(jnp.logical_and(safe, last))
    def _flush():
        acc_sc[...] += jnp.dot(v_ref[:, hb:], p_sc[...], preferred_element_type=F32)

    @pl.when(jnp.logical_not(safe))
    def _online():
        s = jnp.dot(k_ref[...], qs_sc[...], preferred_element_type=F32)
        m_prev = m_sc[...]
        m_new = jnp.maximum(m_prev, jnp.max(s, axis=0, keepdims=True))
        alpha = jnp.exp2(m_prev - m_new)
        p = jnp.exp2(s - m_new)
        l_sc[...] = alpha * l_sc[...] + jnp.sum(p, axis=0, keepdims=True)
        acc_sc[...] = alpha * acc_sc[...] + jnp.dot(v_ref[...], p.astype(BF16), preferred_element_type=F32)
        m_sc[...] = m_new

    @pl.when((flag & 2) != 0)
    def _fin():
        o = acc_sc[...] / l_sc[...]
        for hh in range(Q_PER_KV):
            o_ref[:, hh * HEAD_DIM:(hh + 1) * HEAD_DIM] = o[:, hh * bq:(hh + 1) * bq].T.astype(BF16)


def _flash_schedule(grp, bq, bkv):
    qt, kt, vp, fl = [], [], [], []
    off = 0
    for b, s in grp.groups:
        for bi in range(b):
            base = off + bi * s
            nk = s // bkv
            for qi in range(s // bq):
                for ki in range(nk):
                    blk = base // bkv + ki
                    qt.append(base // bq + qi)
                    kt.append(blk)
                    vp.append(2 * (blk - 1 if ki else blk) + 1)
                    fl.append((1 if ki == 0 else 0) | (2 if ki == nk - 1 else 0))
        off += b * s
    return tuple(np.asarray(a, np.int32) for a in (qt, kt, vp, fl))


def _flash(qt, k, vt, kmax2, safe, grp, bq, bkv):
    n = k.shape[0]
    hb = bkv // 2
    qt_a, kt_a, vp_a, fl_a = _flash_schedule(grp, bq, bkv)
    n_steps = len(qt_a)
    w = Q_PER_KV * HEAD_DIM
    gs = pltpu.PrefetchScalarGridSpec(
        num_scalar_prefetch=5,
        grid=(N_KV_HEADS, n_steps),
        in_specs=[pl.BlockSpec((w, bq), lambda h, s, qa, ka, va, fa, sa: (h, qa[s])),
                  pl.BlockSpec((bkv, HEAD_DIM), lambda h, s, qa, ka, va, fa, sa: (ka[s], h)),
                  pl.BlockSpec((HEAD_DIM, bkv), lambda h, s, qa, ka, va, fa, sa: (h, ka[s])),
                  pl.BlockSpec((HEAD_DIM, hb), lambda h, s, qa, ka, va, fa, sa: (h, va[s])),
                  pl.BlockSpec((8, 128), lambda h, s, qa, ka, va, fa, sa: (0, 0))],
        out_specs=pl.BlockSpec((bq, w), lambda h, s, qa, ka, va, fa, sa: (qa[s], h)),
        scratch_shapes=[pltpu.VMEM((HEAD_DIM, Q_PER_KV * bq), BF16),
                        pltpu.VMEM((1, Q_PER_KV * bq), F32),
                        pltpu.VMEM((1, Q_PER_KV * bq), F32),
                        pltpu.VMEM((HEAD_DIM, Q_PER_KV * bq), F32),
                        pltpu.VMEM((hb, Q_PER_KV * bq), BF16)])
    return pl.pallas_call(
        functools.partial(_flash_kernel, bq=bq, hb=hb),
        grid_spec=gs,
        out_shape=jax.ShapeDtypeStruct((n, ATTN_WIDTH), BF16),
        compiler_params=_cparams(2),
        name="flash",
    )(jnp.asarray(qt_a), jnp.asarray(kt_a), jnp.asarray(vp_a), jnp.asarray(fl_a), safe, qt, k, vt, vt, kmax2)


def _sgu_kernel(u_ref, gv_ref, lng_ref, lnb_ref, ws_ref, bst_ref, og_ref, o_ref, *, n_chunks):
    v = gv_ref[...].astype(F32)
    mean = jnp.mean(v, axis=-1, keepdims=True)
    vc = v - mean
    var = jnp.mean(vc * vc, axis=-1, keepdims=True)
    vn = (vc * lax.rsqrt(var + EPS) * lng_ref[...] + lnb_ref[...]).astype(BF16)
    u = u_ref[...].astype(F32)
    heads = []
    for hh in range(N_A_HEADS):
        cols = slice(hh * A_HEAD_DIM, (hh + 1) * A_HEAD_DIM)
        rhs = jnp.concatenate([vn[c * CHUNK:(c + 1) * CHUNK, cols] for c in range(n_chunks)], axis=1)
        mixed = jnp.dot(ws_ref[hh], rhs, preferred_element_type=F32) + bst_ref[:, hh:hh + 1]
        heads.append(jnp.concatenate(
            [mixed[:, c * A_HEAD_DIM:(c + 1) * A_HEAD_DIM] for c in range(n_chunks)], axis=0))
    sg = u * jnp.concatenate(heads, axis=1)
    ms = jnp.mean(sg * sg, axis=-1, keepdims=True)
    o_ref[...] = (sg * lax.rsqrt(ms + EPS) * og_ref[...]).astype(BF16)


def _sgu(u, gv, lng, lnb, ws, bst, og, tm):
    n = u.shape[0]
    tok = pl.BlockSpec((tm, A_WIDTH), lambda i: (i, 0))
    return pl.pallas_call(
        functools.partial(_sgu_kernel, n_chunks=tm // CHUNK),
        grid=(n // tm,),
        in_specs=[tok, tok, _const_spec((1, A_WIDTH)), _const_spec((1, A_WIDTH)),
                  _const_spec(ws.shape), _const_spec(bst.shape), _const_spec((1, A_WIDTH))],
        out_specs=tok,
        out_shape=jax.ShapeDtypeStruct((n, A_WIDTH), BF16),
        compiler_params=_cparams(1),
        name="sgu",
    )(u, gv, lng, lnb, ws, bst, og)


def _outproj_kernel(attn_ref, sgu_ref, x_ref, mod_ref, ag_ref, wo_ref, g2_ref, wrt_ref,
                    x1_ref, h2_ref, lgt_ref):
    a = attn_ref[...].astype(F32)
    ms = jnp.mean(a * a, axis=-1, keepdims=True)
    an = (a * lax.rsqrt(ms + EPS) * ag_ref[...]).astype(BF16)
    o = jnp.dot(an, wo_ref[:ATTN_WIDTH, :], preferred_element_type=F32)
    o = o + jnp.dot(sgu_ref[...], wo_ref[ATTN_WIDTH:, :], preferred_element_type=F32)
    gate1 = mod_ref[2:3, :]
    shift2 = mod_ref[3:4, :]
    scale2 = mod_ref[4:5, :]
    x1 = x_ref[...] + gate1 * o
    x1_ref[...] = x1
    ms2 = jnp.mean(x1 * x1, axis=-1, keepdims=True)
    h2 = (x1 * lax.rsqrt(ms2 + EPS) * g2_ref[...]) * (1.0 + scale2) + shift2
    h2_ref[...] = h2
    lgt_ref[...] = lax.dot_general(wrt_ref[...], h2.astype(BF16), NT_DIMS, preferred_element_type=F32)


def _outproj(attn, sgun, x, modp, ag, wo, g2, wrt, grp, tm):
    n, d = x.shape
    bidx, _ = grp.tile_maps(tm)
    tok = lambda w: pl.BlockSpec((tm, w), lambda i: (i, 0))
    return pl.pallas_call(
        _outproj_kernel,
        grid=(n // tm,),
        in_specs=[tok(ATTN_WIDTH), tok(A_WIDTH), tok(d),
                  pl.BlockSpec((None, 8, d), lambda i: (bidx(i), 0, 0)),
                  _const_spec((1, ATTN_WIDTH)), _const_spec(wo.shape), _const_spec((1, d)),
                  _const_spec(wrt.shape)],
        out_specs=[tok(d), tok(d), pl.BlockSpec((N_EXPERTS, tm), lambda i: (0, i))],
        out_shape=[jax.ShapeDtypeStruct((n, d), F32), jax.ShapeDtypeStruct((n, d), F32),
                   jax.ShapeDtypeStruct((N_EXPERTS, n), F32)],
        compiler_params=_cparams(1),
        name="outproj",
    )(attn, sgun, x, modp, ag, wo, g2, wrt)


def _first_index(hit, iota, sentinel):
    return jnp.min(jnp.where(hit, iota, sentinel), axis=0, keepdims=True)


def _router_kernel(lg_ref, bias_ref, idx_ref, w_ref, rank_ref, cnt_ref, run_sc):
    i = pl.program_id(0)
    tt = lg_ref.shape[1]

    @pl.when(i == 0)
    def _init():
        run_sc[...] = jnp.zeros(run_sc.shape, F32)

    scores = jax.nn.sigmoid(lg_ref[...])
    biased = scores + bias_ref[...]
    neg = -jnp.inf
    iota_g = lax.broadcasted_iota(I32, (GROUP_SIZE, tt), 0).astype(F32)

    gs_rows = []
    for g in range(N_GROUPS):
        xg = biased[g * GROUP_SIZE:(g + 1) * GROUP_SIZE, :]
        m1 = jnp.max(xg, axis=0, keepdims=True)
        f1 = _first_index(xg == m1, iota_g, float(GROUP_SIZE))
        m2 = jnp.max(jnp.where(iota_g == f1, neg, xg), axis=0, keepdims=True)
        gs_rows.append(m1 + m2)
    gs = jnp.concatenate(gs_rows, axis=0)

    iota_n = lax.broadcasted_iota(I32, (N_GROUPS, tt), 0).astype(F32)
    gsel = jnp.zeros((N_GROUPS, tt), F32)
    cur = gs
    for _ in range(TOPK_GROUPS):
        m = jnp.max(cur, axis=0, keepdims=True)
        f = _first_index(cur == m, iota_n, float(N_GROUPS))
        hit = iota_n == f
        gsel = jnp.where(hit, 1.0, gsel)
        cur = jnp.where(hit, neg, cur)

    masked = jnp.concatenate(
        [jnp.where(gsel[g:g + 1, :] > 0.5, biased[g * GROUP_SIZE:(g + 1) * GROUP_SIZE, :], neg)
         for g in range(N_GROUPS)], axis=0)

    iota_e = lax.broadcasted_iota(I32, (N_EXPERTS, tt), 0).astype(F32)
    sel = jnp.zeros((N_EXPERTS, tt), F32)
    idx_rows, w_rows = [], []
    for _ in range(TOP_K):
        m = jnp.max(masked, axis=0, keepdims=True)
        f = _first_index(masked == m, iota_e, float(N_EXPERTS))
        hit = iota_e == f
        idx_rows.append(f)
        w_rows.append(jnp.sum(jnp.where(hit, scores, 0.0), axis=0, keepdims=True))
        sel = jnp.where(hit, 1.0, sel)
        masked = jnp.where(hit, neg, masked)
    idx = jnp.concatenate(idx_rows, axis=0)
    wk = jnp.concatenate(w_rows, axis=0)
    wk = wk / (jnp.sum(wk, axis=0, keepdims=True) + 1e-20) * ROUTED_SCALE

    r = lax.broadcasted_iota(I32, (tt, tt), 0)
    c = lax.broadcasted_iota(I32, (tt, tt), 1)
    upper = jnp.where(r < c, 1.0, 0.0).astype(BF16)
    prefix = jnp.dot(sel.astype(BF16), upper, preferred_element_type=F32) + run_sc[...]
    rank_rows = [jnp.sum(jnp.where(iota_e == idx[k:k + 1, :], prefix, 0.0), axis=0, keepdims=True)
                 for k in range(TOP_K)]
    run_sc[...] = run_sc[...] + jnp.sum(sel, axis=1, keepdims=True)

    idx_ref[...] = idx.astype(I32)
    w_ref[...] = wk
    rank_ref[...] = jnp.concatenate(rank_rows, axis=0).astype(I32)
    cnt_ref[...] = jnp.broadcast_to(run_sc[...], cnt_ref.shape).astype(I32)


def _router(lgt, bias, tt):
    n = lgt.shape[1]
    tokk = pl.BlockSpec((TOP_K, tt), lambda i: (0, i))
    return pl.pallas_call(
        _router_kernel,
        grid=(n // tt,),
        in_specs=[pl.BlockSpec((N_EXPERTS, tt), lambda i: (0, i)), _const_spec((N_EXPERTS, 1))],
        out_specs=[tokk, tokk, tokk, _const_spec((N_EXPERTS, 128))],
        out_shape=[jax.ShapeDtypeStruct((TOP_K, n), I32), jax.ShapeDtypeStruct((TOP_K, n), F32),
                   jax.ShapeDtypeStruct((TOP_K, n), I32), jax.ShapeDtypeStruct((N_EXPERTS, 128), I32)],
        scratch_shapes=[pltpu.VMEM((N_EXPERTS, 1), F32)],
        compiler_params=_cparams(1),
        name="router",
    )(lgt, bias)


def _dest_kernel(idx_ref, rank_ref, start_ref, o_ref):
    tt = idx_ref.shape[1]
    iota_e = lax.broadcasted_iota(I32, (N_EXPERTS, tt), 0)
    start = start_ref[...]
    rows = []
    for k in range(TOP_K):
        hit = iota_e == idx_ref[k:k + 1, :]
        rows.append(jnp.sum(jnp.where(hit, start, 0), axis=0, keepdims=True))
    o_ref[...] = jnp.concatenate(rows, axis=0) + rank_ref[...]


def _dest(idx, rank, pad_start, tt):
    n = idx.shape[1]
    tokk = pl.BlockSpec((TOP_K, tt), lambda i: (0, i))
    return pl.pallas_call(
        _dest_kernel,
        grid=(n // tt,),
        in_specs=[tokk, tokk, _const_spec((N_EXPERTS, 1))],
        out_specs=tokk,
        out_shape=jax.ShapeDtypeStruct((TOP_K, n), I32),
        compiler_params=_cparams(1),
        name="dest",
    )(idx, rank, pad_start.reshape(N_EXPERTS, 1))


def _dispatch_kernel(cnt_s, start_s, pend_s, dest_ref, h_ref, xs_ref, zero_sc, sem, *, td):
    i = pl.program_id(0)

    def row_copy(t, k):
        return pltpu.make_async_copy(h_ref.at[pl.ds(t, 1)], xs_ref.at[pl.ds(dest_ref[k, t], 1)], sem)

    def issue(t, c):
        for k in range(TOP_K):
            row_copy(t, k).start()
        return c

    lax.fori_loop(0, td, issue, 0)

    def drain(t, c):
        for k in range(TOP_K):
            row_copy(t, k).wait()
        return c

    lax.fori_loop(0, td, drain, 0)

    @pl.when(i == pl.num_programs(0) - 1)
    def _pad():
        zero_sc[...] = jnp.zeros(zero_sc.shape, F32)

        def pad_copy(p):
            return pltpu.make_async_copy(zero_sc, xs_ref.at[pl.ds(p, 1)], sem)

        def per_expert(e, c):
            lo = start_s[e] + cnt_s[e]
            hi = pend_s[e]

            def go(p, c2):
                pad_copy(p).start()
                return c2

            lax.fori_loop(lo, hi, go, 0)

            def done(p, c2):
                pad_copy(p).wait()
                return c2

            lax.fori_loop(lo, hi, done, 0)
            return c

        lax.fori_loop(0, N_EXPERTS, per_expert, 0)


def _dispatch(counts, pad_start, pad_end, dest3, h2, p_rows, td):
    n, d = h2.shape
    gs = pltpu.PrefetchScalarGridSpec(
        num_scalar_prefetch=3,
        grid=(n // td,),
        in_specs=[pl.BlockSpec((None, TOP_K, td), lambda i, *_: (i, 0, 0), memory_space=pltpu.SMEM),
                  pl.BlockSpec((td, d), lambda i, *_: (i, 0))],
        out_specs=pl.BlockSpec(memory_space=pl.ANY),
        scratch_shapes=[pltpu.VMEM((1, d), F32), pltpu.SemaphoreType.DMA(())])
    return pl.pallas_call(
        functools.partial(_dispatch_kernel, td=td),
        grid_spec=gs,
        out_shape=jax.ShapeDtypeStruct((p_rows, d), F32),
        compiler_params=_cparams(1),
        name="dispatch",
    )(counts, pad_start, pad_end, dest3, h2)


def _swiglu(xb, wg, wu, wd):
    a = jnp.dot(xb, wg, preferred_element_type=F32)
    b = jnp.dot(xb, wu, preferred_element_type=F32)
    act = (a * jax.nn.sigmoid(a) * b).astype(BF16)
    return jnp.dot(act, wd, preferred_element_type=F32)


def _shared_kernel(h_ref, wg_ref, wu_ref, wd_ref, o_ref):
    o_ref[...] = _swiglu(h_ref[...].astype(BF16), wg_ref[...], wu_ref[...], wd_ref[...])


def _shared(h2, wg, wu, wd, tm):
    n, d = h2.shape
    tok = pl.BlockSpec((tm, d), lambda i: (i, 0))
    return pl.pallas_call(
        _shared_kernel,
        grid=(n // tm,),
        in_specs=[tok, _const_spec(wg.shape), _const_spec(wu.shape), _const_spec(wd.shape)],
        out_specs=tok,
        out_shape=jax.ShapeDtypeStruct((n, d), F32),
        compiler_params=_cparams(1),
        name="shared",
    )(h2, wg, wu, wd)


def _experts_kernel(be_s, nu_s, x_ref, wg_ref, wu_ref, wd_ref, o_ref, wg_sc, wu_sc, wd_sc):
    i = pl.program_id(0)
    used = i < nu_s[0]
    prev = be_s[jnp.maximum(i - 1, 0)]
    fresh = jnp.logical_or(i == 0, be_s[i] != prev)

    @pl.when(jnp.logical_and(used, fresh))
    def _cast():
        wg_sc[...] = wg_ref[...].astype(BF16)
        wu_sc[...] = wu_ref[...].astype(BF16)
        wd_sc[...] = wd_ref[...].astype(BF16)

    @pl.when(used)
    def _run():
        o_ref[...] = _swiglu(x_ref[...].astype(BF16), wg_sc[...], wu_sc[...], wd_sc[...])


def _experts(block_exp, n_used, xs, w_gate, w_up, w_down, bm):
    p_rows, d = xs.shape
    de = w_gate.shape[2]
    n_blocks = p_rows // bm
    row = lambda i, be, nu: (jnp.minimum(i, nu[0] - 1), 0)
    wsel = lambda i, be, nu: (be[jnp.minimum(i, nu[0] - 1)], 0, 0)
    gs = pltpu.PrefetchScalarGridSpec(
        num_scalar_prefetch=2,
        grid=(n_blocks,),
        in_specs=[pl.BlockSpec((bm, d), row),
                  pl.BlockSpec((None, d, de), wsel),
                  pl.BlockSpec((None, d, de), wsel),
                  pl.BlockSpec((None, de, d), wsel)],
        out_specs=pl.BlockSpec((bm, d), row),
        scratch_shapes=[pltpu.VMEM((d, de), BF16), pltpu.VMEM((d, de), BF16), pltpu.VMEM((de, d), BF16)])
    return pl.pallas_call(
        _experts_kernel,
        grid_spec=gs,
        out_shape=jax.ShapeDtypeStruct((p_rows, d), F32),
        compiler_params=_cparams(1),
        name="experts",
    )(block_exp, n_used, xs, w_gate, w_up, w_down)


def _combine_kernel(dest_ref, w_ref, x1_ref, sh_ref, mod_ref, ys_ref, o_ref, buf, sem, *, tc):
    def row_copy(t, k):
        return pltpu.make_async_copy(ys_ref.at[pl.ds(dest_ref[k, t], 1)], buf.at[k, pl.ds(t, 1)], sem)

    def issue(t, c):
        for k in range(TOP_K):
            row_copy(t, k).start()
        return c

    lax.fori_loop(0, tc, issue, 0)

    def drain(t, c):
        for k in range(TOP_K):
            row_copy(t, k).wait()
        return c

    lax.fori_loop(0, tc, drain, 0)

    w = w_ref[...]
    routed = w[:, 0:1] * buf[0]
    for k in range(1, TOP_K):
        routed = routed + w[:, k:k + 1] * buf[k]
    gate2 = mod_ref[5:6, :]
    o_ref[...] = x1_ref[...] + gate2 * (sh_ref[...] + routed)


def _combine(dest3, wt, x1, shared, modp, ys, grp, tc):
    n, d = x1.shape
    bidx, _ = grp.tile_maps(tc)
    tok = pl.BlockSpec((tc, d), lambda i: (i, 0))
    return pl.pallas_call(
        functools.partial(_combine_kernel, tc=tc),
        grid=(n // tc,),
        in_specs=[pl.BlockSpec((None, TOP_K, tc), lambda i: (i, 0, 0), memory_space=pltpu.SMEM),
                  pl.BlockSpec((tc, TOP_K), lambda i: (i, 0)),
                  tok, tok,
                  pl.BlockSpec((None, 8, d), lambda i: (bidx(i), 0, 0)),
                  pl.BlockSpec(memory_space=pl.ANY)],
        out_specs=tok,
        out_shape=jax.ShapeDtypeStruct((n, d), F32),
        scratch_shapes=[pltpu.VMEM((TOP_K, tc, d), F32), pltpu.SemaphoreType.DMA(())],
        compiler_params=_cparams(1),
        name="combine",
    )(dest3, wt, x1, shared, modp, ys)


def _tok3(a, t):
    k, n = a.shape
    return a.reshape(k, n // t, t).transpose(1, 0, 2)


def _layer(x, c, grp, norm1_g, w_mod, b_mod, w_in, q_norm_g, k_norm_g, sgu_ln_g, sgu_ln_b, w_s, b_s,
           attn_out_g, sgu_out_g, w_out, norm2_g, w_router, router_bias, w_gate, w_up, w_down,
           ws_gate, ws_up, ws_down):
    n, d = x.shape
    nb = c.shape[0]
    s_min = min(s for _, s in grp.groups)
    s_max = max(s for _, s in grp.groups)

    c8 = jnp.pad(c, ((0, 8 - nb), (0, 0)))
    mod = _mod(c8, w_mod, b_mod)[:nb].reshape(nb, N_MOD, d)
    modp = jnp.pad(mod, ((0, 0), (0, 8 - N_MOD), (0, 0)))

    perm = np.concatenate([np.arange(0, HEAD_DIM, 2), np.arange(1, HEAD_DIM, 2)])
    c0, c1, c2, c3 = ATTN_WIDTH, ATTN_WIDTH + KV_WIDTH, ATTN_WIDTH + 2 * KV_WIDTH, ATTN_WIDTH + 2 * KV_WIDTH + A_WIDTH
    wq = w_in[:, :c0].reshape(d, N_Q_HEADS, HEAD_DIM)[:, :, perm].reshape(d, ATTN_WIDTH)
    wk = w_in[:, c0:c1].reshape(d, N_KV_HEADS, HEAD_DIM)[:, :, perm].reshape(d, KV_WIDTH)
    wqt = wq.T.astype(BF16)
    wkb = wk.astype(BF16)
    wvt = w_in[:, c1:c2].T.astype(BF16)
    wu = w_in[:, c2:c3].astype(BF16)
    wgv = w_in[:, c3:].astype(BF16)
    qg = q_norm_g[perm].reshape(HEAD_DIM, 1)
    kg = k_norm_g[perm].reshape(1, HEAD_DIM)

    cos2, sin2 = _rope_tables(s_max)
    scale = HEAD_DIM ** -0.5 * LOG2E
    tabs = (jnp.asarray((cos2 * scale).T, F32), jnp.asarray((sin2 * scale).T, F32),
            jnp.asarray(cos2, F32), jnp.asarray(sin2, F32))

    tm = _tile(s_min, 512)
    qt, k, vt, u, gv, kmax2 = _inproj(x, modp, norm1_g.reshape(1, d), wqt, wkb, wvt, wu, wgv, qg, kg, tabs,
                                      grp, tm)

    q_bound = math.sqrt(HEAD_DIM) * scale * BOUND_SLACK * jnp.max(jnp.abs(q_norm_g))
    safe = (q_bound * jnp.sqrt(kmax2[0, 0]) * BOUND_SLACK < SAFE_SCORE_BOUND).astype(I32).reshape(1)
    attn = _flash(qt, k, vt, kmax2, safe, grp, _tile(s_min, 256), _tile(s_min, 1024))
    sgun = _sgu(u, gv, sgu_ln_g.reshape(1, A_WIDTH), sgu_ln_b.reshape(1, A_WIDTH), w_s.astype(BF16),
                b_s.T, sgu_out_g.reshape(1, A_WIDTH), _tile(s_min, 512))

    x1, h2, lgt = _outproj(attn, sgun, x, modp, attn_out_g.reshape(1, ATTN_WIDTH), w_out.astype(BF16),
                           norm2_g.reshape(1, d), w_router.T.astype(BF16), grp, _tile(s_min, 256))

    idx, wts, rank, cnt = _router(lgt, router_bias.reshape(N_EXPERTS, 1), _tile(s_min, 512))

    bm = 256
    counts = cnt[:, 0]
    padded = (counts + bm - 1) // bm * bm
    pad_end = jnp.cumsum(padded).astype(I32)
    pad_start = pad_end - padded
    n_blocks = n * TOP_K // bm + N_EXPERTS
    p_rows = n_blocks * bm
    n_used = (pad_end[-1] // bm).reshape(1).astype(I32)
    block_start = jnp.arange(n_blocks, dtype=I32) * bm
    block_exp = jnp.minimum(jnp.sum((pad_end[None, :] <= block_start[:, None]).astype(I32), axis=1),
                            N_EXPERTS - 1)

    dest = _dest(idx, rank, pad_start, _tile(s_min, 2048))

    td = _tile(s_min, 256)
    xs = _dispatch(counts, pad_start, pad_end, _tok3(dest, td), h2, p_rows, td)
    shared = _shared(h2, ws_gate.astype(BF16), ws_up.astype(BF16), ws_down.astype(BF16), _tile(s_min, 512))
    ys = _experts(block_exp, n_used, xs, w_gate, w_up, w_down, bm)

    tc = _tile(s_min, 128)
    return _combine(_tok3(dest, tc), wts.T, x1, shared, modp, ys, grp, tc)


def kernel(x_prompt, x_sample, c_prompt, c_sample, norm1_g, w_mod, b_mod, w_in, q_norm_g, k_norm_g, sgu_ln_g, sgu_ln_b, w_s, b_s, attn_out_g, sgu_out_g, w_out, norm2_g, w_router, router_bias, w_gate, w_up, w_down, ws_gate, ws_up, ws_down):
    depth = norm1_g.shape[0]
    bp, sp, d = x_prompt.shape
    bs, ss, _ = x_sample.shape
    grp = _Groups([(bp, sp), (bs, ss)])
    x = jnp.concatenate([x_prompt.reshape(bp * sp, d), x_sample.reshape(bs * ss, d)], axis=0)
    c = jnp.concatenate([c_prompt, c_sample], axis=0)
    for l in range(depth):
        x = _layer(x, c, grp, norm1_g[l], w_mod[l], b_mod[l], w_in[l], q_norm_g[l], k_norm_g[l],
                   sgu_ln_g[l], sgu_ln_b[l], w_s[l], b_s[l], attn_out_g[l], sgu_out_g[l], w_out[l],
                   norm2_g[l], w_router[l], router_bias[l], w_gate[l], w_up[l], w_down[l],
                   ws_gate[l], ws_up[l], ws_down[l])
    return (x[:bp * sp].reshape(bp, sp, d), x[bp * sp:].reshape(bs, ss, d))
```

```python
import functools
import math

import numpy as np
import jax
import jax.numpy as jnp
from jax import lax
from jax.experimental import pallas as pl
from jax.experimental.pallas import tpu as pltpu

F32 = jnp.float32
BF16 = jnp.bfloat16
I32 = jnp.int32

HEAD_DIM = 128
N_Q_HEADS = 8
N_KV_HEADS = 2
Q_PER_KV = N_Q_HEADS // N_KV_HEADS
ATTN_WIDTH = N_Q_HEADS * HEAD_DIM
KV_WIDTH = N_KV_HEADS * HEAD_DIM
ROPE_THETA = 10000.0
GRID_W = 64
N_A_HEADS = 8
A_HEAD_DIM = 128
A_WIDTH = N_A_HEADS * A_HEAD_DIM
CHUNK = 128
N_EXPERTS = 64
TOP_K = 8
N_GROUPS = 8
GROUP_SIZE = N_EXPERTS // N_GROUPS
TOPK_GROUPS = 4
ROUTED_SCALE = 2.5
N_MOD = 6
EPS = 1e-6

VMEM_LIMIT_BYTES = 56 * 1024 * 1024
NT_DIMS = (((1,), (1,)), ((), ()))


def _cparams(n_axes):
    return pltpu.CompilerParams(
        dimension_semantics=("arbitrary",) * n_axes, vmem_limit_bytes=VMEM_LIMIT_BYTES)


def _tile(n, pref):
    t = min(n, pref)
    while n % t:
        t //= 2
    return t


def _const_spec(shape):
    nd = len(shape)
    return pl.BlockSpec(shape, lambda *a: (0,) * nd)


def _mod_kernel(c_ref, w_ref, b_ref, o_ref):
    c = c_ref[...]
    a = (c * jax.nn.sigmoid(c)).astype(BF16)
    o_ref[...] = jnp.dot(a, w_ref[...].astype(BF16), preferred_element_type=F32) + b_ref[...]


def _mod(c8, w_mod, b_mod):
    d, n = w_mod.shape
    tn = _tile(n, 1024)
    return pl.pallas_call(
        _mod_kernel,
        grid=(n // tn,),
        in_specs=[_const_spec((8, d)),
                  pl.BlockSpec((d, tn), lambda j: (0, j)),
                  pl.BlockSpec((1, tn), lambda j: (0, j))],
        out_specs=pl.BlockSpec((8, tn), lambda j: (0, j)),
        out_shape=jax.ShapeDtypeStruct((8, n), F32),
        compiler_params=_cparams(1),
        name="mod",
    )(c8, w_mod, b_mod.reshape(1, n))


class _Groups:
    def __init__(self, groups):
        self.groups = groups
        self.n_tokens = sum(b * s for b, s in groups)

    def tile_maps(self, t):
        (b0, s0), (b1, s1) = self.groups
        n0 = b0 * s0 // t
        t0, t1 = s0 // t, s1 // t

        def bidx(i):
            return jnp.where(i < n0, i // t0, b0 + (i - n0) // t1)

        def pidx(i):
            return jnp.where(i < n0, i % t0, (i - n0) % t1)

        return bidx, pidx

    def split_specs(self, t, d):
        (b0, s0), _ = self.groups
        n0 = b0 * s0 // t
        return (pl.BlockSpec((t, d), lambda i, *_: (jnp.minimum(i, n0 - 1), 0)),
                pl.BlockSpec((t, d), lambda i, *_: (jnp.maximum(i - n0, 0), 0)), n0)


def _pick_group(i, n0, a_ref, b_ref):
    return jnp.where(i < n0, a_ref[...], b_ref[...])


LANES = 128
SUBLANES = 8


def _slab_rows(d):
    rows = d // LANES
    assert d % LANES == 0 and rows % SUBLANES == 0, d
    return rows


def _load_slabs(ref, n_tok):
    rows = ref.shape[0] // n_tok
    return jnp.concatenate([ref[pl.ds(r, n_tok, stride=rows), :] for r in range(rows)], axis=1)


def _store_slabs(ref, x):
    n_tok, d = x.shape
    rows = _slab_rows(d)
    for r in range(rows):
        ref[pl.ds(r, n_tok, stride=rows), :] = x[:, r * LANES:(r + 1) * LANES]


def _rope_tables(seq_len):
    pos = np.arange(seq_len)
    row = (pos // GRID_W).astype(np.float64)
    col = (pos % GRID_W).astype(np.float64)
    n_pairs = HEAD_DIM // 4
    freqs = ROPE_THETA ** (-np.arange(n_pairs, dtype=np.float64) / n_pairs)
    ang = np.concatenate([row[:, None] * freqs, col[:, None] * freqs], axis=-1)
    c, s = np.cos(ang), np.sin(ang)
    cos2 = np.concatenate([c, c], axis=-1)
    sin2 = np.concatenate([-s, s], axis=-1)
    return cos2, sin2


def _inproj_kernel(xa_ref, xb_ref, mod_ref, g1_ref, wqt_ref, wk_ref, wvt_ref, wu_ref, wgv_ref,
                   qg_ref, kg_ref, cost_ref, sint_ref, cos_ref, sin_ref,
                   qt_ref, k_ref, vt_ref, u_ref, gv_ref, *, n0):
    x = _pick_group(pl.program_id(0), n0, xa_ref, xb_ref)
    shift1 = mod_ref[0:1, :]
    scale1 = mod_ref[1:2, :]
    ms = jnp.mean(x * x, axis=-1, keepdims=True)
    h = (x * lax.rsqrt(ms + EPS) * g1_ref[...]) * (1.0 + scale1) + shift1
    hb = h.astype(BF16)
    half = HEAD_DIM // 2

    qt = lax.dot_general(wqt_ref[...], hb, NT_DIMS, preferred_element_type=F32)
    cost = cost_ref[...]
    sint = sint_ref[...]
    qg = qg_ref[...]
    for hh in range(N_Q_HEADS):
        qh = qt[hh * HEAD_DIM:(hh + 1) * HEAD_DIM, :]
        msq = jnp.mean(qh * qh, axis=0, keepdims=True)
        qn = qh * lax.rsqrt(msq + EPS) * qg
        rot = jnp.concatenate([qn[half:, :], qn[:half, :]], axis=0)
        qt_ref[hh * HEAD_DIM:(hh + 1) * HEAD_DIM, :] = (qn * cost + rot * sint).astype(BF16)

    vt_ref[...] = lax.dot_general(wvt_ref[...], hb, NT_DIMS, preferred_element_type=F32).astype(BF16)

    kk = jnp.dot(hb, wk_ref[...], preferred_element_type=F32)
    cos = cos_ref[...]
    sin = sin_ref[...]
    kg = kg_ref[...]
    for j in range(N_KV_HEADS):
        kh = kk[:, j * HEAD_DIM:(j + 1) * HEAD_DIM]
        msk = jnp.mean(kh * kh, axis=-1, keepdims=True)
        kn = kh * lax.rsqrt(msk + EPS) * kg
        rot = jnp.concatenate([kn[:, half:], kn[:, :half]], axis=1)
        k_ref[:, j * HEAD_DIM:(j + 1) * HEAD_DIM] = (kn * cos + rot * sin).astype(BF16)

    u_ref[...] = jnp.dot(hb, wu_ref[...], preferred_element_type=F32).astype(BF16)
    gv_ref[...] = jnp.dot(hb, wgv_ref[...], preferred_element_type=F32).astype(BF16)


def _inproj(xa, xb, modp, g1, wqt, wk, wvt, wu, wgv, qg, kg, tabs, grp, tm):
    n, d = grp.n_tokens, xa.shape[1]
    bidx, pidx = grp.tile_maps(tm)
    cost, sint, cos, sin = tabs
    tok = lambda w: pl.BlockSpec((tm, w), lambda i: (i, 0))
    tokt = lambda w: pl.BlockSpec((w, tm), lambda i: (0, i))
    xa_spec, xb_spec, n0 = grp.split_specs(tm, d)
    return pl.pallas_call(
        functools.partial(_inproj_kernel, n0=n0),
        grid=(n // tm,),
        in_specs=[xa_spec, xb_spec,
                  pl.BlockSpec((None, 8, d), lambda i: (bidx(i), 0, 0)),
                  _const_spec((1, d)),
                  _const_spec(wqt.shape), _const_spec(wk.shape), _const_spec(wvt.shape),
                  _const_spec(wu.shape), _const_spec(wgv.shape),
                  _const_spec((HEAD_DIM, 1)), _const_spec((1, HEAD_DIM)),
                  pl.BlockSpec((HEAD_DIM, tm), lambda i: (0, pidx(i))),
                  pl.BlockSpec((HEAD_DIM, tm), lambda i: (0, pidx(i))),
                  pl.BlockSpec((tm, HEAD_DIM), lambda i: (pidx(i), 0)),
                  pl.BlockSpec((tm, HEAD_DIM), lambda i: (pidx(i), 0))],
        out_specs=[tokt(ATTN_WIDTH), tok(KV_WIDTH), tokt(KV_WIDTH), tok(A_WIDTH), tok(A_WIDTH)],
        out_shape=[jax.ShapeDtypeStruct((ATTN_WIDTH, n), BF16),
                   jax.ShapeDtypeStruct((n, KV_WIDTH), BF16),
                   jax.ShapeDtypeStruct((KV_WIDTH, n), BF16),
                   jax.ShapeDtypeStruct((n, A_WIDTH), BF16),
                   jax.ShapeDtypeStruct((n, A_WIDTH), BF16)],
        compiler_params=_cparams(1),
        name="inproj",
    )(xa, xb, modp, g1, wqt, wk, wvt, wu, wgv, qg, kg, cost, sint, cos, sin)


LOG2E = 1.4426950408889634
SAFE_SCORE_BOUND = 60.0
BOUND_SLACK = 1.01


def _flash_kernel(qt_s, kt_s, vp_s, fl_s, safe_s, q_ref, k_ref, v_ref, vp_ref, km_ref, o_ref,
                  qs_sc, m_sc, l_sc, acc_sc, p_sc, *, bq, hb):
    s_id = pl.program_id(1)
    flag = fl_s[s_id]
    safe = safe_s[0] != 0
    last = (flag & 2) != 0

    @pl.when((flag & 1) != 0)
    def _init():
        l_sc[...] = jnp.zeros(l_sc.shape, F32)
        acc_sc[...] = jnp.zeros(acc_sc.shape, F32)
        p_sc[...] = jnp.zeros(p_sc.shape, BF16)
        for hh in range(Q_PER_KV):
            qs_sc[:, hh * bq:(hh + 1) * bq] = q_ref[hh * HEAD_DIM:(hh + 1) * HEAD_DIM, :]
        qf = qs_sc[...].astype(F32)
        qn2 = jnp.sum(qf * qf, axis=0, keepdims=True)
        bound = jnp.sqrt(qn2 * km_ref[0:1, 0:1]) * BOUND_SLACK
        m_sc[...] = jnp.where(safe, bound, -jnp.inf)

    @pl.when(safe)
    def _fast():
        q = qs_sc[...]
        m = m_sc[...]
        sa = jnp.dot(k_ref[:hb, :], q, preferred_element_type=F32)
        acc = acc_sc[...] + jnp.dot(vp_ref[...], p_sc[...], preferred_element_type=F32)
        pa = jnp.exp2(sa - m)
        la = jnp.sum(pa, axis=0, keepdims=True)
        sb = jnp.dot(k_ref[hb:, :], q, preferred_element_type=F32)
        acc_sc[...] = acc + jnp.dot(v_ref[:, :hb], pa.astype(BF16), preferred_element_type=F32)
        pb = jnp.exp2(sb - m)
        l_sc[...] += la + jnp.sum(pb, axis=0, keepdims=True)
        p_sc[...] = pb.astype(BF16)

    @pl.when(jnp.logical_and(safe, last))
    def _flush():
        acc_sc[...] += jnp.dot(v_ref[:, hb:], p_sc[...], preferred_element_type=F32)

    @pl.when(jnp.logical_not(safe))
    def _online():
        s = jnp.dot(k_ref[...], qs_sc[...], preferred_element_type=F32)
        m_prev = m_sc[...]
        m_new = jnp.maximum(m_prev, jnp.max(s, axis=0, keepdims=True))
        alpha = jnp.exp2(m_prev - m_new)
        p = jnp.exp2(s - m_new)
        l_sc[...] = alpha * l_sc[...] + jnp.sum(p, axis=0, keepdims=True)
        acc_sc[...] = alpha * acc_sc[...] + jnp.dot(v_ref[...], p.astype(BF16), preferred_element_type=F32)
        m_sc[...] = m_new

    @pl.when((flag & 2) != 0)
    def _fin():
        o = acc_sc[...] / l_sc[...]
        for hh in range(Q_PER_KV):
            o_ref[:, hh * HEAD_DIM:(hh + 1) * HEAD_DIM] = o[:, hh * bq:(hh + 1) * bq].T.astype(BF16)


def _flash_schedule(grp, bq, bkv):
    qt, kt, vp, fl = [], [], [], []
    off = 0
    for b, s in grp.groups:
        for bi in range(b):
            base = off + bi * s
            nk = s // bkv
            for qi in range(s // bq):
                for ki in range(nk):
                    blk = base // bkv + ki
                    qt.append(base // bq + qi)
                    kt.append(blk)
                    vp.append(2 * (blk - 1 if ki else blk) + 1)
                    fl.append((1 if ki == 0 else 0) | (2 if ki == nk - 1 else 0))
        off += b * s
    return tuple(np.asarray(a, np.int32) for a in (qt, kt, vp, fl))


def _flash(qt, k, vt, kmax2, safe, grp, bq, bkv):
    n = k.shape[0]
    hb = bkv // 2
    qt_a, kt_a, vp_a, fl_a = _flash_schedule(grp, bq, bkv)
    n_steps = len(qt_a)
    w = Q_PER_KV * HEAD_DIM
    gs = pltpu.PrefetchScalarGridSpec(
        num_scalar_prefetch=5,
        grid=(N_KV_HEADS, n_steps),
        in_specs=[pl.BlockSpec((w, bq), lambda h, s, qa, ka, va, fa, sa: (h, qa[s])),
                  pl.BlockSpec((bkv, HEAD_DIM), lambda h, s, qa, ka, va, fa, sa: (ka[s], h)),
                  pl.BlockSpec((HEAD_DIM, bkv), lambda h, s, qa, ka, va, fa, sa: (h, ka[s])),
                  pl.BlockSpec((HEAD_DIM, hb), lambda h, s, qa, ka, va, fa, sa: (h, va[s])),
                  pl.BlockSpec((8, 128), lambda h, s, qa, ka, va, fa, sa: (0, 0))],
        out_specs=pl.BlockSpec((bq, w), lambda h, s, qa, ka, va, fa, sa: (qa[s], h)),
        scratch_shapes=[pltpu.VMEM((HEAD_DIM, Q_PER_KV * bq), BF16),
                        pltpu.VMEM((1, Q_PER_KV * bq), F32),
                        pltpu.VMEM((1, Q_PER_KV * bq), F32),
                        pltpu.VMEM((HEAD_DIM, Q_PER_KV * bq), F32),
                        pltpu.VMEM((hb, Q_PER_KV * bq), BF16)])
    return pl.pallas_call(
        functools.partial(_flash_kernel, bq=bq, hb=hb),
        grid_spec=gs,
        out_shape=jax.ShapeDtypeStruct((n, ATTN_WIDTH), BF16),
        compiler_params=_cparams(2),
        name="flash",
    )(jnp.asarray(qt_a), jnp.asarray(kt_a), jnp.asarray(vp_a), jnp.asarray(fl_a), safe, qt, k, vt, vt, kmax2)


def _sgu_kernel(u_ref, gv_ref, lng_ref, lnb_ref, ws_ref, bst_ref, og_ref, o_ref, *, n_chunks):
    v = gv_ref[...].astype(F32)
    mean = jnp.mean(v, axis=-1, keepdims=True)
    vc = v - mean
    var = jnp.mean(vc * vc, axis=-1, keepdims=True)
    vn = (vc * lax.rsqrt(var + EPS) * lng_ref[...] + lnb_ref[...]).astype(BF16)
    u = u_ref[...].astype(F32)
    heads = []
    for hh in range(N_A_HEADS):
        cols = slice(hh * A_HEAD_DIM, (hh + 1) * A_HEAD_DIM)
        rhs = jnp.concatenate([vn[c * CHUNK:(c + 1) * CHUNK, cols] for c in range(n_chunks)], axis=1)
        mixed = jnp.dot(ws_ref[hh], rhs, preferred_element_type=F32) + bst_ref[:, hh:hh + 1]
        heads.append(jnp.concatenate(
            [mixed[:, c * A_HEAD_DIM:(c + 1) * A_HEAD_DIM] for c in range(n_chunks)], axis=0))
    sg = u * jnp.concatenate(heads, axis=1)
    ms = jnp.mean(sg * sg, axis=-1, keepdims=True)
    o_ref[...] = (sg * lax.rsqrt(ms + EPS) * og_ref[...]).astype(BF16)


def _sgu(u, gv, lng, lnb, ws, bst, og, tm):
    n = u.shape[0]
    tok = pl.BlockSpec((tm, A_WIDTH), lambda i: (i, 0))
    return pl.pallas_call(
        functools.partial(_sgu_kernel, n_chunks=tm // CHUNK),
        grid=(n // tm,),
        in_specs=[tok, tok, _const_spec((1, A_WIDTH)), _const_spec((1, A_WIDTH)),
                  _const_spec(ws.shape), _const_spec(bst.shape), _const_spec((1, A_WIDTH))],
        out_specs=tok,
        out_shape=jax.ShapeDtypeStruct((n, A_WIDTH), BF16),
        compiler_params=_cparams(1),
        name="sgu",
    )(u, gv, lng, lnb, ws, bst, og)


def _outproj_kernel(attn_ref, sgu_ref, xa_ref, xb_ref, mod_ref, ag_ref, wo_ref, g2_ref, wrt_ref,
                    x1_ref, h2_ref, lgt_ref, *, n0):
    a = attn_ref[...].astype(F32)
    ms = jnp.mean(a * a, axis=-1, keepdims=True)
    an = (a * lax.rsqrt(ms + EPS) * ag_ref[...]).astype(BF16)
    o = jnp.dot(an, wo_ref[:ATTN_WIDTH, :], preferred_element_type=F32)
    o = o + jnp.dot(sgu_ref[...], wo_ref[ATTN_WIDTH:, :], preferred_element_type=F32)
    gate1 = mod_ref[2:3, :]
    shift2 = mod_ref[3:4, :]
    scale2 = mod_ref[4:5, :]
    x1 = _pick_group(pl.program_id(0), n0, xa_ref, xb_ref) + gate1 * o
    x1_ref[...] = x1
    ms2 = jnp.mean(x1 * x1, axis=-1, keepdims=True)
    h2 = (x1 * lax.rsqrt(ms2 + EPS) * g2_ref[...]) * (1.0 + scale2) + shift2
    _store_slabs(h2_ref, h2)
    lgt_ref[...] = lax.dot_general(wrt_ref[...], h2.astype(BF16), NT_DIMS, preferred_element_type=F32)


def _outproj(attn, sgun, xa, xb, modp, ag, wo, g2, wrt, grp, tm):
    n, d = grp.n_tokens, xa.shape[1]
    bidx, _ = grp.tile_maps(tm)
    tok = lambda w: pl.BlockSpec((tm, w), lambda i: (i, 0))
    xa_spec, xb_spec, n0 = grp.split_specs(tm, d)
    return pl.pallas_call(
        functools.partial(_outproj_kernel, n0=n0),
        grid=(n // tm,),
        in_specs=[tok(ATTN_WIDTH), tok(A_WIDTH), xa_spec, xb_spec,
                  pl.BlockSpec((None, 8, d), lambda i: (bidx(i), 0, 0)),
                  _const_spec((1, ATTN_WIDTH)), _const_spec(wo.shape), _const_spec((1, d)),
                  _const_spec(wrt.shape)],
        out_specs=[tok(d), pl.BlockSpec((tm * _slab_rows(d), LANES), lambda i: (i, 0)),
                   pl.BlockSpec((N_EXPERTS, tm), lambda i: (0, i))],
        out_shape=[jax.ShapeDtypeStruct((n, d), F32),
                   jax.ShapeDtypeStruct((n * _slab_rows(d), LANES), F32),
                   jax.ShapeDtypeStruct((N_EXPERTS, n), F32)],
        compiler_params=_cparams(1),
        name="outproj",
    )(attn, sgun, xa, xb, modp, ag, wo, g2, wrt)


def _first_index(hit, iota, sentinel):
    return jnp.min(jnp.where(hit, iota, sentinel), axis=0, keepdims=True)


def _router_kernel(lg_ref, bias_ref, idx_ref, w_ref, rank_ref, cnt_ref, run_sc):
    i = pl.program_id(0)
    tt = lg_ref.shape[1]

    @pl.when(i == 0)
    def _init():
        run_sc[...] = jnp.zeros(run_sc.shape, F32)

    scores = jax.nn.sigmoid(lg_ref[...])
    biased = scores + bias_ref[...]
    neg = -jnp.inf
    iota_g = lax.broadcasted_iota(I32, (GROUP_SIZE, tt), 0).astype(F32)

    gs_rows = []
    for g in range(N_GROUPS):
        xg = biased[g * GROUP_SIZE:(g + 1) * GROUP_SIZE, :]
        m1 = jnp.max(xg, axis=0, keepdims=True)
        f1 = _first_index(xg == m1, iota_g, float(GROUP_SIZE))
        m2 = jnp.max(jnp.where(iota_g == f1, neg, xg), axis=0, keepdims=True)
        gs_rows.append(m1 + m2)
    gs = jnp.concatenate(gs_rows, axis=0)

    iota_n = lax.broadcasted_iota(I32, (N_GROUPS, tt), 0).astype(F32)
    gsel = jnp.zeros((N_GROUPS, tt), F32)
    cur = gs
    for _ in range(TOPK_GROUPS):
        m = jnp.max(cur, axis=0, keepdims=True)
        f = _first_index(cur == m, iota_n, float(N_GROUPS))
        hit = iota_n == f
        gsel = jnp.where(hit, 1.0, gsel)
        cur = jnp.where(hit, neg, cur)

    masked = jnp.concatenate(
        [jnp.where(gsel[g:g + 1, :] > 0.5, biased[g * GROUP_SIZE:(g + 1) * GROUP_SIZE, :], neg)
         for g in range(N_GROUPS)], axis=0)

    iota_e = lax.broadcasted_iota(I32, (N_EXPERTS, tt), 0).astype(F32)
    sel = jnp.zeros((N_EXPERTS, tt), F32)
    idx_rows, w_rows = [], []
    for _ in range(TOP_K):
        m = jnp.max(masked, axis=0, keepdims=True)
        f = _first_index(masked == m, iota_e, float(N_EXPERTS))
        hit = iota_e == f
        idx_rows.append(f)
        w_rows.append(jnp.sum(jnp.where(hit, scores, 0.0), axis=0, keepdims=True))
        sel = jnp.where(hit, 1.0, sel)
        masked = jnp.where(hit, neg, masked)
    idx = jnp.concatenate(idx_rows, axis=0)
    wk = jnp.concatenate(w_rows, axis=0)
    wk = wk / (jnp.sum(wk, axis=0, keepdims=True) + 1e-20) * ROUTED_SCALE

    r = lax.broadcasted_iota(I32, (tt, tt), 0)
    c = lax.broadcasted_iota(I32, (tt, tt), 1)
    upper = jnp.where(r < c, 1.0, 0.0).astype(BF16)
    prefix = jnp.dot(sel.astype(BF16), upper, preferred_element_type=F32) + run_sc[...]
    rank_rows = [jnp.sum(jnp.where(iota_e == idx[k:k + 1, :], prefix, 0.0), axis=0, keepdims=True)
                 for k in range(TOP_K)]
    run_sc[...] = run_sc[...] + jnp.sum(sel, axis=1, keepdims=True)

    idx_ref[...] = idx.astype(I32)
    w_ref[...] = wk
    rank_ref[...] = jnp.concatenate(rank_rows, axis=0).astype(I32)
    cnt_ref[...] = jnp.broadcast_to(run_sc[...], cnt_ref.shape).astype(I32)


def _router(lgt, bias, tt):
    n = lgt.shape[1]
    tokk = pl.BlockSpec((TOP_K, tt), lambda i: (0, i))
    return pl.pallas_call(
        _router_kernel,
        grid=(n // tt,),
        in_specs=[pl.BlockSpec((N_EXPERTS, tt), lambda i: (0, i)), _const_spec((N_EXPERTS, 1))],
        out_specs=[tokk, tokk, tokk, _const_spec((N_EXPERTS, 128))],
        out_shape=[jax.ShapeDtypeStruct((TOP_K, n), I32), jax.ShapeDtypeStruct((TOP_K, n), F32),
                   jax.ShapeDtypeStruct((TOP_K, n), I32), jax.ShapeDtypeStruct((N_EXPERTS, 128), I32)],
        scratch_shapes=[pltpu.VMEM((N_EXPERTS, 1), F32)],
        compiler_params=_cparams(1),
        name="router",
    )(lgt, bias)


def _dest_kernel(idx_ref, rank_ref, start_ref, o_ref):
    tt = idx_ref.shape[1]
    iota_e = lax.broadcasted_iota(I32, (N_EXPERTS, tt), 0)
    start = start_ref[...]
    rows = []
    for k in range(TOP_K):
        hit = iota_e == idx_ref[k:k + 1, :]
        rows.append(jnp.sum(jnp.where(hit, start, 0), axis=0, keepdims=True))
    o_ref[...] = jnp.concatenate(rows, axis=0) + rank_ref[...]


def _dest(idx, rank, pad_start, tt):
    n = idx.shape[1]
    tokk = pl.BlockSpec((TOP_K, tt), lambda i: (0, i))
    return pl.pallas_call(
        _dest_kernel,
        grid=(n // tt,),
        in_specs=[tokk, tokk, _const_spec((N_EXPERTS, 1))],
        out_specs=tokk,
        out_shape=jax.ShapeDtypeStruct((TOP_K, n), I32),
        compiler_params=_cparams(1),
        name="dest",
    )(idx, rank, pad_start.reshape(N_EXPERTS, 1))


def _slab(ref, tok, rows):
    return ref.at[pl.ds(pl.multiple_of(tok * rows, SUBLANES), rows)]


def _dispatch_kernel(cnt_s, start_s, pend_s, dest_ref, h_ref, xs_ref, zero_sc, sem, *, td):
    i = pl.program_id(0)
    rows = zero_sc.shape[0]

    def slab_copy(t, slot):
        return pltpu.make_async_copy(_slab(h_ref, t, rows), _slab(xs_ref, slot, rows), sem)

    def issue(t, c):
        for k in range(TOP_K):
            slab_copy(t, dest_ref[k, t]).start(priority=k % 2)
        return c

    lax.fori_loop(0, td, issue, 0)

    def drain(t, c):
        for k in range(TOP_K):
            slab_copy(0, 0).wait()
        return c

    lax.fori_loop(0, td, drain, 0)

    @pl.when(i == pl.num_programs(0) - 1)
    def _pad():
        zero_sc[...] = jnp.zeros(zero_sc.shape, F32)

        def pad_copy(p):
            return pltpu.make_async_copy(zero_sc, _slab(xs_ref, p, rows), sem)

        def per_expert(e, c):
            lo = start_s[e] + cnt_s[e]
            hi = pend_s[e]

            def go(p, c2):
                pad_copy(p).start()
                return c2

            lax.fori_loop(lo, hi, go, 0)

            def done(p, c2):
                pad_copy(p).wait()
                return c2

            lax.fori_loop(lo, hi, done, 0)
            return c

        lax.fori_loop(0, N_EXPERTS, per_expert, 0)


def _dispatch(counts, pad_start, pad_end, dest3, h2s, n, p_tok, td):
    rows = h2s.shape[0] // n
    gs = pltpu.PrefetchScalarGridSpec(
        num_scalar_prefetch=3,
        grid=(n // td,),
        in_specs=[pl.BlockSpec((None, TOP_K, td), lambda i, *_: (i, 0, 0), memory_space=pltpu.SMEM),
                  pl.BlockSpec((td * rows, LANES), lambda i, *_: (i, 0))],
        out_specs=pl.BlockSpec(memory_space=pl.ANY),
        scratch_shapes=[pltpu.VMEM((rows, LANES), F32), pltpu.SemaphoreType.DMA(())])
    return pl.pallas_call(
        functools.partial(_dispatch_kernel, td=td),
        grid_spec=gs,
        out_shape=jax.ShapeDtypeStruct((p_tok * rows, LANES), F32),
        compiler_params=_cparams(1),
        name="dispatch",
    )(counts, pad_start, pad_end, dest3, h2s)


def _swiglu_slabs(x_ref, n_tok, wgu_ref, wd_ref):
    de = wd_ref.shape[0]
    r = jnp.dot(_load_slabs(x_ref, n_tok).astype(BF16), wgu_ref[...], preferred_element_type=F32)
    a = r[:, :de]
    act = (a * jax.nn.sigmoid(a) * r[:, de:]).astype(BF16)
    return jnp.dot(act, wd_ref[...], preferred_element_type=F32)


def _shared_kernel(h_ref, wgu_ref, wd_ref, o_ref):
    o_ref[...] = _swiglu_slabs(h_ref, o_ref.shape[0], wgu_ref, wd_ref)


def _shared(h2s, wgu, wd, tm):
    d = wd.shape[1]
    rows = _slab_rows(d)
    n = h2s.shape[0] // rows
    return pl.pallas_call(
        _shared_kernel,
        grid=(n // tm,),
        in_specs=[pl.BlockSpec((tm * rows, LANES), lambda i: (i, 0)),
                  _const_spec(wgu.shape), _const_spec(wd.shape)],
        out_specs=pl.BlockSpec((tm, d), lambda i: (i, 0)),
        out_shape=jax.ShapeDtypeStruct((n, d), F32),
        compiler_params=_cparams(1),
        name="shared",
    )(h2s, wgu, wd)


def _experts_kernel(be_s, nu_s, x_ref, wg_ref, wu_ref, wd_ref, o_ref, wgu_sc, wd_sc, *, bm):
    i = pl.program_id(0)
    used = i < nu_s[0]
    prev = be_s[jnp.maximum(i - 1, 0)]
    fresh = jnp.logical_or(i == 0, be_s[i] != prev)
    de = wd_sc.shape[0]

    @pl.when(jnp.logical_and(used, fresh))
    def _cast():
        wgu_sc[:, :de] = wg_ref[...].astype(BF16)
        wgu_sc[:, de:] = wu_ref[...].astype(BF16)
        wd_sc[...] = wd_ref[...].astype(BF16)

    @pl.when(used)
    def _run():
        o_ref[...] = _swiglu_slabs(x_ref, bm, wgu_sc, wd_sc)


def _experts(block_exp, n_used, xs, w_gate, w_up, w_down, bm):
    d, de = w_gate.shape[1], w_gate.shape[2]
    rows = _slab_rows(d)
    p_tok = xs.shape[0] // rows
    n_blocks = p_tok // bm
    row = lambda i, be, nu: (jnp.minimum(i, nu[0] - 1), 0)
    wsel = lambda i, be, nu: (be[jnp.minimum(i, nu[0] - 1)], 0, 0)
    gs = pltpu.PrefetchScalarGridSpec(
        num_scalar_prefetch=2,
        grid=(n_blocks,),
        in_specs=[pl.BlockSpec((bm * rows, LANES), row),
                  pl.BlockSpec((None, d, de), wsel),
                  pl.BlockSpec((None, d, de), wsel),
                  pl.BlockSpec((None, de, d), wsel)],
        out_specs=pl.BlockSpec((bm, d), row),
        scratch_shapes=[pltpu.VMEM((d, 2 * de), BF16), pltpu.VMEM((de, d), BF16)])
    return pl.pallas_call(
        functools.partial(_experts_kernel, bm=bm),
        grid_spec=gs,
        out_shape=jax.ShapeDtypeStruct((p_tok, d), F32),
        compiler_params=_cparams(1),
        name="experts",
    )(block_exp, n_used, xs, w_gate, w_up, w_down)


def _combine_kernel(dest_ref, w_ref, x1_ref, sh_ref, mod_ref, ys_ref, oa_ref, ob_ref, buf, sem, *, tc, n0):
    def row_copy(k, t, slot):
        return pltpu.make_async_copy(ys_ref.at[pl.ds(slot, 1)], buf.at[k, pl.ds(t, 1)], sem)

    def issue(t, c):
        for k in range(TOP_K):
            row_copy(k, t, dest_ref[k, t]).start(priority=k % 2)
        return c

    lax.fori_loop(0, tc, issue, 0)

    def drain(t, c):
        for k in range(TOP_K):
            row_copy(k, 0, 0).wait()
        return c

    lax.fori_loop(0, tc, drain, 0)

    w = w_ref[...]
    routed = w[:, 0:1] * buf[0]
    for k in range(1, TOP_K):
        routed = routed + w[:, k:k + 1] * buf[k]
    y = x1_ref[...] + mod_ref[5:6, :] * (sh_ref[...] + routed)

    @pl.when(pl.program_id(0) < n0)
    def _():
        oa_ref[...] = y

    @pl.when(pl.program_id(0) >= n0)
    def _():
        ob_ref[...] = y


def _combine(dest3, wt, x1, shared, modp, ys, grp, tc):
    n, d = x1.shape
    bidx, _ = grp.tile_maps(tc)
    tok = pl.BlockSpec((tc, d), lambda i: (i, 0))
    oa_spec, ob_spec, n0 = grp.split_specs(tc, d)
    (b0, s0), (b1, s1) = grp.groups
    return pl.pallas_call(
        functools.partial(_combine_kernel, tc=tc, n0=n0),
        grid=(n // tc,),
        in_specs=[pl.BlockSpec((None, TOP_K, tc), lambda i: (i, 0, 0), memory_space=pltpu.SMEM),
                  pl.BlockSpec((tc, TOP_K), lambda i: (i, 0)),
                  tok, tok,
                  pl.BlockSpec((None, 8, d), lambda i: (bidx(i), 0, 0)),
                  pl.BlockSpec(memory_space=pl.ANY)],
        out_specs=[oa_spec, ob_spec],
        out_shape=[jax.ShapeDtypeStruct((b0 * s0, d), F32), jax.ShapeDtypeStruct((b1 * s1, d), F32)],
        scratch_shapes=[pltpu.VMEM((TOP_K, tc, d), F32), pltpu.SemaphoreType.DMA(())],
        compiler_params=_cparams(1),
        name="combine",
    )(dest3, wt, x1, shared, modp, ys)


def _tok3(a, t):
    k, n = a.shape
    return a.reshape(k, n // t, t).transpose(1, 0, 2)


def _layer(xa, xb, c, grp, norm1_g, w_mod, b_mod, w_in, q_norm_g, k_norm_g, sgu_ln_g, sgu_ln_b, w_s, b_s,
           attn_out_g, sgu_out_g, w_out, norm2_g, w_router, router_bias, w_gate, w_up, w_down,
           ws_gate, ws_up, ws_down):
    n, d = grp.n_tokens, xa.shape[1]
    nb = c.shape[0]
    s_min = min(s for _, s in grp.groups)
    s_max = max(s for _, s in grp.groups)

    c8 = jnp.pad(c, ((0, 8 - nb), (0, 0)))
    mod = _mod(c8, w_mod, b_mod)[:nb].reshape(nb, N_MOD, d)
    modp = jnp.pad(mod, ((0, 0), (0, 8 - N_MOD), (0, 0)))

    perm = np.concatenate([np.arange(0, HEAD_DIM, 2), np.arange(1, HEAD_DIM, 2)])
    c0, c1, c2, c3 = ATTN_WIDTH, ATTN_WIDTH + KV_WIDTH, ATTN_WIDTH + 2 * KV_WIDTH, ATTN_WIDTH + 2 * KV_WIDTH + A_WIDTH
    wq = w_in[:, :c0].reshape(d, N_Q_HEADS, HEAD_DIM)[:, :, perm].reshape(d, ATTN_WIDTH)
    wk = w_in[:, c0:c1].reshape(d, N_KV_HEADS, HEAD_DIM)[:, :, perm].reshape(d, KV_WIDTH)
    wqt = wq.T.astype(BF16)
    wkb = wk.astype(BF16)
    wvt = w_in[:, c1:c2].T.astype(BF16)
    wu = w_in[:, c2:c3].astype(BF16)
    wgv = w_in[:, c3:].astype(BF16)
    qg = q_norm_g[perm].reshape(HEAD_DIM, 1)
    kg = k_norm_g[perm].reshape(1, HEAD_DIM)

    cos2, sin2 = _rope_tables(s_max)
    scale = HEAD_DIM ** -0.5 * LOG2E
    tabs = (jnp.asarray((cos2 * scale).T, F32), jnp.asarray((sin2 * scale).T, F32),
            jnp.asarray(cos2, F32), jnp.asarray(sin2, F32))

    tm = _tile(s_min, 512)
    qt, k, vt, u, gv = _inproj(xa, xb, modp, norm1_g.reshape(1, d), wqt, wkb, wvt, wu, wgv, qg, kg, tabs,
                               grp, tm)

    k_bound2 = HEAD_DIM * jnp.max(jnp.square(k_norm_g)) * BOUND_SLACK
    q_bound = math.sqrt(HEAD_DIM) * scale * BOUND_SLACK * jnp.max(jnp.abs(q_norm_g))
    safe = (q_bound * jnp.sqrt(k_bound2) * BOUND_SLACK < SAFE_SCORE_BOUND).astype(I32).reshape(1)
    attn = _flash(qt, k, vt, jnp.full((8, 128), k_bound2, F32), safe, grp,
                  _tile(s_min, 256), _tile(s_min, 1024))
    sgun = _sgu(u, gv, sgu_ln_g.reshape(1, A_WIDTH), sgu_ln_b.reshape(1, A_WIDTH), w_s.astype(BF16),
                b_s.T, sgu_out_g.reshape(1, A_WIDTH), _tile(s_min, 512))

    x1, h2, lgt = _outproj(attn, sgun, xa, xb, modp, attn_out_g.reshape(1, ATTN_WIDTH), w_out.astype(BF16),
                           norm2_g.reshape(1, d), w_router.T.astype(BF16), grp, _tile(s_min, 256))

    idx, wts, rank, cnt = _router(lgt, router_bias.reshape(N_EXPERTS, 1), _tile(s_min, 512))

    bm = 256
    counts = cnt[:, 0]
    padded = (counts + bm - 1) // bm * bm
    pad_end = jnp.cumsum(padded).astype(I32)
    pad_start = pad_end - padded
    n_blocks = n * TOP_K // bm + N_EXPERTS
    p_tok = n_blocks * bm
    n_used = (pad_end[-1] // bm).reshape(1).astype(I32)
    block_start = jnp.arange(n_blocks, dtype=I32) * bm
    block_exp = jnp.minimum(jnp.sum((pad_end[None, :] <= block_start[:, None]).astype(I32), axis=1),
                            N_EXPERTS - 1)

    dest = _dest(idx, rank, pad_start, _tile(s_min, 2048))

    td = _tile(s_min, 256)
    xs = _dispatch(counts, pad_start, pad_end, _tok3(dest, td), h2, n, p_tok, td)
    shared = _shared(h2, jnp.concatenate([ws_gate, ws_up], axis=1).astype(BF16), ws_down.astype(BF16),
                     _tile(s_min, 512))
    ys = _experts(block_exp, n_used, xs, w_gate, w_up, w_down, bm)

    tc = _tile(s_min, 128)
    return _combine(_tok3(dest, tc), wts.T, x1, shared, modp, ys, grp, tc)


def kernel(x_prompt, x_sample, c_prompt, c_sample, norm1_g, w_mod, b_mod, w_in, q_norm_g, k_norm_g, sgu_ln_g, sgu_ln_b, w_s, b_s, attn_out_g, sgu_out_g, w_out, norm2_g, w_router, router_bias, w_gate, w_up, w_down, ws_gate, ws_up, ws_down):
    depth = norm1_g.shape[0]
    bp, sp, d = x_prompt.shape
    bs, ss, _ = x_sample.shape
    grp = _Groups([(bp, sp), (bs, ss)])
    xa, xb = x_prompt.reshape(bp * sp, d), x_sample.reshape(bs * ss, d)
    c = jnp.concatenate([c_prompt, c_sample], axis=0)
    for l in range(depth):
        xa, xb = _layer(xa, xb, c, grp, norm1_g[l], w_mod[l], b_mod[l], w_in[l], q_norm_g[l], k_norm_g[l],
                   sgu_ln_g[l], sgu_ln_b[l], w_s[l], b_s[l], attn_out_g[l], sgu_out_g[l], w_out[l],
                   norm2_g[l], w_router[l], router_bias[l], w_gate[l], w_up[l], w_down[l],
                   ws_gate[l], ws_up[l], ws_down[l])
    return (xa.reshape(bp, sp, d), xb.reshape(bs, ss, d))
```

```python
import functools
import math

import numpy as np
import jax
import jax.numpy as jnp
from jax import lax
from jax.experimental import pallas as pl
from jax.experimental.pallas import tpu as pltpu

F32 = jnp.float32
BF16 = jnp.bfloat16
I32 = jnp.int32

HEAD_DIM = 128
N_Q_HEADS = 8
N_KV_HEADS = 2
Q_PER_KV = N_Q_HEADS // N_KV_HEADS
ATTN_WIDTH = N_Q_HEADS * HEAD_DIM
KV_WIDTH = N_KV_HEADS * HEAD_DIM
ROPE_THETA = 10000.0
GRID_W = 64
N_A_HEADS = 8
A_HEAD_DIM = 128
A_WIDTH = N_A_HEADS * A_HEAD_DIM
CHUNK = 128
N_EXPERTS = 64
TOP_K = 8
N_GROUPS = 8
GROUP_SIZE = N_EXPERTS // N_GROUPS
TOPK_GROUPS = 4
ROUTED_SCALE = 2.5
N_MOD = 6
EPS = 1e-6

VMEM_LIMIT_BYTES = 56 * 1024 * 1024
NT_DIMS = (((1,), (1,)), ((), ()))


def _cparams(n_axes):
    return pltpu.CompilerParams(
        dimension_semantics=("arbitrary",) * n_axes, vmem_limit_bytes=VMEM_LIMIT_BYTES)


def _tile(n, pref):
    t = min(n, pref)
    while n % t:
        t //= 2
    return t


def _const_spec(shape):
    nd = len(shape)
    return pl.BlockSpec(shape, lambda *a: (0,) * nd)


def _mod_kernel(c_ref, w_ref, b_ref, o_ref):
    c = c_ref[...]
    a = (c * jax.nn.sigmoid(c)).astype(BF16)
    o_ref[...] = jnp.dot(a, w_ref[...].astype(BF16), preferred_element_type=F32) + b_ref[...]


def _mod(c8, w_mod, b_mod):
    d, n = w_mod.shape
    tn = _tile(n, 1024)
    return pl.pallas_call(
        _mod_kernel,
        grid=(n // tn,),
        in_specs=[_const_spec((8, d)),
                  pl.BlockSpec((d, tn), lambda j: (0, j)),
                  pl.BlockSpec((1, tn), lambda j: (0, j))],
        out_specs=pl.BlockSpec((8, tn), lambda j: (0, j)),
        out_shape=jax.ShapeDtypeStruct((8, n), F32),
        compiler_params=_cparams(1),
        name="mod",
    )(c8, w_mod, b_mod.reshape(1, n))


class _Groups:
    def __init__(self, groups):
        self.groups = groups
        self.n_tokens = sum(b * s for b, s in groups)

    def tile_maps(self, t):
        (b0, s0), (b1, s1) = self.groups
        n0 = b0 * s0 // t
        t0, t1 = s0 // t, s1 // t

        def bidx(i):
            return jnp.where(i < n0, i // t0, b0 + (i - n0) // t1)

        def pidx(i):
            return jnp.where(i < n0, i % t0, (i - n0) % t1)

        return bidx, pidx

    def split_specs(self, t, d):
        (b0, s0), _ = self.groups
        n0 = b0 * s0 // t
        return (pl.BlockSpec((t, d), lambda i, *_: (jnp.minimum(i, n0 - 1), 0)),
                pl.BlockSpec((t, d), lambda i, *_: (jnp.maximum(i - n0, 0), 0)), n0)


def _pick_group(i, n0, a_ref, b_ref):
    return jnp.where(i < n0, a_ref[...], b_ref[...])


LANES = 128
SUBLANES = 8


def _slab_rows(d):
    rows = d // LANES
    assert d % LANES == 0 and rows % SUBLANES == 0, d
    return rows


def _load_slabs(ref, rows, tok0, n_tok):
    return jnp.concatenate(
        [ref[pl.ds(tok0 * rows + r, n_tok, stride=rows), :] for r in range(rows)], axis=1)


def _store_slabs(ref, x):
    n_tok, d = x.shape
    rows = _slab_rows(d)
    for r in range(rows):
        ref[pl.ds(r, n_tok, stride=rows), :] = x[:, r * LANES:(r + 1) * LANES]


def _rope_tables(seq_len):
    pos = np.arange(seq_len)
    row = (pos // GRID_W).astype(np.float64)
    col = (pos % GRID_W).astype(np.float64)
    n_pairs = HEAD_DIM // 4
    freqs = ROPE_THETA ** (-np.arange(n_pairs, dtype=np.float64) / n_pairs)
    ang = np.concatenate([row[:, None] * freqs, col[:, None] * freqs], axis=-1)
    c, s = np.cos(ang), np.sin(ang)
    cos2 = np.concatenate([c, c], axis=-1)
    sin2 = np.concatenate([-s, s], axis=-1)
    return cos2, sin2


def _inproj_kernel(xa_ref, xb_ref, mod_ref, g1_ref, wqt_ref, wk_ref, wvt_ref, wu_ref, wgv_ref,
                   qg_ref, kg_ref, cost_ref, sint_ref, cos_ref, sin_ref,
                   qt_ref, k_ref, vt_ref, u_ref, gv_ref, *, n0):
    x = _pick_group(pl.program_id(0), n0, xa_ref, xb_ref)
    shift1 = mod_ref[0:1, :]
    scale1 = mod_ref[1:2, :]
    ms = jnp.mean(x * x, axis=-1, keepdims=True)
    h = (x * lax.rsqrt(ms + EPS) * g1_ref[...]) * (1.0 + scale1) + shift1
    hb = h.astype(BF16)
    half = HEAD_DIM // 2

    qt = lax.dot_general(wqt_ref[...], hb, NT_DIMS, preferred_element_type=F32)
    cost = cost_ref[...]
    sint = sint_ref[...]
    qg = qg_ref[...]
    for hh in range(N_Q_HEADS):
        qh = qt[hh * HEAD_DIM:(hh + 1) * HEAD_DIM, :]
        msq = jnp.mean(qh * qh, axis=0, keepdims=True)
        qn = qh * lax.rsqrt(msq + EPS) * qg
        rot = jnp.concatenate([qn[half:, :], qn[:half, :]], axis=0)
        qt_ref[hh * HEAD_DIM:(hh + 1) * HEAD_DIM, :] = (qn * cost + rot * sint).astype(BF16)

    vt_ref[...] = lax.dot_general(wvt_ref[...], hb, NT_DIMS, preferred_element_type=F32).astype(BF16)

    kk = jnp.dot(hb, wk_ref[...], preferred_element_type=F32)
    cos = cos_ref[...]
    sin = sin_ref[...]
    kg = kg_ref[...]
    for j in range(N_KV_HEADS):
        kh = kk[:, j * HEAD_DIM:(j + 1) * HEAD_DIM]
        msk = jnp.mean(kh * kh, axis=-1, keepdims=True)
        kn = kh * lax.rsqrt(msk + EPS) * kg
        rot = jnp.concatenate([kn[:, half:], kn[:, :half]], axis=1)
        k_ref[:, j * HEAD_DIM:(j + 1) * HEAD_DIM] = (kn * cos + rot * sin).astype(BF16)

    u_ref[...] = jnp.dot(hb, wu_ref[...], preferred_element_type=F32).astype(BF16)
    gv_ref[...] = jnp.dot(hb, wgv_ref[...], preferred_element_type=F32).astype(BF16)


def _inproj(xa, xb, modp, g1, wqt, wk, wvt, wu, wgv, qg, kg, tabs, grp, tm):
    n, d = grp.n_tokens, xa.shape[1]
    bidx, pidx = grp.tile_maps(tm)
    cost, sint, cos, sin = tabs
    tok = lambda w: pl.BlockSpec((tm, w), lambda i: (i, 0))
    tokt = lambda w: pl.BlockSpec((w, tm), lambda i: (0, i))
    xa_spec, xb_spec, n0 = grp.split_specs(tm, d)
    return pl.pallas_call(
        functools.partial(_inproj_kernel, n0=n0),
        grid=(n // tm,),
        in_specs=[xa_spec, xb_spec,
                  pl.BlockSpec((None, 8, d), lambda i: (bidx(i), 0, 0)),
                  _const_spec((1, d)),
                  _const_spec(wqt.shape), _const_spec(wk.shape), _const_spec(wvt.shape),
                  _const_spec(wu.shape), _const_spec(wgv.shape),
                  _const_spec((HEAD_DIM, 1)), _const_spec((1, HEAD_DIM)),
                  pl.BlockSpec((HEAD_DIM, tm), lambda i: (0, pidx(i))),
                  pl.BlockSpec((HEAD_DIM, tm), lambda i: (0, pidx(i))),
                  pl.BlockSpec((tm, HEAD_DIM), lambda i: (pidx(i), 0)),
                  pl.BlockSpec((tm, HEAD_DIM), lambda i: (pidx(i), 0))],
        out_specs=[tokt(ATTN_WIDTH), tok(KV_WIDTH), tokt(KV_WIDTH), tok(A_WIDTH), tok(A_WIDTH)],
        out_shape=[jax.ShapeDtypeStruct((ATTN_WIDTH, n), BF16),
                   jax.ShapeDtypeStruct((n, KV_WIDTH), BF16),
                   jax.ShapeDtypeStruct((KV_WIDTH, n), BF16),
                   jax.ShapeDtypeStruct((n, A_WIDTH), BF16),
                   jax.ShapeDtypeStruct((n, A_WIDTH), BF16)],
        compiler_params=_cparams(1),
        name="inproj",
    )(xa, xb, modp, g1, wqt, wk, wvt, wu, wgv, qg, kg, cost, sint, cos, sin)


LOG2E = 1.4426950408889634
SAFE_SCORE_BOUND = 60.0
BOUND_SLACK = 1.01


def _flash_kernel(qt_s, kt_s, vp_s, fl_s, safe_s, q_ref, k_ref, v_ref, vp_ref, km_ref, o_ref,
                  qs_sc, m_sc, l_sc, acc_sc, p_sc, *, bq, hb):
    s_id = pl.program_id(1)
    flag = fl_s[s_id]
    safe = safe_s[0] != 0
    last = (flag & 2) != 0

    @pl.when((flag & 1) != 0)
    def _init():
        l_sc[...] = jnp.zeros(l_sc.shape, F32)
        acc_sc[...] = jnp.zeros(acc_sc.shape, F32)
        p_sc[...] = jnp.zeros(p_sc.shape, BF16)
        for hh in range(Q_PER_KV):
            qs_sc[:, hh * bq:(hh + 1) * bq] = q_ref[hh * HEAD_DIM:(hh + 1) * HEAD_DIM, :]
        qf = qs_sc[...].astype(F32)
        qn2 = jnp.sum(qf * qf, axis=0, keepdims=True)
        bound = jnp.sqrt(qn2 * km_ref[0:1, 0:1]) * BOUND_SLACK
        m_sc[...] = jnp.where(safe, bound, -jnp.inf)

    @pl.when(safe)
    def _fast():
        q = qs_sc[...]
        m = m_sc[...]
        sa = jnp.dot(k_ref[:hb, :], q, preferred_element_type=F32)
        acc = acc_sc[...] + jnp.dot(vp_ref[...], p_sc[...], preferred_element_type=F32)
        pa = jnp.exp2(sa - m)
        la = jnp.sum(pa, axis=0, keepdims=True)
        sb = jnp.dot(k_ref[hb:, :], q, preferred_element_type=F32)
        acc_sc[...] = acc + jnp.dot(v_ref[:, :hb], pa.astype(BF16), preferred_element_type=F32)
        pb = jnp.exp2(sb - m)
        l_sc[...] += la + jnp.sum(pb, axis=0, keepdims=True)
        p_sc[...] = pb.astype(BF16)

    @pl.when(jnp.logical_and(safe, last))
    def _flush():
        acc_sc[...] += jnp.dot(v_ref[:, hb:], p_sc[...], preferred_element_type=F32)

    @pl.when(jnp.logical_not(safe))
    def _online():
        s = jnp.dot(k_ref[...], qs_sc[...], preferred_element_type=F32)
        m_prev = m_sc[...]
        m_new = jnp.maximum(m_prev, jnp.max(s, axis=0, keepdims=True))
        alpha = jnp.exp2(m_prev - m_new)
        p = jnp.exp2(s - m_new)
        l_sc[...] = alpha * l_sc[...] + jnp.sum(p, axis=0, keepdims=True)
        acc_sc[...] = alpha * acc_sc[...] + jnp.dot(v_ref[...], p.astype(BF16), preferred_element_type=F32)
        m_sc[...] = m_new

    @pl.when((flag & 2) != 0)
    def _fin():
        o = acc_sc[...] / l_sc[...]
        for hh in range(Q_PER_KV):
            o_ref[:, hh * HEAD_DIM:(hh + 1) * HEAD_DIM] = o[:, hh * bq:(hh + 1) * bq].T.astype(BF16)


def _flash_schedule(grp, bq, bkv):
    qt, kt, vp, fl = [], [], [], []
    off = 0
    for b, s in grp.groups:
        for bi in range(b):
            base = off + bi * s
            nk = s // bkv
            for qi in range(s // bq):
                for ki in range(nk):
                    blk = base // bkv + ki
                    qt.append(base // bq + qi)
                    kt.append(blk)
                    vp.append(2 * (blk - 1 if ki else blk) + 1)
                    fl.append((1 if ki == 0 else 0) | (2 if ki == nk - 1 else 0))
        off += b * s
    return tuple(np.asarray(a, np.int32) for a in (qt, kt, vp, fl))


def _flash(qt, k, vt, kmax2, safe, grp, bq, bkv):
    n = k.shape[0]
    hb = bkv // 2
    qt_a, kt_a, vp_a, fl_a = _flash_schedule(grp, bq, bkv)
    n_steps = len(qt_a)
    w = Q_PER_KV * HEAD_DIM
    gs = pltpu.PrefetchScalarGridSpec(
        num_scalar_prefetch=5,
        grid=(N_KV_HEADS, n_steps),
        in_specs=[pl.BlockSpec((w, bq), lambda h, s, qa, ka, va, fa, sa: (h, qa[s])),
                  pl.BlockSpec((bkv, HEAD_DIM), lambda h, s, qa, ka, va, fa, sa: (ka[s], h)),
                  pl.BlockSpec((HEAD_DIM, bkv), lambda h, s, qa, ka, va, fa, sa: (h, ka[s])),
                  pl.BlockSpec((HEAD_DIM, hb), lambda h, s, qa, ka, va, fa, sa: (h, va[s])),
                  pl.BlockSpec((8, 128), lambda h, s, qa, ka, va, fa, sa: (0, 0))],
        out_specs=pl.BlockSpec((bq, w), lambda h, s, qa, ka, va, fa, sa: (qa[s], h)),
        scratch_shapes=[pltpu.VMEM((HEAD_DIM, Q_PER_KV * bq), BF16),
                        pltpu.VMEM((1, Q_PER_KV * bq), F32),
                        pltpu.VMEM((1, Q_PER_KV * bq), F32),
                        pltpu.VMEM((HEAD_DIM, Q_PER_KV * bq), F32),
                        pltpu.VMEM((hb, Q_PER_KV * bq), BF16)])
    return pl.pallas_call(
        functools.partial(_flash_kernel, bq=bq, hb=hb),
        grid_spec=gs,
        out_shape=jax.ShapeDtypeStruct((n, ATTN_WIDTH), BF16),
        compiler_params=_cparams(2),
        name="flash",
    )(jnp.asarray(qt_a), jnp.asarray(kt_a), jnp.asarray(vp_a), jnp.asarray(fl_a), safe, qt, k, vt, vt, kmax2)


def _sgu_kernel(u_ref, gv_ref, lng_ref, lnb_ref, ws_ref, bst_ref, og_ref, o_ref, *, n_chunks):
    v = gv_ref[...].astype(F32)
    mean = jnp.mean(v, axis=-1, keepdims=True)
    vc = v - mean
    var = jnp.mean(vc * vc, axis=-1, keepdims=True)
    vn = (vc * lax.rsqrt(var + EPS) * lng_ref[...] + lnb_ref[...]).astype(BF16)
    u = u_ref[...].astype(F32)
    heads = []
    for hh in range(N_A_HEADS):
        cols = slice(hh * A_HEAD_DIM, (hh + 1) * A_HEAD_DIM)
        rhs = jnp.concatenate([vn[c * CHUNK:(c + 1) * CHUNK, cols] for c in range(n_chunks)], axis=1)
        mixed = jnp.dot(ws_ref[hh], rhs, preferred_element_type=F32) + bst_ref[:, hh:hh + 1]
        heads.append(jnp.concatenate(
            [mixed[:, c * A_HEAD_DIM:(c + 1) * A_HEAD_DIM] for c in range(n_chunks)], axis=0))
    sg = u * jnp.concatenate(heads, axis=1)
    ms = jnp.mean(sg * sg, axis=-1, keepdims=True)
    o_ref[...] = (sg * lax.rsqrt(ms + EPS) * og_ref[...]).astype(BF16)


def _sgu(u, gv, lng, lnb, ws, bst, og, tm):
    n = u.shape[0]
    tok = pl.BlockSpec((tm, A_WIDTH), lambda i: (i, 0))
    return pl.pallas_call(
        functools.partial(_sgu_kernel, n_chunks=tm // CHUNK),
        grid=(n // tm,),
        in_specs=[tok, tok, _const_spec((1, A_WIDTH)), _const_spec((1, A_WIDTH)),
                  _const_spec(ws.shape), _const_spec(bst.shape), _const_spec((1, A_WIDTH))],
        out_specs=tok,
        out_shape=jax.ShapeDtypeStruct((n, A_WIDTH), BF16),
        compiler_params=_cparams(1),
        name="sgu",
    )(u, gv, lng, lnb, ws, bst, og)


def _outproj_kernel(attn_ref, sgu_ref, xa_ref, xb_ref, mod_ref, ag_ref, wo_ref, g2_ref, wrt_ref,
                    x1_ref, h2_ref, lgt_ref, *, n0):
    a = attn_ref[...].astype(F32)
    ms = jnp.mean(a * a, axis=-1, keepdims=True)
    an = (a * lax.rsqrt(ms + EPS) * ag_ref[...]).astype(BF16)
    o = jnp.dot(an, wo_ref[:ATTN_WIDTH, :], preferred_element_type=F32)
    o = o + jnp.dot(sgu_ref[...], wo_ref[ATTN_WIDTH:, :], preferred_element_type=F32)
    gate1 = mod_ref[2:3, :]
    shift2 = mod_ref[3:4, :]
    scale2 = mod_ref[4:5, :]
    x1 = _pick_group(pl.program_id(0), n0, xa_ref, xb_ref) + gate1 * o
    x1_ref[...] = x1
    ms2 = jnp.mean(x1 * x1, axis=-1, keepdims=True)
    h2 = (x1 * lax.rsqrt(ms2 + EPS) * g2_ref[...]) * (1.0 + scale2) + shift2
    _store_slabs(h2_ref, h2)
    lgt_ref[...] = lax.dot_general(wrt_ref[...], h2.astype(BF16), NT_DIMS, preferred_element_type=F32)


def _outproj(attn, sgun, xa, xb, modp, ag, wo, g2, wrt, grp, tm):
    n, d = grp.n_tokens, xa.shape[1]
    bidx, _ = grp.tile_maps(tm)
    tok = lambda w: pl.BlockSpec((tm, w), lambda i: (i, 0))
    xa_spec, xb_spec, n0 = grp.split_specs(tm, d)
    return pl.pallas_call(
        functools.partial(_outproj_kernel, n0=n0),
        grid=(n // tm,),
        in_specs=[tok(ATTN_WIDTH), tok(A_WIDTH), xa_spec, xb_spec,
                  pl.BlockSpec((None, 8, d), lambda i: (bidx(i), 0, 0)),
                  _const_spec((1, ATTN_WIDTH)), _const_spec(wo.shape), _const_spec((1, d)),
                  _const_spec(wrt.shape)],
        out_specs=[tok(d), pl.BlockSpec((tm * _slab_rows(d), LANES), lambda i: (i, 0)),
                   pl.BlockSpec((N_EXPERTS, tm), lambda i: (0, i))],
        out_shape=[jax.ShapeDtypeStruct((n, d), F32),
                   jax.ShapeDtypeStruct((n * _slab_rows(d), LANES), F32),
                   jax.ShapeDtypeStruct((N_EXPERTS, n), F32)],
        compiler_params=_cparams(1),
        name="outproj",
    )(attn, sgun, xa, xb, modp, ag, wo, g2, wrt)


def _first_index(hit, iota, sentinel):
    return jnp.min(jnp.where(hit, iota, sentinel), axis=0, keepdims=True)


def _router_kernel(lg_ref, bias_ref, idx_ref, w_ref, rank_ref, cnt_ref, run_sc):
    i = pl.program_id(0)
    tt = lg_ref.shape[1]

    @pl.when(i == 0)
    def _init():
        run_sc[...] = jnp.zeros(run_sc.shape, F32)

    scores = jax.nn.sigmoid(lg_ref[...])
    biased = scores + bias_ref[...]
    neg = -jnp.inf
    iota_g = lax.broadcasted_iota(I32, (GROUP_SIZE, tt), 0).astype(F32)

    gs_rows = []
    for g in range(N_GROUPS):
        xg = biased[g * GROUP_SIZE:(g + 1) * GROUP_SIZE, :]
        m1 = jnp.max(xg, axis=0, keepdims=True)
        f1 = _first_index(xg == m1, iota_g, float(GROUP_SIZE))
        m2 = jnp.max(jnp.where(iota_g == f1, neg, xg), axis=0, keepdims=True)
        gs_rows.append(m1 + m2)
    gs = jnp.concatenate(gs_rows, axis=0)

    iota_n = lax.broadcasted_iota(I32, (N_GROUPS, tt), 0).astype(F32)
    gsel = jnp.zeros((N_GROUPS, tt), F32)
    cur = gs
    for _ in range(TOPK_GROUPS):
        m = jnp.max(cur, axis=0, keepdims=True)
        f = _first_index(cur == m, iota_n, float(N_GROUPS))
        hit = iota_n == f
        gsel = jnp.where(hit, 1.0, gsel)
        cur = jnp.where(hit, neg, cur)

    masked = jnp.concatenate(
        [jnp.where(gsel[g:g + 1, :] > 0.5, biased[g * GROUP_SIZE:(g + 1) * GROUP_SIZE, :], neg)
         for g in range(N_GROUPS)], axis=0)

    iota_e = lax.broadcasted_iota(I32, (N_EXPERTS, tt), 0).astype(F32)
    sel = jnp.zeros((N_EXPERTS, tt), F32)
    idx_rows, w_rows = [], []
    for _ in range(TOP_K):
        m = jnp.max(masked, axis=0, keepdims=True)
        f = _first_index(masked == m, iota_e, float(N_EXPERTS))
        hit = iota_e == f
        idx_rows.append(f)
        w_rows.append(jnp.sum(jnp.where(hit, scores, 0.0), axis=0, keepdims=True))
        sel = jnp.where(hit, 1.0, sel)
        masked = jnp.where(hit, neg, masked)
    idx = jnp.concatenate(idx_rows, axis=0)
    wk = jnp.concatenate(w_rows, axis=0)
    wk = wk / (jnp.sum(wk, axis=0, keepdims=True) + 1e-20) * ROUTED_SCALE

    r = lax.broadcasted_iota(I32, (tt, tt), 0)
    c = lax.broadcasted_iota(I32, (tt, tt), 1)
    upper = jnp.where(r < c, 1.0, 0.0).astype(BF16)
    prefix = jnp.dot(sel.astype(BF16), upper, preferred_element_type=F32) + run_sc[...]
    rank_rows = [jnp.sum(jnp.where(iota_e == idx[k:k + 1, :], prefix, 0.0), axis=0, keepdims=True)
                 for k in range(TOP_K)]
    run_sc[...] = run_sc[...] + jnp.sum(sel, axis=1, keepdims=True)

    idx_ref[...] = idx.astype(I32)
    w_ref[...] = wk
    rank_ref[...] = jnp.concatenate(rank_rows, axis=0).astype(I32)
    cnt_ref[...] = jnp.broadcast_to(run_sc[...], cnt_ref.shape).astype(I32)


def _router(lgt, bias, tt):
    n = lgt.shape[1]
    tokk = pl.BlockSpec((TOP_K, tt), lambda i: (0, i))
    return pl.pallas_call(
        _router_kernel,
        grid=(n // tt,),
        in_specs=[pl.BlockSpec((N_EXPERTS, tt), lambda i: (0, i)), _const_spec((N_EXPERTS, 1))],
        out_specs=[tokk, tokk, tokk, _const_spec((N_EXPERTS, 128))],
        out_shape=[jax.ShapeDtypeStruct((TOP_K, n), I32), jax.ShapeDtypeStruct((TOP_K, n), F32),
                   jax.ShapeDtypeStruct((TOP_K, n), I32), jax.ShapeDtypeStruct((N_EXPERTS, 128), I32)],
        scratch_shapes=[pltpu.VMEM((N_EXPERTS, 1), F32)],
        compiler_params=_cparams(1),
        name="router",
    )(lgt, bias)


def _dest_kernel(idx_ref, rank_ref, start_ref, o_ref):
    tt = idx_ref.shape[1]
    iota_e = lax.broadcasted_iota(I32, (N_EXPERTS, tt), 0)
    start = start_ref[...]
    rows = []
    for k in range(TOP_K):
        hit = iota_e == idx_ref[k:k + 1, :]
        rows.append(jnp.sum(jnp.where(hit, start, 0), axis=0, keepdims=True))
    o_ref[...] = jnp.concatenate(rows, axis=0) + rank_ref[...]


def _dest(idx, rank, pad_start, tt):
    n = idx.shape[1]
    tokk = pl.BlockSpec((TOP_K, tt), lambda i: (0, i))
    return pl.pallas_call(
        _dest_kernel,
        grid=(n // tt,),
        in_specs=[tokk, tokk, _const_spec((N_EXPERTS, 1))],
        out_specs=tokk,
        out_shape=jax.ShapeDtypeStruct((TOP_K, n), I32),
        compiler_params=_cparams(1),
        name="dest",
    )(idx, rank, pad_start.reshape(N_EXPERTS, 1))


def _slab(ref, tok, rows):
    return ref.at[pl.ds(pl.multiple_of(tok * rows, SUBLANES), rows)]


def _dispatch_kernel(cnt_s, start_s, pend_s, dest_ref, h_ref, xs_ref, zero_sc, sem, *, td):
    i = pl.program_id(0)
    rows = zero_sc.shape[0]

    def slab_copy(t, slot):
        return pltpu.make_async_copy(_slab(h_ref, t, rows), _slab(xs_ref, slot, rows), sem)

    def issue(t, c):
        for k in range(TOP_K):
            slab_copy(t, dest_ref[k, t]).start(priority=k % 2)
        return c

    lax.fori_loop(0, td, issue, 0)

    def drain(t, c):
        for k in range(TOP_K):
            slab_copy(0, 0).wait()
        return c

    lax.fori_loop(0, td, drain, 0)

    @pl.when(i == pl.num_programs(0) - 1)
    def _pad():
        zero_sc[...] = jnp.zeros(zero_sc.shape, F32)

        def pad_copy(p):
            return pltpu.make_async_copy(zero_sc, _slab(xs_ref, p, rows), sem)

        def per_expert(e, c):
            lo = start_s[e] + cnt_s[e]
            hi = pend_s[e]

            def go(p, c2):
                pad_copy(p).start()
                return c2

            lax.fori_loop(lo, hi, go, 0)

            def done(p, c2):
                pad_copy(p).wait()
                return c2

            lax.fori_loop(lo, hi, done, 0)
            return c

        lax.fori_loop(0, N_EXPERTS, per_expert, 0)


def _dispatch(counts, pad_start, pad_end, dest3, h2s, n, p_tok, td):
    rows = h2s.shape[0] // n
    gs = pltpu.PrefetchScalarGridSpec(
        num_scalar_prefetch=3,
        grid=(n // td,),
        in_specs=[pl.BlockSpec((None, TOP_K, td), lambda i, *_: (i, 0, 0), memory_space=pltpu.SMEM),
                  pl.BlockSpec((td * rows, LANES), lambda i, *_: (i, 0))],
        out_specs=pl.BlockSpec(memory_space=pl.ANY),
        scratch_shapes=[pltpu.VMEM((rows, LANES), F32), pltpu.SemaphoreType.DMA(())])
    return pl.pallas_call(
        functools.partial(_dispatch_kernel, td=td),
        grid_spec=gs,
        out_shape=jax.ShapeDtypeStruct((p_tok * rows, LANES), F32),
        compiler_params=_cparams(1),
        name="dispatch",
    )(counts, pad_start, pad_end, dest3, h2s)


def _swiglu_slabs(x_ref, wgu_ref, wd_ref, o_ref):
    n_tok, d = o_ref.shape
    de = wd_ref.shape[0]
    rows = _slab_rows(d)
    half = n_tok // 2
    ups = [jnp.dot(_load_slabs(x_ref, rows, t0, half).astype(BF16), wgu_ref[...],
                   preferred_element_type=F32) for t0 in (0, half)]
    for t0, r in zip((0, half), ups):
        a = r[:, :de]
        act = (a * jax.nn.sigmoid(a) * r[:, de:]).astype(BF16)
        o_ref[t0:t0 + half, :] = jnp.dot(act, wd_ref[...], preferred_element_type=F32)


def _shared_kernel(h_ref, wgu_ref, wd_ref, o_ref):
    _swiglu_slabs(h_ref, wgu_ref, wd_ref, o_ref)


def _shared(h2s, wgu, wd, tm):
    d = wd.shape[1]
    rows = _slab_rows(d)
    n = h2s.shape[0] // rows
    return pl.pallas_call(
        _shared_kernel,
        grid=(n // tm,),
        in_specs=[pl.BlockSpec((tm * rows, LANES), lambda i: (i, 0)),
                  _const_spec(wgu.shape), _const_spec(wd.shape)],
        out_specs=pl.BlockSpec((tm, d), lambda i: (i, 0)),
        out_shape=jax.ShapeDtypeStruct((n, d), F32),
        compiler_params=_cparams(1),
        name="shared",
    )(h2s, wgu, wd)


def _experts_kernel(be_s, nu_s, x_ref, wg_ref, wu_ref, wd_ref, o_ref, wgu_sc, wd_sc):
    i = pl.program_id(0)
    used = i < nu_s[0]
    prev = be_s[jnp.maximum(i - 1, 0)]
    fresh = jnp.logical_or(i == 0, be_s[i] != prev)
    de = wd_sc.shape[0]

    @pl.when(jnp.logical_and(used, fresh))
    def _cast():
        wgu_sc[:, :de] = wg_ref[...].astype(BF16)
        wgu_sc[:, de:] = wu_ref[...].astype(BF16)
        wd_sc[...] = wd_ref[...].astype(BF16)

    @pl.when(used)
    def _run():
        _swiglu_slabs(x_ref, wgu_sc, wd_sc, o_ref)


def _experts(block_exp, n_used, xs, w_gate, w_up, w_down, bm):
    d, de = w_gate.shape[1], w_gate.shape[2]
    rows = _slab_rows(d)
    p_tok = xs.shape[0] // rows
    n_blocks = p_tok // bm
    row = lambda i, be, nu: (jnp.minimum(i, nu[0] - 1), 0)
    wsel = lambda i, be, nu: (be[jnp.minimum(i, nu[0] - 1)], 0, 0)
    gs = pltpu.PrefetchScalarGridSpec(
        num_scalar_prefetch=2,
        grid=(n_blocks,),
        in_specs=[pl.BlockSpec((bm * rows, LANES), row),
                  pl.BlockSpec((None, d, de), wsel),
                  pl.BlockSpec((None, d, de), wsel),
                  pl.BlockSpec((None, de, d), wsel)],
        out_specs=pl.BlockSpec((bm, d), row),
        scratch_shapes=[pltpu.VMEM((d, 2 * de), BF16), pltpu.VMEM((de, d), BF16)])
    return pl.pallas_call(
        _experts_kernel,
        grid_spec=gs,
        out_shape=jax.ShapeDtypeStruct((p_tok, d), F32),
        compiler_params=_cparams(1),
        name="experts",
    )(block_exp, n_used, xs, w_gate, w_up, w_down)


def _combine_kernel(dest_ref, w_ref, x1_ref, sh_ref, mod_ref, ys_ref, oa_ref, ob_ref, buf, sem, *, tc, n0):
    def row_copy(k, t, slot):
        return pltpu.make_async_copy(ys_ref.at[pl.ds(slot, 1)], buf.at[k, pl.ds(t, 1)], sem)

    def issue(t, c):
        for k in range(TOP_K):
            row_copy(k, t, dest_ref[k, t]).start(priority=k % 2)
        return c

    lax.fori_loop(0, tc, issue, 0)

    def drain(t, c):
        for k in range(TOP_K):
            row_copy(k, 0, 0).wait()
        return c

    lax.fori_loop(0, tc, drain, 0)

    w = w_ref[...]
    routed = w[:, 0:1] * buf[0]
    for k in range(1, TOP_K):
        routed = routed + w[:, k:k + 1] * buf[k]
    y = x1_ref[...] + mod_ref[5:6, :] * (sh_ref[...] + routed)

    @pl.when(pl.program_id(0) < n0)
    def _():
        oa_ref[...] = y

    @pl.when(pl.program_id(0) >= n0)
    def _():
        ob_ref[...] = y


def _combine(dest3, wt, x1, shared, modp, ys, grp, tc):
    n, d = x1.shape
    bidx, _ = grp.tile_maps(tc)
    tok = pl.BlockSpec((tc, d), lambda i: (i, 0))
    oa_spec, ob_spec, n0 = grp.split_specs(tc, d)
    (b0, s0), (b1, s1) = grp.groups
    return pl.pallas_call(
        functools.partial(_combine_kernel, tc=tc, n0=n0),
        grid=(n // tc,),
        in_specs=[pl.BlockSpec((None, TOP_K, tc), lambda i: (i, 0, 0), memory_space=pltpu.SMEM),
                  pl.BlockSpec((tc, TOP_K), lambda i: (i, 0)),
                  tok, tok,
                  pl.BlockSpec((None, 8, d), lambda i: (bidx(i), 0, 0)),
                  pl.BlockSpec(memory_space=pl.ANY)],
        out_specs=[oa_spec, ob_spec],
        out_shape=[jax.ShapeDtypeStruct((b0 * s0, d), F32), jax.ShapeDtypeStruct((b1 * s1, d), F32)],
        scratch_shapes=[pltpu.VMEM((TOP_K, tc, d), F32), pltpu.SemaphoreType.DMA(())],
        compiler_params=_cparams(1),
        name="combine",
    )(dest3, wt, x1, shared, modp, ys)


def _tok3(a, t):
    k, n = a.shape
    return a.reshape(k, n // t, t).transpose(1, 0, 2)


def _layer(xa, xb, c, grp, norm1_g, w_mod, b_mod, w_in, q_norm_g, k_norm_g, sgu_ln_g, sgu_ln_b, w_s, b_s,
           attn_out_g, sgu_out_g, w_out, norm2_g, w_router, router_bias, w_gate, w_up, w_down,
           ws_gate, ws_up, ws_down):
    n, d = grp.n_tokens, xa.shape[1]
    nb = c.shape[0]
    s_min = min(s for _, s in grp.groups)
    s_max = max(s for _, s in grp.groups)

    c8 = jnp.pad(c, ((0, 8 - nb), (0, 0)))
    mod = _mod(c8, w_mod, b_mod)[:nb].reshape(nb, N_MOD, d)
    modp = jnp.pad(mod, ((0, 0), (0, 8 - N_MOD), (0, 0)))

    perm = np.concatenate([np.arange(0, HEAD_DIM, 2), np.arange(1, HEAD_DIM, 2)])
    c0, c1, c2, c3 = ATTN_WIDTH, ATTN_WIDTH + KV_WIDTH, ATTN_WIDTH + 2 * KV_WIDTH, ATTN_WIDTH + 2 * KV_WIDTH + A_WIDTH
    wq = w_in[:, :c0].reshape(d, N_Q_HEADS, HEAD_DIM)[:, :, perm].reshape(d, ATTN_WIDTH)
    wk = w_in[:, c0:c1].reshape(d, N_KV_HEADS, HEAD_DIM)[:, :, perm].reshape(d, KV_WIDTH)
    wqt = wq.T.astype(BF16)
    wkb = wk.astype(BF16)
    wvt = w_in[:, c1:c2].T.astype(BF16)
    wu = w_in[:, c2:c3].astype(BF16)
    wgv = w_in[:, c3:].astype(BF16)
    qg = q_norm_g[perm].reshape(HEAD_DIM, 1)
    kg = k_norm_g[perm].reshape(1, HEAD_DIM)

    cos2, sin2 = _rope_tables(s_max)
    scale = HEAD_DIM ** -0.5 * LOG2E
    tabs = (jnp.asarray((cos2 * scale).T, F32), jnp.asarray((sin2 * scale).T, F32),
            jnp.asarray(cos2, F32), jnp.asarray(sin2, F32))

    tm = _tile(s_min, 512)
    qt, k, vt, u, gv = _inproj(xa, xb, modp, norm1_g.reshape(1, d), wqt, wkb, wvt, wu, wgv, qg, kg, tabs,
                               grp, tm)

    k_bound2 = HEAD_DIM * jnp.max(jnp.square(k_norm_g)) * BOUND_SLACK
    q_bound = math.sqrt(HEAD_DIM) * scale * BOUND_SLACK * jnp.max(jnp.abs(q_norm_g))
    safe = (q_bound * jnp.sqrt(k_bound2) * BOUND_SLACK < SAFE_SCORE_BOUND).astype(I32).reshape(1)
    attn = _flash(qt, k, vt, jnp.full((8, 128), k_bound2, F32), safe, grp,
                  _tile(s_min, 1024), _tile(s_min, 1024))
    sgun = _sgu(u, gv, sgu_ln_g.reshape(1, A_WIDTH), sgu_ln_b.reshape(1, A_WIDTH), w_s.astype(BF16),
                b_s.T, sgu_out_g.reshape(1, A_WIDTH), _tile(s_min, 512))

    x1, h2, lgt = _outproj(attn, sgun, xa, xb, modp, attn_out_g.reshape(1, ATTN_WIDTH), w_out.astype(BF16),
                           norm2_g.reshape(1, d), w_router.T.astype(BF16), grp, _tile(s_min, 256))

    idx, wts, rank, cnt = _router(lgt, router_bias.reshape(N_EXPERTS, 1), _tile(s_min, 512))

    bm = 512
    counts = cnt[:, 0]
    padded = (counts + bm - 1) // bm * bm
    pad_end = jnp.cumsum(padded).astype(I32)
    pad_start = pad_end - padded
    n_blocks = n * TOP_K // bm + N_EXPERTS
    p_tok = n_blocks * bm
    n_used = (pad_end[-1] // bm).reshape(1).astype(I32)
    block_start = jnp.arange(n_blocks, dtype=I32) * bm
    block_exp = jnp.minimum(jnp.sum((pad_end[None, :] <= block_start[:, None]).astype(I32), axis=1),
                            N_EXPERTS - 1)

    dest = _dest(idx, rank, pad_start, _tile(s_min, 2048))

    td = _tile(s_min, 256)
    xs = _dispatch(counts, pad_start, pad_end, _tok3(dest, td), h2, n, p_tok, td)
    shared = _shared(h2, jnp.concatenate([ws_gate, ws_up], axis=1).astype(BF16), ws_down.astype(BF16),
                     _tile(s_min, 512))
    ys = _experts(block_exp, n_used, xs, w_gate, w_up, w_down, bm)

    tc = _tile(s_min, 128)
    return _combine(_tok3(dest, tc), wts.T, x1, shared, modp, ys, grp, tc)


def kernel(x_prompt, x_sample, c_prompt, c_sample, norm1_g, w_mod, b_mod, w_in, q_norm_g, k_norm_g, sgu_ln_g, sgu_ln_b, w_s, b_s, attn_out_g, sgu_out_g, w_out, norm2_g, w_router, router_bias, w_gate, w_up, w_down, ws_gate, ws_up, ws_down):
    depth = norm1_g.shape[0]
    bp, sp, d = x_prompt.shape
    bs, ss, _ = x_sample.shape
    grp = _Groups([(bp, sp), (bs, ss)])
    xa, xb = x_prompt.reshape(bp * sp, d), x_sample.reshape(bs * ss, d)
    c = jnp.concatenate([c_prompt, c_sample], axis=0)
    for l in range(depth):
        xa, xb = _layer(xa, xb, c, grp, norm1_g[l], w_mod[l], b_mod[l], w_in[l], q_norm_g[l], k_norm_g[l],
                   sgu_ln_g[l], sgu_ln_b[l], w_s[l], b_s[l], attn_out_g[l], sgu_out_g[l], w_out[l],
                   norm2_g[l], w_router[l], router_bias[l], w_gate[l], w_up[l], w_down[l],
                   ws_gate[l], ws_up[l], ws_down[l])
    return (xa.reshape(bp, sp, d), xb.reshape(bs, ss, d))
```

```python
import functools
import math

import numpy as np
import jax
import jax.numpy as jnp
from jax import lax
from jax.experimental import pallas as pl
from jax.experimental.pallas import tpu as pltpu

F32 = jnp.float32
BF16 = jnp.bfloat16
I32 = jnp.int32

HEAD_DIM = 128
N_Q_HEADS = 8
N_KV_HEADS = 2
Q_PER_KV = N_Q_HEADS // N_KV_HEADS
ATTN_WIDTH = N_Q_HEADS * HEAD_DIM
KV_WIDTH = N_KV_HEADS * HEAD_DIM
ROPE_THETA = 10000.0
GRID_W = 64
N_A_HEADS = 8
A_HEAD_DIM = 128
A_WIDTH = N_A_HEADS * A_HEAD_DIM
CHUNK = 128
N_EXPERTS = 64
TOP_K = 8
N_GROUPS = 8
GROUP_SIZE = N_EXPERTS // N_GROUPS
TOPK_GROUPS = 4
ROUTED_SCALE = 2.5
N_MOD = 6
EPS = 1e-6

VMEM_LIMIT_BYTES = 56 * 1024 * 1024
NT_DIMS = (((1,), (1,)), ((), ()))


def _cparams(n_axes):
    return pltpu.CompilerParams(
        dimension_semantics=("arbitrary",) * n_axes, vmem_limit_bytes=VMEM_LIMIT_BYTES)


def _tile(n, pref):
    t = min(n, pref)
    while n % t:
        t //= 2
    return t


def _const_spec(shape):
    nd = len(shape)
    return pl.BlockSpec(shape, lambda *a: (0,) * nd)


def _mod_kernel(c_ref, w_ref, b_ref, o_ref):
    c = c_ref[...]
    a = (c * jax.nn.sigmoid(c)).astype(BF16)
    o_ref[...] = jnp.dot(a, w_ref[...].astype(BF16), preferred_element_type=F32) + b_ref[...]


def _mod(c8, w_mod, b_mod):
    d, n = w_mod.shape
    tn = _tile(n, 1024)
    return pl.pallas_call(
        _mod_kernel,
        grid=(n // tn,),
        in_specs=[_const_spec((8, d)),
                  pl.BlockSpec((d, tn), lambda j: (0, j)),
                  pl.BlockSpec((1, tn), lambda j: (0, j))],
        out_specs=pl.BlockSpec((8, tn), lambda j: (0, j)),
        out_shape=jax.ShapeDtypeStruct((8, n), F32),
        compiler_params=_cparams(1),
        name="mod",
    )(c8, w_mod, b_mod.reshape(1, n))


class _Groups:
    def __init__(self, groups):
        self.groups = groups
        self.n_tokens = sum(b * s for b, s in groups)

    def tile_maps(self, t):
        (b0, s0), (b1, s1) = self.groups
        n0 = b0 * s0 // t
        t0, t1 = s0 // t, s1 // t

        def bidx(i):
            return jnp.where(i < n0, i // t0, b0 + (i - n0) // t1)

        def pidx(i):
            return jnp.where(i < n0, i % t0, (i - n0) % t1)

        return bidx, pidx

    def split_specs(self, t, d):
        (b0, s0), _ = self.groups
        n0 = b0 * s0 // t
        return (pl.BlockSpec((t, d), lambda i, *_: (jnp.minimum(i, n0 - 1), 0)),
                pl.BlockSpec((t, d), lambda i, *_: (jnp.maximum(i - n0, 0), 0)), n0)


def _pick_group(i, n0, a_ref, b_ref):
    return jnp.where(i < n0, a_ref[...], b_ref[...])


LANES = 128
SUBLANES = 8
U32 = jnp.uint32


def _slab_rows(d):
    rows = d // (2 * LANES)
    assert d % (2 * LANES) == 0 and rows % SUBLANES == 0, d
    return rows


def _load_slabs(ref, rows, tok0, n_tok):
    pairs = [_unpack_bf16_pair(ref[pl.ds(tok0 * rows + r, n_tok, stride=rows), :]) for r in range(rows)]
    return jnp.concatenate([p[0] for p in pairs] + [p[1] for p in pairs], axis=1).astype(BF16)


def _pack_bf16_pair(lo, hi):
    def bf16_bits(v):
        return lax.bitcast_convert_type(v.astype(BF16).astype(F32), U32)

    return (bf16_bits(lo) >> 16) | bf16_bits(hi)


def _unpack_bf16_pair(w):
    return (lax.bitcast_convert_type(w << 16, F32),
            lax.bitcast_convert_type(w & jnp.uint32(0xFFFF0000), F32))


def _store_slabs(ref, x, tok0=0):
    n_tok, d = x.shape
    rows = _slab_rows(d)
    for r in range(rows):
        ref[pl.ds(tok0 * rows + r, n_tok, stride=rows), :] = _pack_bf16_pair(
            x[:, r * LANES:(r + 1) * LANES], x[:, (rows + r) * LANES:(rows + r + 1) * LANES])


def _rope_tables(seq_len):
    pos = np.arange(seq_len)
    row = (pos // GRID_W).astype(np.float64)
    col = (pos % GRID_W).astype(np.float64)
    n_pairs = HEAD_DIM // 4
    freqs = ROPE_THETA ** (-np.arange(n_pairs, dtype=np.float64) / n_pairs)
    ang = np.concatenate([row[:, None] * freqs, col[:, None] * freqs], axis=-1)
    c, s = np.cos(ang), np.sin(ang)
    cos2 = np.concatenate([c, c], axis=-1)
    sin2 = np.concatenate([-s, s], axis=-1)
    return cos2, sin2


def _inproj_kernel(xa_ref, xb_ref, mod_ref, g1_ref, wqt_ref, wk_ref, wvt_ref, wu_ref, wgv_ref,
                   qg_ref, kg_ref, cost_ref, sint_ref, cos_ref, sin_ref,
                   qt_ref, k_ref, vt_ref, u_ref, gv_ref, *, n0):
    x = _pick_group(pl.program_id(0), n0, xa_ref, xb_ref)
    shift1 = mod_ref[0:1, :]
    scale1 = mod_ref[1:2, :]
    ms = jnp.mean(x * x, axis=-1, keepdims=True)
    h = (x * lax.rsqrt(ms + EPS) * g1_ref[...]) * (1.0 + scale1) + shift1
    hb = h.astype(BF16)
    half = HEAD_DIM // 2

    qt = lax.dot_general(wqt_ref[...], hb, NT_DIMS, preferred_element_type=F32)
    cost = cost_ref[...]
    sint = sint_ref[...]
    qg = qg_ref[...]
    for hh in range(N_Q_HEADS):
        qh = qt[hh * HEAD_DIM:(hh + 1) * HEAD_DIM, :]
        msq = jnp.mean(qh * qh, axis=0, keepdims=True)
        qn = qh * lax.rsqrt(msq + EPS) * qg
        rot = jnp.concatenate([qn[half:, :], qn[:half, :]], axis=0)
        qt_ref[hh * HEAD_DIM:(hh + 1) * HEAD_DIM, :] = (qn * cost + rot * sint).astype(BF16)

    vt_ref[...] = lax.dot_general(wvt_ref[...], hb, NT_DIMS, preferred_element_type=F32).astype(BF16)

    kk = jnp.dot(hb, wk_ref[...], preferred_element_type=F32)
    cos = cos_ref[...]
    sin = sin_ref[...]
    kg = kg_ref[...]
    for j in range(N_KV_HEADS):
        kh = kk[:, j * HEAD_DIM:(j + 1) * HEAD_DIM]
        msk = jnp.mean(kh * kh, axis=-1, keepdims=True)
        kn = kh * lax.rsqrt(msk + EPS) * kg
        rot = jnp.concatenate([kn[:, half:], kn[:, :half]], axis=1)
        k_ref[:, j * HEAD_DIM:(j + 1) * HEAD_DIM] = (kn * cos + rot * sin).astype(BF16)

    u_ref[...] = jnp.dot(hb, wu_ref[...], preferred_element_type=F32).astype(BF16)
    gv_ref[...] = jnp.dot(hb, wgv_ref[...], preferred_element_type=F32).astype(BF16)


def _inproj(xa, xb, modp, g1, wqt, wk, wvt, wu, wgv, qg, kg, tabs, grp, tm):
    n, d = grp.n_tokens, xa.shape[1]
    bidx, pidx = grp.tile_maps(tm)
    cost, sint, cos, sin = tabs
    tok = lambda w: pl.BlockSpec((tm, w), lambda i: (i, 0))
    tokt = lambda w: pl.BlockSpec((w, tm), lambda i: (0, i))
    xa_spec, xb_spec, n0 = grp.split_specs(tm, d)
    return pl.pallas_call(
        functools.partial(_inproj_kernel, n0=n0),
        grid=(n // tm,),
        in_specs=[xa_spec, xb_spec,
                  pl.BlockSpec((None, 8, d), lambda i: (bidx(i), 0, 0)),
                  _const_spec((1, d)),
                  _const_spec(wqt.shape), _const_spec(wk.shape), _const_spec(wvt.shape),
                  _const_spec(wu.shape), _const_spec(wgv.shape),
                  _const_spec((HEAD_DIM, 1)), _const_spec((1, HEAD_DIM)),
                  pl.BlockSpec((HEAD_DIM, tm), lambda i: (0, pidx(i))),
                  pl.BlockSpec((HEAD_DIM, tm), lambda i: (0, pidx(i))),
                  pl.BlockSpec((tm, HEAD_DIM), lambda i: (pidx(i), 0)),
                  pl.BlockSpec((tm, HEAD_DIM), lambda i: (pidx(i), 0))],
        out_specs=[tokt(ATTN_WIDTH), tok(KV_WIDTH), tokt(KV_WIDTH), tok(A_WIDTH), tok(A_WIDTH)],
        out_shape=[jax.ShapeDtypeStruct((ATTN_WIDTH, n), BF16),
                   jax.ShapeDtypeStruct((n, KV_WIDTH), BF16),
                   jax.ShapeDtypeStruct((KV_WIDTH, n), BF16),
                   jax.ShapeDtypeStruct((n, A_WIDTH), BF16),
                   jax.ShapeDtypeStruct((n, A_WIDTH), BF16)],
        compiler_params=_cparams(1),
        name="inproj",
    )(xa, xb, modp, g1, wqt, wk, wvt, wu, wgv, qg, kg, cost, sint, cos, sin)


LOG2E = 1.4426950408889634
SAFE_SCORE_BOUND = 60.0
BOUND_SLACK = 1.01


def _flash_kernel(qt_s, kt_s, vp_s, fl_s, safe_s, q_ref, k_ref, v_ref, vp_ref, km_ref, o_ref,
                  qs_sc, m_sc, l_sc, acc_sc, p_sc, *, bq, hb):
    s_id = pl.program_id(1)
    flag = fl_s[s_id]
    safe = safe_s[0] != 0
    last = (flag & 2) != 0

    @pl.when((flag & 1) != 0)
    def _init():
        l_sc[...] = jnp.zeros(l_sc.shape, F32)
        acc_sc[...] = jnp.zeros(acc_sc.shape, F32)
        p_sc[...] = jnp.zeros(p_sc.shape, BF16)
        for hh in range(Q_PER_KV):
            qs_sc[:, hh * bq:(hh + 1) * bq] = q_ref[hh * HEAD_DIM:(hh + 1) * HEAD_DIM, :]
        qf = qs_sc[...].astype(F32)
        qn2 = jnp.sum(qf * qf, axis=0, keepdims=True)
        bound = jnp.sqrt(qn2 * km_ref[0:1, 0:1]) * BOUND_SLACK
        m_sc[...] = jnp.where(safe, bound, -jnp.inf)

    @pl.when(safe)
    def _fast():
        q = qs_sc[...]
        m = m_sc[...]
        sa = jnp.dot(k_ref[:hb, :], q, preferred_element_type=F32)
        acc = acc_sc[...] + jnp.dot(vp_ref[...], p_sc[...], preferred_element_type=F32)
        pa = jnp.exp2(sa - m)
        la = jnp.sum(pa, axis=0, keepdims=True)
        sb = jnp.dot(k_ref[hb:, :], q, preferred_element_type=F32)
        acc_sc[...] = acc + jnp.dot(v_ref[:, :hb], pa.astype(BF16), preferred_element_type=F32)
        pb = jnp.exp2(sb - m)
        l_sc[...] += la + jnp.sum(pb, axis=0, keepdims=True)
        p_sc[...] = pb.astype(BF16)

    @pl.when(jnp.logical_and(safe, last))
    def _flush():
        acc_sc[...] += jnp.dot(v_ref[:, hb:], p_sc[...], preferred_element_type=F32)

    @pl.when(jnp.logical_not(safe))
    def _online():
        s = jnp.dot(k_ref[...], qs_sc[...], preferred_element_type=F32)
        m_prev = m_sc[...]
        m_new = jnp.maximum(m_prev, jnp.max(s, axis=0, keepdims=True))
        alpha = jnp.exp2(m_prev - m_new)
        p = jnp.exp2(s - m_new)
        l_sc[...] = alpha * l_sc[...] + jnp.sum(p, axis=0, keepdims=True)
        acc_sc[...] = alpha * acc_sc[...] + jnp.dot(v_ref[...], p.astype(BF16), preferred_element_type=F32)
        m_sc[...] = m_new

    @pl.when((flag & 2) != 0)
    def _fin():
        o = acc_sc[...] / l_sc[...]
        for hh in range(Q_PER_KV):
            o_ref[:, hh * HEAD_DIM:(hh + 1) * HEAD_DIM] = o[:, hh * bq:(hh + 1) * bq].T.astype(BF16)


def _flash_schedule(grp, bq, bkv):
    qt, kt, vp, fl = [], [], [], []
    off = 0
    for b, s in grp.groups:
        for bi in range(b):
            base = off + bi * s
            nk = s // bkv
            for qi in range(s // bq):
                for ki in range(nk):
                    blk = base // bkv + ki
                    qt.append(base // bq + qi)
                    kt.append(blk)
                    vp.append(2 * (blk - 1 if ki else blk) + 1)
                    fl.append((1 if ki == 0 else 0) | (2 if ki == nk - 1 else 0))
        off += b * s
    return tuple(np.asarray(a, np.int32) for a in (qt, kt, vp, fl))


def _flash(qt, k, vt, kmax2, safe, grp, bq, bkv):
    n = k.shape[0]
    hb = bkv // 2
    qt_a, kt_a, vp_a, fl_a = _flash_schedule(grp, bq, bkv)
    n_steps = len(qt_a)
    w = Q_PER_KV * HEAD_DIM
    gs = pltpu.PrefetchScalarGridSpec(
        num_scalar_prefetch=5,
        grid=(N_KV_HEADS, n_steps),
        in_specs=[pl.BlockSpec((w, bq), lambda h, s, qa, ka, va, fa, sa: (h, qa[s])),
                  pl.BlockSpec((bkv, HEAD_DIM), lambda h, s, qa, ka, va, fa, sa: (ka[s], h)),
                  pl.BlockSpec((HEAD_DIM, bkv), lambda h, s, qa, ka, va, fa, sa: (h, ka[s])),
                  pl.BlockSpec((HEAD_DIM, hb), lambda h, s, qa, ka, va, fa, sa: (h, va[s])),
                  pl.BlockSpec((8, 128), lambda h, s, qa, ka, va, fa, sa: (0, 0))],
        out_specs=pl.BlockSpec((bq, w), lambda h, s, qa, ka, va, fa, sa: (qa[s], h)),
        scratch_shapes=[pltpu.VMEM((HEAD_DIM, Q_PER_KV * bq), BF16),
                        pltpu.VMEM((1, Q_PER_KV * bq), F32),
                        pltpu.VMEM((1, Q_PER_KV * bq), F32),
                        pltpu.VMEM((HEAD_DIM, Q_PER_KV * bq), F32),
                        pltpu.VMEM((hb, Q_PER_KV * bq), BF16)])
    return pl.pallas_call(
        functools.partial(_flash_kernel, bq=bq, hb=hb),
        grid_spec=gs,
        out_shape=jax.ShapeDtypeStruct((n, ATTN_WIDTH), BF16),
        compiler_params=_cparams(2),
        name="flash",
    )(jnp.asarray(qt_a), jnp.asarray(kt_a), jnp.asarray(vp_a), jnp.asarray(fl_a), safe, qt, k, vt, vt, kmax2)


def _sgu_kernel(u_ref, gv_ref, lng_ref, lnb_ref, ws_ref, bst_ref, og_ref, o_ref, *, n_chunks):
    v = gv_ref[...].astype(F32)
    mean = jnp.mean(v, axis=-1, keepdims=True)
    vc = v - mean
    var = jnp.mean(vc * vc, axis=-1, keepdims=True)
    vn = (vc * lax.rsqrt(var + EPS) * lng_ref[...] + lnb_ref[...]).astype(BF16)
    u = u_ref[...].astype(F32)
    heads = []
    for hh in range(N_A_HEADS):
        cols = slice(hh * A_HEAD_DIM, (hh + 1) * A_HEAD_DIM)
        rhs = jnp.concatenate([vn[c * CHUNK:(c + 1) * CHUNK, cols] for c in range(n_chunks)], axis=1)
        mixed = jnp.dot(ws_ref[hh], rhs, preferred_element_type=F32) + bst_ref[:, hh:hh + 1]
        heads.append(jnp.concatenate(
            [mixed[:, c * A_HEAD_DIM:(c + 1) * A_HEAD_DIM] for c in range(n_chunks)], axis=0))
    sg = u * jnp.concatenate(heads, axis=1)
    ms = jnp.mean(sg * sg, axis=-1, keepdims=True)
    o_ref[...] = (sg * lax.rsqrt(ms + EPS) * og_ref[...]).astype(BF16)


def _sgu(u, gv, lng, lnb, ws, bst, og, tm):
    n = u.shape[0]
    tok = pl.BlockSpec((tm, A_WIDTH), lambda i: (i, 0))
    return pl.pallas_call(
        functools.partial(_sgu_kernel, n_chunks=tm // CHUNK),
        grid=(n // tm,),
        in_specs=[tok, tok, _const_spec((1, A_WIDTH)), _const_spec((1, A_WIDTH)),
                  _const_spec(ws.shape), _const_spec(bst.shape), _const_spec((1, A_WIDTH))],
        out_specs=tok,
        out_shape=jax.ShapeDtypeStruct((n, A_WIDTH), BF16),
        compiler_params=_cparams(1),
        name="sgu",
    )(u, gv, lng, lnb, ws, bst, og)


def _outproj_kernel(attn_ref, sgu_ref, xa_ref, xb_ref, mod_ref, ag_ref, wo_ref, g2_ref, wrt_ref,
                    x1_ref, h2_ref, lgt_ref, *, n0):
    a = attn_ref[...].astype(F32)
    ms = jnp.mean(a * a, axis=-1, keepdims=True)
    an = (a * lax.rsqrt(ms + EPS) * ag_ref[...]).astype(BF16)
    o = jnp.dot(an, wo_ref[:ATTN_WIDTH, :], preferred_element_type=F32)
    o = o + jnp.dot(sgu_ref[...], wo_ref[ATTN_WIDTH:, :], preferred_element_type=F32)
    gate1 = mod_ref[2:3, :]
    shift2 = mod_ref[3:4, :]
    scale2 = mod_ref[4:5, :]
    x1 = _pick_group(pl.program_id(0), n0, xa_ref, xb_ref) + gate1 * o
    x1_ref[...] = x1
    ms2 = jnp.mean(x1 * x1, axis=-1, keepdims=True)
    h2 = (x1 * lax.rsqrt(ms2 + EPS) * g2_ref[...]) * (1.0 + scale2) + shift2
    _store_slabs(h2_ref, h2)
    lgt_ref[...] = lax.dot_general(wrt_ref[...], h2.astype(BF16), NT_DIMS, preferred_element_type=F32)


def _outproj(attn, sgun, xa, xb, modp, ag, wo, g2, wrt, grp, tm):
    n, d = grp.n_tokens, xa.shape[1]
    bidx, _ = grp.tile_maps(tm)
    tok = lambda w: pl.BlockSpec((tm, w), lambda i: (i, 0))
    xa_spec, xb_spec, n0 = grp.split_specs(tm, d)
    return pl.pallas_call(
        functools.partial(_outproj_kernel, n0=n0),
        grid=(n // tm,),
        in_specs=[tok(ATTN_WIDTH), tok(A_WIDTH), xa_spec, xb_spec,
                  pl.BlockSpec((None, 8, d), lambda i: (bidx(i), 0, 0)),
                  _const_spec((1, ATTN_WIDTH)), _const_spec(wo.shape), _const_spec((1, d)),
                  _const_spec(wrt.shape)],
        out_specs=[tok(d), pl.BlockSpec((tm * _slab_rows(d), LANES), lambda i: (i, 0)),
                   pl.BlockSpec((N_EXPERTS, tm), lambda i: (0, i))],
        out_shape=[jax.ShapeDtypeStruct((n, d), F32),
                   jax.ShapeDtypeStruct((n * _slab_rows(d), LANES), U32),
                   jax.ShapeDtypeStruct((N_EXPERTS, n), F32)],
        compiler_params=_cparams(1),
        name="outproj",
    )(attn, sgun, xa, xb, modp, ag, wo, g2, wrt)


def _first_index(hit, iota, sentinel):
    return jnp.min(jnp.where(hit, iota, sentinel), axis=0, keepdims=True)


def _router_kernel(lg_ref, bias_ref, idx_ref, w_ref, rank_ref, cnt_ref, run_sc):
    i = pl.program_id(0)
    tt = lg_ref.shape[1]

    @pl.when(i == 0)
    def _init():
        run_sc[...] = jnp.zeros(run_sc.shape, F32)

    scores = jax.nn.sigmoid(lg_ref[...])
    biased = scores + bias_ref[...]
    neg = -jnp.inf
    iota_g = lax.broadcasted_iota(I32, (GROUP_SIZE, tt), 0).astype(F32)

    gs_rows = []
    for g in range(N_GROUPS):
        xg = biased[g * GROUP_SIZE:(g + 1) * GROUP_SIZE, :]
        m1 = jnp.max(xg, axis=0, keepdims=True)
        f1 = _first_index(xg == m1, iota_g, float(GROUP_SIZE))
        m2 = jnp.max(jnp.where(iota_g == f1, neg, xg), axis=0, keepdims=True)
        gs_rows.append(m1 + m2)
    gs = jnp.concatenate(gs_rows, axis=0)

    iota_n = lax.broadcasted_iota(I32, (N_GROUPS, tt), 0).astype(F32)
    gsel = jnp.zeros((N_GROUPS, tt), F32)
    cur = gs
    for _ in range(TOPK_GROUPS):
        m = jnp.max(cur, axis=0, keepdims=True)
        f = _first_index(cur == m, iota_n, float(N_GROUPS))
        hit = iota_n == f
        gsel = jnp.where(hit, 1.0, gsel)
        cur = jnp.where(hit, neg, cur)

    masked = jnp.concatenate(
        [jnp.where(gsel[g:g + 1, :] > 0.5, biased[g * GROUP_SIZE:(g + 1) * GROUP_SIZE, :], neg)
         for g in range(N_GROUPS)], axis=0)

    iota_e = lax.broadcasted_iota(I32, (N_EXPERTS, tt), 0).astype(F32)
    sel = jnp.zeros((N_EXPERTS, tt), F32)
    idx_rows, w_rows = [], []
    for _ in range(TOP_K):
        m = jnp.max(masked, axis=0, keepdims=True)
        f = _first_index(masked == m, iota_e, float(N_EXPERTS))
        hit = iota_e == f
        idx_rows.append(f)
        w_rows.append(jnp.sum(jnp.where(hit, scores, 0.0), axis=0, keepdims=True))
        sel = jnp.where(hit, 1.0, sel)
        masked = jnp.where(hit, neg, masked)
    idx = jnp.concatenate(idx_rows, axis=0)
    wk = jnp.concatenate(w_rows, axis=0)
    wk = wk / (jnp.sum(wk, axis=0, keepdims=True) + 1e-20) * ROUTED_SCALE

    r = lax.broadcasted_iota(I32, (tt, tt), 0)
    c = lax.broadcasted_iota(I32, (tt, tt), 1)
    upper = jnp.where(r < c, 1.0, 0.0).astype(BF16)
    prefix = jnp.dot(sel.astype(BF16), upper, preferred_element_type=F32) + run_sc[...]
    rank_rows = [jnp.sum(jnp.where(iota_e == idx[k:k + 1, :], prefix, 0.0), axis=0, keepdims=True)
                 for k in range(TOP_K)]
    run_sc[...] = run_sc[...] + jnp.sum(sel, axis=1, keepdims=True)

    idx_ref[...] = idx.astype(I32)
    w_ref[...] = wk
    rank_ref[...] = jnp.concatenate(rank_rows, axis=0).astype(I32)
    cnt_ref[...] = jnp.broadcast_to(run_sc[...], cnt_ref.shape).astype(I32)


def _router(lgt, bias, tt):
    n = lgt.shape[1]
    tokk = pl.BlockSpec((TOP_K, tt), lambda i: (0, i))
    return pl.pallas_call(
        _router_kernel,
        grid=(n // tt,),
        in_specs=[pl.BlockSpec((N_EXPERTS, tt), lambda i: (0, i)), _const_spec((N_EXPERTS, 1))],
        out_specs=[tokk, tokk, tokk, _const_spec((N_EXPERTS, 128))],
        out_shape=[jax.ShapeDtypeStruct((TOP_K, n), I32), jax.ShapeDtypeStruct((TOP_K, n), F32),
                   jax.ShapeDtypeStruct((TOP_K, n), I32), jax.ShapeDtypeStruct((N_EXPERTS, 128), I32)],
        scratch_shapes=[pltpu.VMEM((N_EXPERTS, 1), F32)],
        compiler_params=_cparams(1),
        name="router",
    )(lgt, bias)


def _dest_kernel(idx_ref, rank_ref, start_ref, o_ref):
    tt = idx_ref.shape[1]
    iota_e = lax.broadcasted_iota(I32, (N_EXPERTS, tt), 0)
    start = start_ref[...]
    rows = []
    for k in range(TOP_K):
        hit = iota_e == idx_ref[k:k + 1, :]
        rows.append(jnp.sum(jnp.where(hit, start, 0), axis=0, keepdims=True))
    o_ref[...] = jnp.concatenate(rows, axis=0) + rank_ref[...]


def _dest(idx, rank, pad_start, tt):
    n = idx.shape[1]
    tokk = pl.BlockSpec((TOP_K, tt), lambda i: (0, i))
    return pl.pallas_call(
        _dest_kernel,
        grid=(n // tt,),
        in_specs=[tokk, tokk, _const_spec((N_EXPERTS, 1))],
        out_specs=tokk,
        out_shape=jax.ShapeDtypeStruct((TOP_K, n), I32),
        compiler_params=_cparams(1),
        name="dest",
    )(idx, rank, pad_start.reshape(N_EXPERTS, 1))


def _slab(ref, tok, rows):
    return ref.at[pl.ds(pl.multiple_of(tok * rows, SUBLANES), rows)]


def _dispatch_kernel(cnt_s, start_s, pend_s, dest_ref, h_ref, xs_ref, zero_sc, sem, *, td):
    i = pl.program_id(0)
    rows = zero_sc.shape[0]

    def slab_copy(t, slot):
        return pltpu.make_async_copy(_slab(h_ref, t, rows), _slab(xs_ref, slot, rows), sem)

    def issue(t, c):
        for k in range(TOP_K):
            slab_copy(t, dest_ref[k, t]).start(priority=k % 2)
        return c

    lax.fori_loop(0, td, issue, 0)

    def drain(t, c):
        for k in range(TOP_K):
            slab_copy(0, 0).wait()
        return c

    lax.fori_loop(0, td, drain, 0)

    @pl.when(i == pl.num_programs(0) - 1)
    def _pad():
        zero_sc[...] = jnp.zeros(zero_sc.shape, U32)

        def pad_copy(p):
            return pltpu.make_async_copy(zero_sc, _slab(xs_ref, p, rows), sem)

        def per_expert(e, c):
            lo = start_s[e] + cnt_s[e]
            hi = pend_s[e]

            def go(p, c2):
                pad_copy(p).start()
                return c2

            lax.fori_loop(lo, hi, go, 0)

            def done(p, c2):
                pad_copy(p).wait()
                return c2

            lax.fori_loop(lo, hi, done, 0)
            return c

        lax.fori_loop(0, N_EXPERTS, per_expert, 0)


def _dispatch(counts, pad_start, pad_end, dest3, h2s, n, p_tok, td):
    rows = h2s.shape[0] // n
    gs = pltpu.PrefetchScalarGridSpec(
        num_scalar_prefetch=3,
        grid=(n // td,),
        in_specs=[pl.BlockSpec((None, TOP_K, td), lambda i, *_: (i, 0, 0), memory_space=pltpu.SMEM),
                  pl.BlockSpec((td * rows, LANES), lambda i, *_: (i, 0))],
        out_specs=pl.BlockSpec(memory_space=pl.ANY),
        scratch_shapes=[pltpu.VMEM((rows, LANES), U32), pltpu.SemaphoreType.DMA(())])
    return pl.pallas_call(
        functools.partial(_dispatch_kernel, td=td),
        grid_spec=gs,
        out_shape=jax.ShapeDtypeStruct((p_tok * rows, LANES), U32),
        compiler_params=_cparams(1),
        name="dispatch",
    )(counts, pad_start, pad_end, dest3, h2s)


def _swiglu_slabs(x_ref, wgu_ref, wd_ref, o_ref, *, packed_out):
    de, d = wd_ref.shape
    rows = _slab_rows(d)
    half = x_ref.shape[0] // rows // 2
    ups = [jnp.dot(_load_slabs(x_ref, rows, t0, half), wgu_ref[...],
                   preferred_element_type=F32) for t0 in (0, half)]
    for t0, r in zip((0, half), ups):
        a = r[:, :de]
        act = (a * jax.nn.sigmoid(a) * r[:, de:]).astype(BF16)
        out = jnp.dot(act, wd_ref[...], preferred_element_type=F32)
        if packed_out:
            _store_slabs(o_ref, out, t0)
        else:
            o_ref[t0:t0 + half, :] = out


def _shared_kernel(h_ref, wgu_ref, wd_ref, o_ref):
    _swiglu_slabs(h_ref, wgu_ref, wd_ref, o_ref, packed_out=False)


def _shared(h2s, wgu, wd, tm):
    d = wd.shape[1]
    rows = _slab_rows(d)
    n = h2s.shape[0] // rows
    return pl.pallas_call(
        _shared_kernel,
        grid=(n // tm,),
        in_specs=[pl.BlockSpec((tm * rows, LANES), lambda i: (i, 0)),
                  _const_spec(wgu.shape), _const_spec(wd.shape)],
        out_specs=pl.BlockSpec((tm, d), lambda i: (i, 0)),
        out_shape=jax.ShapeDtypeStruct((n, d), F32),
        compiler_params=_cparams(1),
        name="shared",
    )(h2s, wgu, wd)


def _experts_kernel(be_s, nu_s, x_ref, wg_ref, wu_ref, wd_ref, o_ref, wgu_sc, wd_sc):
    i = pl.program_id(0)
    used = i < nu_s[0]
    prev = be_s[jnp.maximum(i - 1, 0)]
    fresh = jnp.logical_or(i == 0, be_s[i] != prev)
    de = wd_sc.shape[0]

    @pl.when(jnp.logical_and(used, fresh))
    def _cast():
        wgu_sc[:, :de] = wg_ref[...].astype(BF16)
        wgu_sc[:, de:] = wu_ref[...].astype(BF16)
        wd_sc[...] = wd_ref[...].astype(BF16)

    @pl.when(used)
    def _run():
        _swiglu_slabs(x_ref, wgu_sc, wd_sc, o_ref, packed_out=True)


def _experts(block_exp, n_used, xs, w_gate, w_up, w_down, bm):
    d, de = w_gate.shape[1], w_gate.shape[2]
    rows = _slab_rows(d)
    p_tok = xs.shape[0] // rows
    n_blocks = p_tok // bm
    row = lambda i, be, nu: (jnp.minimum(i, nu[0] - 1), 0)
    wsel = lambda i, be, nu: (be[jnp.minimum(i, nu[0] - 1)], 0, 0)
    gs = pltpu.PrefetchScalarGridSpec(
        num_scalar_prefetch=2,
        grid=(n_blocks,),
        in_specs=[pl.BlockSpec((bm * rows, LANES), row),
                  pl.BlockSpec((None, d, de), wsel),
                  pl.BlockSpec((None, d, de), wsel),
                  pl.BlockSpec((None, de, d), wsel)],
        out_specs=pl.BlockSpec((bm * rows, LANES), row),
        scratch_shapes=[pltpu.VMEM((d, 2 * de), BF16), pltpu.VMEM((de, d), BF16)])
    return pl.pallas_call(
        _experts_kernel,
        grid_spec=gs,
        out_shape=jax.ShapeDtypeStruct(xs.shape, U32),
        compiler_params=_cparams(1),
        name="experts",
    )(block_exp, n_used, xs, w_gate, w_up, w_down)


def _combine_kernel(dest_ref, w_ref, x1_ref, sh_ref, mod_ref, ys_ref, oa_ref, ob_ref,
                    buf, lo_sc, hi_sc, sem, *, tc, n0):
    rows = lo_sc.shape[0] // tc

    def slab_copy(k, t, slot):
        return pltpu.make_async_copy(_slab(ys_ref, slot, rows), _slab(buf.at[k], t, rows), sem)

    def issue(t, c):
        for k in range(TOP_K):
            slab_copy(k, t, dest_ref[k, t]).start(priority=k % 2)
        return c

    lax.fori_loop(0, tc, issue, 0)

    def drain(t, c):
        for k in range(TOP_K):
            slab_copy(k, 0, 0).wait()
        return c

    lax.fori_loop(0, tc, drain, 0)

    def token_sum(t, c):
        r0 = pl.multiple_of(t * rows, SUBLANES)
        lo = hi = None
        for k in range(TOP_K):
            lo_k, hi_k = _unpack_bf16_pair(buf[k, pl.ds(r0, rows), :])
            wk = w_ref[k, t]
            lo = wk * lo_k if lo is None else lo + wk * lo_k
            hi = wk * hi_k if hi is None else hi + wk * hi_k
        lo_sc[pl.ds(r0, rows), :] = lo
        hi_sc[pl.ds(r0, rows), :] = hi
        return c

    lax.fori_loop(0, tc, token_sum, 0, unroll=4)

    routed = jnp.concatenate([lo_sc[pl.ds(r, tc, stride=rows), :] for r in range(rows)]
                             + [hi_sc[pl.ds(r, tc, stride=rows), :] for r in range(rows)], axis=1)
    y = x1_ref[...] + mod_ref[5:6, :] * (sh_ref[...] + routed)

    @pl.when(pl.program_id(0) < n0)
    def _():
        oa_ref[...] = y

    @pl.when(pl.program_id(0) >= n0)
    def _():
        ob_ref[...] = y


def _combine(dest3, w3, x1, shared, modp, ys, grp, tc):
    n, d = x1.shape
    rows = _slab_rows(d)
    bidx, _ = grp.tile_maps(tc)
    tok = pl.BlockSpec((tc, d), lambda i: (i, 0))
    oa_spec, ob_spec, n0 = grp.split_specs(tc, d)
    (b0, s0), (b1, s1) = grp.groups
    return pl.pallas_call(
        functools.partial(_combine_kernel, tc=tc, n0=n0),
        grid=(n // tc,),
        in_specs=[pl.BlockSpec((None, TOP_K, tc), lambda i: (i, 0, 0), memory_space=pltpu.SMEM),
                  pl.BlockSpec((None, TOP_K, tc), lambda i: (i, 0, 0), memory_space=pltpu.SMEM),
                  tok, tok,
                  pl.BlockSpec((None, 8, d), lambda i: (bidx(i), 0, 0)),
                  pl.BlockSpec(memory_space=pl.ANY)],
        out_specs=[oa_spec, ob_spec],
        out_shape=[jax.ShapeDtypeStruct((b0 * s0, d), F32), jax.ShapeDtypeStruct((b1 * s1, d), F32)],
        scratch_shapes=[pltpu.VMEM((TOP_K, tc * rows, LANES), U32),
                        pltpu.VMEM((tc * rows, LANES), F32), pltpu.VMEM((tc * rows, LANES), F32),
                        pltpu.SemaphoreType.DMA(())],
        compiler_params=_cparams(1),
        name="combine",
    )(dest3, w3, x1, shared, modp, ys)


def _tok3(a, t):
    k, n = a.shape
    return a.reshape(k, n // t, t).transpose(1, 0, 2)


def _layer(xa, xb, c, grp, norm1_g, w_mod, b_mod, w_in, q_norm_g, k_norm_g, sgu_ln_g, sgu_ln_b, w_s, b_s,
           attn_out_g, sgu_out_g, w_out, norm2_g, w_router, router_bias, w_gate, w_up, w_down,
           ws_gate, ws_up, ws_down):
    n, d = grp.n_tokens, xa.shape[1]
    nb = c.shape[0]
    s_min = min(s for _, s in grp.groups)
    s_max = max(s for _, s in grp.groups)

    c8 = jnp.pad(c, ((0, 8 - nb), (0, 0)))
    mod = _mod(c8, w_mod, b_mod)[:nb].reshape(nb, N_MOD, d)
    modp = jnp.pad(mod, ((0, 0), (0, 8 - N_MOD), (0, 0)))

    perm = np.concatenate([np.arange(0, HEAD_DIM, 2), np.arange(1, HEAD_DIM, 2)])
    c0, c1, c2, c3 = ATTN_WIDTH, ATTN_WIDTH + KV_WIDTH, ATTN_WIDTH + 2 * KV_WIDTH, ATTN_WIDTH + 2 * KV_WIDTH + A_WIDTH
    wq = w_in[:, :c0].reshape(d, N_Q_HEADS, HEAD_DIM)[:, :, perm].reshape(d, ATTN_WIDTH)
    wk = w_in[:, c0:c1].reshape(d, N_KV_HEADS, HEAD_DIM)[:, :, perm].reshape(d, KV_WIDTH)
    wqt = wq.T.astype(BF16)
    wkb = wk.astype(BF16)
    wvt = w_in[:, c1:c2].T.astype(BF16)
    wu = w_in[:, c2:c3].astype(BF16)
    wgv = w_in[:, c3:].astype(BF16)
    qg = q_norm_g[perm].reshape(HEAD_DIM, 1)
    kg = k_norm_g[perm].reshape(1, HEAD_DIM)

    cos2, sin2 = _rope_tables(s_max)
    scale = HEAD_DIM ** -0.5 * LOG2E
    tabs = (jnp.asarray((cos2 * scale).T, F32), jnp.asarray((sin2 * scale).T, F32),
            jnp.asarray(cos2, F32), jnp.asarray(sin2, F32))

    tm = _tile(s_min, 512)
    qt, k, vt, u, gv = _inproj(xa, xb, modp, norm1_g.reshape(1, d), wqt, wkb, wvt, wu, wgv, qg, kg, tabs,
                               grp, tm)

    k_bound2 = HEAD_DIM * jnp.max(jnp.square(k_norm_g)) * BOUND_SLACK
    q_bound = math.sqrt(HEAD_DIM) * scale * BOUND_SLACK * jnp.max(jnp.abs(q_norm_g))
    safe = (q_bound * jnp.sqrt(k_bound2) * BOUND_SLACK < SAFE_SCORE_BOUND).astype(I32).reshape(1)
    attn = _flash(qt, k, vt, jnp.full((8, 128), k_bound2, F32), safe, grp,
                  _tile(s_min, 1024), _tile(s_min, 1024))
    sgun = _sgu(u, gv, sgu_ln_g.reshape(1, A_WIDTH), sgu_ln_b.reshape(1, A_WIDTH), w_s.astype(BF16),
                b_s.T, sgu_out_g.reshape(1, A_WIDTH), _tile(s_min, 512))

    x1, h2, lgt = _outproj(attn, sgun, xa, xb, modp, attn_out_g.reshape(1, ATTN_WIDTH), w_out.astype(BF16),
                           norm2_g.reshape(1, d), w_router.T.astype(BF16), grp, _tile(s_min, 256))

    idx, wts, rank, cnt = _router(lgt, router_bias.reshape(N_EXPERTS, 1), _tile(s_min, 512))

    bm = 512
    counts = cnt[:, 0]
    padded = (counts + bm - 1) // bm * bm
    pad_end = jnp.cumsum(padded).astype(I32)
    pad_start = pad_end - padded
    n_blocks = n * TOP_K // bm + N_EXPERTS
    p_tok = n_blocks * bm
    n_used = (pad_end[-1] // bm).reshape(1).astype(I32)
    block_start = jnp.arange(n_blocks, dtype=I32) * bm
    block_exp = jnp.minimum(jnp.sum((pad_end[None, :] <= block_start[:, None]).astype(I32), axis=1),
                            N_EXPERTS - 1)

    dest = _dest(idx, rank, pad_start, _tile(s_min, 2048))

    td = _tile(s_min, 256)
    xs = _dispatch(counts, pad_start, pad_end, _tok3(dest, td), h2, n, p_tok, td)
    shared = _shared(h2, jnp.concatenate([ws_gate, ws_up], axis=1).astype(BF16), ws_down.astype(BF16),
                     _tile(s_min, 512))
    ys = _experts(block_exp, n_used, xs, w_gate, w_up, w_down, bm)

    tc = _tile(s_min, 128)
    return _combine(_tok3(dest, tc), _tok3(wts, tc), x1, shared, modp, ys, grp, tc)


def kernel(x_prompt, x_sample, c_prompt, c_sample, norm1_g, w_mod, b_mod, w_in, q_norm_g, k_norm_g, sgu_ln_g, sgu_ln_b, w_s, b_s, attn_out_g, sgu_out_g, w_out, norm2_g, w_router, router_bias, w_gate, w_up, w_down, ws_gate, ws_up, ws_down):
    depth = norm1_g.shape[0]
    bp, sp, d = x_prompt.shape
    bs, ss, _ = x_sample.shape
    grp = _Groups([(bp, sp), (bs, ss)])
    xa, xb = x_prompt.reshape(bp * sp, d), x_sample.reshape(bs * ss, d)
    c = jnp.concatenate([c_prompt, c_sample], axis=0)
    for l in range(depth):
        xa, xb = _layer(xa, xb, c, grp, norm1_g[l], w_mod[l], b_mod[l], w_in[l], q_norm_g[l], k_norm_g[l],
                   sgu_ln_g[l], sgu_ln_b[l], w_s[l], b_s[l], attn_out_g[l], sgu_out_g[l], w_out[l],
                   norm2_g[l], w_router[l], router_bias[l], w_gate[l], w_up[l], w_down[l],
                   ws_gate[l], ws_up[l], ws_down[l])
    return (xa.reshape(bp, sp, d), xb.reshape(bs, ss, d))
```

```python
import functools
import math

import numpy as np
import jax
import jax.numpy as jnp
from jax import lax
from jax.experimental import pallas as pl
from jax.experimental.pallas import tpu as pltpu

F32 = jnp.float32
BF16 = jnp.bfloat16
I32 = jnp.int32

HEAD_DIM = 128
N_Q_HEADS = 8
N_KV_HEADS = 2
Q_PER_KV = N_Q_HEADS // N_KV_HEADS
ATTN_WIDTH = N_Q_HEADS * HEAD_DIM
KV_WIDTH = N_KV_HEADS * HEAD_DIM
ROPE_THETA = 10000.0
GRID_W = 64
N_A_HEADS = 8
A_HEAD_DIM = 128
A_WIDTH = N_A_HEADS * A_HEAD_DIM
CHUNK = 128
N_EXPERTS = 64
TOP_K = 8
N_GROUPS = 8
GROUP_SIZE = N_EXPERTS // N_GROUPS
TOPK_GROUPS = 4
ROUTED_SCALE = 2.5
N_MOD = 6
EPS = 1e-6

VMEM_LIMIT_BYTES = 56 * 1024 * 1024
NT_DIMS = (((1,), (1,)), ((), ()))


def _cparams(n_axes):
    return pltpu.CompilerParams(
        dimension_semantics=("arbitrary",) * n_axes, vmem_limit_bytes=VMEM_LIMIT_BYTES)


def _tile(n, pref):
    t = min(n, pref)
    while n % t:
        t //= 2
    return t


def _const_spec(shape):
    nd = len(shape)
    return pl.BlockSpec(shape, lambda *a: (0,) * nd)


def _mod_kernel(c_ref, w_ref, b_ref, o_ref):
    c = c_ref[...]
    a = (c * jax.nn.sigmoid(c)).astype(BF16)
    o_ref[...] = jnp.dot(a, w_ref[...].astype(BF16), preferred_element_type=F32) + b_ref[...]


def _mod(c8, w_mod, b_mod):
    d, n = w_mod.shape
    tn = _tile(n, 1024)
    return pl.pallas_call(
        _mod_kernel,
        grid=(n // tn,),
        in_specs=[_const_spec((8, d)),
                  pl.BlockSpec((d, tn), lambda j: (0, j)),
                  pl.BlockSpec((1, tn), lambda j: (0, j))],
        out_specs=pl.BlockSpec((8, tn), lambda j: (0, j)),
        out_shape=jax.ShapeDtypeStruct((8, n), F32),
        compiler_params=_cparams(1),
        name="mod",
    )(c8, w_mod, b_mod.reshape(1, n))


class _Groups:
    def __init__(self, groups):
        self.groups = groups
        self.n_tokens = sum(b * s for b, s in groups)

    def tile_maps(self, t):
        (b0, s0), (b1, s1) = self.groups
        n0 = b0 * s0 // t
        t0, t1 = s0 // t, s1 // t

        def bidx(i):
            return jnp.where(i < n0, i // t0, b0 + (i - n0) // t1)

        def pidx(i):
            return jnp.where(i < n0, i % t0, (i - n0) % t1)

        return bidx, pidx

    def split_specs(self, t, d):
        (b0, s0), _ = self.groups
        n0 = b0 * s0 // t
        return (pl.BlockSpec((t, d), lambda i, *_: (jnp.minimum(i, n0 - 1), 0)),
                pl.BlockSpec((t, d), lambda i, *_: (jnp.maximum(i - n0, 0), 0)), n0)


def _pick_group(i, n0, a_ref, b_ref):
    return jnp.where(i < n0, a_ref[...], b_ref[...])


LANES = 128
SUBLANES = 8
U32 = jnp.uint32


def _slab_rows(d):
    rows = d // (2 * LANES)
    assert d % (2 * LANES) == 0 and rows % SUBLANES == 0, d
    return rows


def _load_slabs(ref, rows, tok0, n_tok):
    pairs = [_unpack_bf16_pair(ref[pl.ds(tok0 * rows + r, n_tok, stride=rows), :]) for r in range(rows)]
    return jnp.concatenate([p[0] for p in pairs] + [p[1] for p in pairs], axis=1).astype(BF16)


def _pack_bf16_pair(lo, hi):
    def bf16_bits(v):
        return lax.bitcast_convert_type(v.astype(BF16).astype(F32), U32)

    return (bf16_bits(lo) >> 16) | bf16_bits(hi)


def _unpack_bf16_pair(w):
    return (lax.bitcast_convert_type(w << 16, F32),
            lax.bitcast_convert_type(w & jnp.uint32(0xFFFF0000), F32))


def _store_slabs(ref, x, tok0=0):
    n_tok, d = x.shape
    rows = _slab_rows(d)
    for r in range(rows):
        ref[pl.ds(tok0 * rows + r, n_tok, stride=rows), :] = _pack_bf16_pair(
            x[:, r * LANES:(r + 1) * LANES], x[:, (rows + r) * LANES:(rows + r + 1) * LANES])


def _rope_tables(seq_len):
    pos = np.arange(seq_len)
    row = (pos // GRID_W).astype(np.float64)
    col = (pos % GRID_W).astype(np.float64)
    n_pairs = HEAD_DIM // 4
    freqs = ROPE_THETA ** (-np.arange(n_pairs, dtype=np.float64) / n_pairs)
    ang = np.concatenate([row[:, None] * freqs, col[:, None] * freqs], axis=-1)
    c, s = np.cos(ang), np.sin(ang)
    cos2 = np.concatenate([c, c], axis=-1)
    sin2 = np.concatenate([-s, s], axis=-1)
    return cos2, sin2


def _inproj_kernel(xa_ref, xb_ref, mod_ref, g1_ref, wqt_ref, wk_ref, wvt_ref, wu_ref, wgv_ref,
                   qg_ref, kg_ref, cost_ref, sint_ref, cos_ref, sin_ref,
                   qt_ref, k_ref, vt_ref, u_ref, gv_ref, *, n0):
    x = _pick_group(pl.program_id(0), n0, xa_ref, xb_ref)
    shift1 = mod_ref[0:1, :]
    scale1 = mod_ref[1:2, :]
    ms = jnp.mean(x * x, axis=-1, keepdims=True)
    h = (x * lax.rsqrt(ms + EPS) * g1_ref[...]) * (1.0 + scale1) + shift1
    hb = h.astype(BF16)
    half = HEAD_DIM // 2

    qt = lax.dot_general(wqt_ref[...], hb, NT_DIMS, preferred_element_type=F32)
    cost = cost_ref[...]
    sint = sint_ref[...]
    qg = qg_ref[...]
    for hh in range(N_Q_HEADS):
        qh = qt[hh * HEAD_DIM:(hh + 1) * HEAD_DIM, :]
        msq = jnp.mean(qh * qh, axis=0, keepdims=True)
        qn = qh * lax.rsqrt(msq + EPS) * qg
        rot = jnp.concatenate([qn[half:, :], qn[:half, :]], axis=0)
        qt_ref[hh * HEAD_DIM:(hh + 1) * HEAD_DIM, :] = (qn * cost + rot * sint).astype(BF16)

    vt_ref[...] = lax.dot_general(wvt_ref[...], hb, NT_DIMS, preferred_element_type=F32).astype(BF16)

    kk = jnp.dot(hb, wk_ref[...], preferred_element_type=F32)
    cos = cos_ref[...]
    sin = sin_ref[...]
    kg = kg_ref[...]
    for j in range(N_KV_HEADS):
        kh = kk[:, j * HEAD_DIM:(j + 1) * HEAD_DIM]
        msk = jnp.mean(kh * kh, axis=-1, keepdims=True)
        kn = kh * lax.rsqrt(msk + EPS) * kg
        rot = jnp.concatenate([kn[:, half:], kn[:, :half]], axis=1)
        k_ref[:, j * HEAD_DIM:(j + 1) * HEAD_DIM] = (kn * cos + rot * sin).astype(BF16)

    u_ref[...] = jnp.dot(hb, wu_ref[...], preferred_element_type=F32).astype(BF16)
    gv_ref[...] = jnp.dot(hb, wgv_ref[...], preferred_element_type=F32).astype(BF16)


def _inproj(xa, xb, modp, g1, wqt, wk, wvt, wu, wgv, qg, kg, tabs, grp, tm):
    n, d = grp.n_tokens, xa.shape[1]
    bidx, pidx = grp.tile_maps(tm)
    cost, sint, cos, sin = tabs
    tok = lambda w: pl.BlockSpec((tm, w), lambda i: (i, 0))
    tokt = lambda w: pl.BlockSpec((w, tm), lambda i: (0, i))
    xa_spec, xb_spec, n0 = grp.split_specs(tm, d)
    return pl.pallas_call(
        functools.partial(_inproj_kernel, n0=n0),
        grid=(n // tm,),
        in_specs=[xa_spec, xb_spec,
                  pl.BlockSpec((None, 8, d), lambda i: (bidx(i), 0, 0)),
                  _const_spec((1, d)),
                  _const_spec(wqt.shape), _const_spec(wk.shape), _const_spec(wvt.shape),
                  _const_spec(wu.shape), _const_spec(wgv.shape),
                  _const_spec((HEAD_DIM, 1)), _const_spec((1, HEAD_DIM)),
                  pl.BlockSpec((HEAD_DIM, tm), lambda i: (0, pidx(i))),
                  pl.BlockSpec((HEAD_DIM, tm), lambda i: (0, pidx(i))),
                  pl.BlockSpec((tm, HEAD_DIM), lambda i: (pidx(i), 0)),
                  pl.BlockSpec((tm, HEAD_DIM), lambda i: (pidx(i), 0))],
        out_specs=[tokt(ATTN_WIDTH), tok(KV_WIDTH), tokt(KV_WIDTH), tok(A_WIDTH), tok(A_WIDTH)],
        out_shape=[jax.ShapeDtypeStruct((ATTN_WIDTH, n), BF16),
                   jax.ShapeDtypeStruct((n, KV_WIDTH), BF16),
                   jax.ShapeDtypeStruct((KV_WIDTH, n), BF16),
                   jax.ShapeDtypeStruct((n, A_WIDTH), BF16),
                   jax.ShapeDtypeStruct((n, A_WIDTH), BF16)],
        compiler_params=_cparams(1),
        name="inproj",
    )(xa, xb, modp, g1, wqt, wk, wvt, wu, wgv, qg, kg, cost, sint, cos, sin)


LOG2E = 1.4426950408889634
SAFE_SCORE_BOUND = 60.0
BOUND_SLACK = 1.01


def _flash_kernel(qt_s, kt_s, vp_s, fl_s, safe_s, q_ref, k_ref, v_ref, vp_ref, km_ref, o_ref,
                  qs_sc, m_sc, l_sc, acc_sc, p_sc, *, bq, hb):
    s_id = pl.program_id(1)
    flag = fl_s[s_id]
    safe = safe_s[0] != 0
    last = (flag & 2) != 0

    @pl.when((flag & 1) != 0)
    def _init():
        l_sc[...] = jnp.zeros(l_sc.shape, F32)
        acc_sc[...] = jnp.zeros(acc_sc.shape, F32)
        p_sc[...] = jnp.zeros(p_sc.shape, BF16)
        for hh in range(Q_PER_KV):
            qs_sc[:, hh * bq:(hh + 1) * bq] = q_ref[hh * HEAD_DIM:(hh + 1) * HEAD_DIM, :]
        qf = qs_sc[...].astype(F32)
        qn2 = jnp.sum(qf * qf, axis=0, keepdims=True)
        bound = jnp.sqrt(qn2 * km_ref[0:1, 0:1]) * BOUND_SLACK
        m_sc[...] = jnp.where(safe, bound, -jnp.inf)

    @pl.when(safe)
    def _fast():
        q = qs_sc[...]
        m = m_sc[...]
        sa = jnp.dot(k_ref[:hb, :], q, preferred_element_type=F32)
        acc = acc_sc[...] + jnp.dot(vp_ref[...], p_sc[...], preferred_element_type=F32)
        pa = jnp.exp2(sa - m)
        la = jnp.sum(pa, axis=0, keepdims=True)
        sb = jnp.dot(k_ref[hb:, :], q, preferred_element_type=F32)
        acc_sc[...] = acc + jnp.dot(v_ref[:, :hb], pa.astype(BF16), preferred_element_type=F32)
        pb = jnp.exp2(sb - m)
        l_sc[...] += la + jnp.sum(pb, axis=0, keepdims=True)
        p_sc[...] = pb.astype(BF16)

    @pl.when(jnp.logical_and(safe, last))
    def _flush():
        acc_sc[...] += jnp.dot(v_ref[:, hb:], p_sc[...], preferred_element_type=F32)

    @pl.when(jnp.logical_not(safe))
    def _online():
        s = jnp.dot(k_ref[...], qs_sc[...], preferred_element_type=F32)
        m_prev = m_sc[...]
        m_new = jnp.maximum(m_prev, jnp.max(s, axis=0, keepdims=True))
        alpha = jnp.exp2(m_prev - m_new)
        p = jnp.exp2(s - m_new)
        l_sc[...] = alpha * l_sc[...] + jnp.sum(p, axis=0, keepdims=True)
        acc_sc[...] = alpha * acc_sc[...] + jnp.dot(v_ref[...], p.astype(BF16), preferred_element_type=F32)
        m_sc[...] = m_new

    @pl.when((flag & 2) != 0)
    def _fin():
        o = acc_sc[...] / l_sc[...]
        for hh in range(Q_PER_KV):
            o_ref[:, hh * HEAD_DIM:(hh + 1) * HEAD_DIM] = o[:, hh * bq:(hh + 1) * bq].T.astype(BF16)


def _flash_schedule(grp, bq, bkv):
    qt, kt, vp, fl = [], [], [], []
    off = 0
    for b, s in grp.groups:
        for bi in range(b):
            base = off + bi * s
            nk = s // bkv
            for qi in range(s // bq):
                for ki in range(nk):
                    blk = base // bkv + ki
                    qt.append(base // bq + qi)
                    kt.append(blk)
                    vp.append(2 * (blk - 1 if ki else blk) + 1)
                    fl.append((1 if ki == 0 else 0) | (2 if ki == nk - 1 else 0))
        off += b * s
    return tuple(np.asarray(a, np.int32) for a in (qt, kt, vp, fl))


def _flash(qt, k, vt, kmax2, safe, grp, bq, bkv):
    n = k.shape[0]
    hb = bkv // 2
    qt_a, kt_a, vp_a, fl_a = _flash_schedule(grp, bq, bkv)
    n_steps = len(qt_a)
    w = Q_PER_KV * HEAD_DIM
    gs = pltpu.PrefetchScalarGridSpec(
        num_scalar_prefetch=5,
        grid=(N_KV_HEADS, n_steps),
        in_specs=[pl.BlockSpec((w, bq), lambda h, s, qa, ka, va, fa, sa: (h, qa[s])),
                  pl.BlockSpec((bkv, HEAD_DIM), lambda h, s, qa, ka, va, fa, sa: (ka[s], h)),
                  pl.BlockSpec((HEAD_DIM, bkv), lambda h, s, qa, ka, va, fa, sa: (h, ka[s])),
                  pl.BlockSpec((HEAD_DIM, hb), lambda h, s, qa, ka, va, fa, sa: (h, va[s])),
                  pl.BlockSpec((8, 128), lambda h, s, qa, ka, va, fa, sa: (0, 0))],
        out_specs=pl.BlockSpec((bq, w), lambda h, s, qa, ka, va, fa, sa: (qa[s], h)),
        scratch_shapes=[pltpu.VMEM((HEAD_DIM, Q_PER_KV * bq), BF16),
                        pltpu.VMEM((1, Q_PER_KV * bq), F32),
                        pltpu.VMEM((1, Q_PER_KV * bq), F32),
                        pltpu.VMEM((HEAD_DIM, Q_PER_KV * bq), F32),
                        pltpu.VMEM((hb, Q_PER_KV * bq), BF16)])
    return pl.pallas_call(
        functools.partial(_flash_kernel, bq=bq, hb=hb),
        grid_spec=gs,
        out_shape=jax.ShapeDtypeStruct((n, ATTN_WIDTH), BF16),
        compiler_params=_cparams(2),
        name="flash",
    )(jnp.asarray(qt_a), jnp.asarray(kt_a), jnp.asarray(vp_a), jnp.asarray(fl_a), safe, qt, k, vt, vt, kmax2)


def _sgu_kernel(u_ref, gv_ref, lng_ref, lnb_ref, ws_ref, bst_ref, og_ref, o_ref, *, n_chunks):
    v = gv_ref[...].astype(F32)
    mean = jnp.mean(v, axis=-1, keepdims=True)
    vc = v - mean
    var = jnp.mean(vc * vc, axis=-1, keepdims=True)
    vn = (vc * lax.rsqrt(var + EPS) * lng_ref[...] + lnb_ref[...]).astype(BF16)
    u = u_ref[...].astype(F32)
    heads = []
    for hh in range(N_A_HEADS):
        cols = slice(hh * A_HEAD_DIM, (hh + 1) * A_HEAD_DIM)
        rhs = jnp.concatenate([vn[c * CHUNK:(c + 1) * CHUNK, cols] for c in range(n_chunks)], axis=1)
        mixed = jnp.dot(ws_ref[hh], rhs, preferred_element_type=F32) + bst_ref[:, hh:hh + 1]
        heads.append(jnp.concatenate(
            [mixed[:, c * A_HEAD_DIM:(c + 1) * A_HEAD_DIM] for c in range(n_chunks)], axis=0))
    sg = u * jnp.concatenate(heads, axis=1)
    ms = jnp.mean(sg * sg, axis=-1, keepdims=True)
    o_ref[...] = (sg * lax.rsqrt(ms + EPS) * og_ref[...]).astype(BF16)


def _sgu(u, gv, lng, lnb, ws, bst, og, tm):
    n = u.shape[0]
    tok = pl.BlockSpec((tm, A_WIDTH), lambda i: (i, 0))
    return pl.pallas_call(
        functools.partial(_sgu_kernel, n_chunks=tm // CHUNK),
        grid=(n // tm,),
        in_specs=[tok, tok, _const_spec((1, A_WIDTH)), _const_spec((1, A_WIDTH)),
                  _const_spec(ws.shape), _const_spec(bst.shape), _const_spec((1, A_WIDTH))],
        out_specs=tok,
        out_shape=jax.ShapeDtypeStruct((n, A_WIDTH), BF16),
        compiler_params=_cparams(1),
        name="sgu",
    )(u, gv, lng, lnb, ws, bst, og)


def _outproj_kernel(attn_ref, sgu_ref, xa_ref, xb_ref, mod_ref, ag_ref, wo_ref, g2_ref, wrt_ref,
                    x1_ref, h2_ref, lgt_ref, *, n0):
    a = attn_ref[...].astype(F32)
    ms = jnp.mean(a * a, axis=-1, keepdims=True)
    an = (a * lax.rsqrt(ms + EPS) * ag_ref[...]).astype(BF16)
    o = jnp.dot(an, wo_ref[:ATTN_WIDTH, :], preferred_element_type=F32)
    o = o + jnp.dot(sgu_ref[...], wo_ref[ATTN_WIDTH:, :], preferred_element_type=F32)
    gate1 = mod_ref[2:3, :]
    shift2 = mod_ref[3:4, :]
    scale2 = mod_ref[4:5, :]
    x1 = _pick_group(pl.program_id(0), n0, xa_ref, xb_ref) + gate1 * o
    x1_ref[...] = x1
    ms2 = jnp.mean(x1 * x1, axis=-1, keepdims=True)
    h2 = (x1 * lax.rsqrt(ms2 + EPS) * g2_ref[...]) * (1.0 + scale2) + shift2
    _store_slabs(h2_ref, h2)
    lgt_ref[...] = lax.dot_general(wrt_ref[...], h2.astype(BF16), NT_DIMS, preferred_element_type=F32)


def _outproj(attn, sgun, xa, xb, modp, ag, wo, g2, wrt, grp, tm):
    n, d = grp.n_tokens, xa.shape[1]
    bidx, _ = grp.tile_maps(tm)
    tok = lambda w: pl.BlockSpec((tm, w), lambda i: (i, 0))
    xa_spec, xb_spec, n0 = grp.split_specs(tm, d)
    return pl.pallas_call(
        functools.partial(_outproj_kernel, n0=n0),
        grid=(n // tm,),
        in_specs=[tok(ATTN_WIDTH), tok(A_WIDTH), xa_spec, xb_spec,
                  pl.BlockSpec((None, 8, d), lambda i: (bidx(i), 0, 0)),
                  _const_spec((1, ATTN_WIDTH)), _const_spec(wo.shape), _const_spec((1, d)),
                  _const_spec(wrt.shape)],
        out_specs=[tok(d), pl.BlockSpec((tm * _slab_rows(d), LANES), lambda i: (i, 0)),
                   pl.BlockSpec((N_EXPERTS, tm), lambda i: (0, i))],
        out_shape=[jax.ShapeDtypeStruct((n, d), F32),
                   jax.ShapeDtypeStruct((n * _slab_rows(d), LANES), U32),
                   jax.ShapeDtypeStruct((N_EXPERTS, n), F32)],
        compiler_params=_cparams(1),
        name="outproj",
    )(attn, sgun, xa, xb, modp, ag, wo, g2, wrt)


def _first_index(hit, iota, sentinel):
    return jnp.min(jnp.where(hit, iota, sentinel), axis=0, keepdims=True)


def _router_kernel(lg_ref, bias_ref, idx_ref, w_ref, rank_ref, cnt_ref, run_sc):
    i = pl.program_id(0)
    tt = lg_ref.shape[1]

    @pl.when(i == 0)
    def _init():
        run_sc[...] = jnp.zeros(run_sc.shape, F32)

    scores = jax.nn.sigmoid(lg_ref[...])
    biased = scores + bias_ref[...]
    neg = -jnp.inf
    iota_g = lax.broadcasted_iota(I32, (GROUP_SIZE, tt), 0).astype(F32)

    gs_rows = []
    for g in range(N_GROUPS):
        xg = biased[g * GROUP_SIZE:(g + 1) * GROUP_SIZE, :]
        m1 = jnp.max(xg, axis=0, keepdims=True)
        f1 = _first_index(xg == m1, iota_g, float(GROUP_SIZE))
        m2 = jnp.max(jnp.where(iota_g == f1, neg, xg), axis=0, keepdims=True)
        gs_rows.append(m1 + m2)
    gs = jnp.concatenate(gs_rows, axis=0)

    iota_n = lax.broadcasted_iota(I32, (N_GROUPS, tt), 0).astype(F32)
    gsel = jnp.zeros((N_GROUPS, tt), F32)
    cur = gs
    for _ in range(TOPK_GROUPS):
        m = jnp.max(cur, axis=0, keepdims=True)
        f = _first_index(cur == m, iota_n, float(N_GROUPS))
        hit = iota_n == f
        gsel = jnp.where(hit, 1.0, gsel)
        cur = jnp.where(hit, neg, cur)

    masked = jnp.concatenate(
        [jnp.where(gsel[g:g + 1, :] > 0.5, biased[g * GROUP_SIZE:(g + 1) * GROUP_SIZE, :], neg)
         for g in range(N_GROUPS)], axis=0)

    iota_e = lax.broadcasted_iota(I32, (N_EXPERTS, tt), 0).astype(F32)
    sel = jnp.zeros((N_EXPERTS, tt), F32)
    idx_rows, w_rows = [], []
    for _ in range(TOP_K):
        m = jnp.max(masked, axis=0, keepdims=True)
        f = _first_index(masked == m, iota_e, float(N_EXPERTS))
        hit = iota_e == f
        idx_rows.append(f)
        w_rows.append(jnp.sum(jnp.where(hit, scores, 0.0), axis=0, keepdims=True))
        sel = jnp.where(hit, 1.0, sel)
        masked = jnp.where(hit, neg, masked)
    idx = jnp.concatenate(idx_rows, axis=0)
    wk = jnp.concatenate(w_rows, axis=0)
    wk = wk / (jnp.sum(wk, axis=0, keepdims=True) + 1e-20) * ROUTED_SCALE

    r = lax.broadcasted_iota(I32, (tt, tt), 0)
    c = lax.broadcasted_iota(I32, (tt, tt), 1)
    upper = jnp.where(r < c, 1.0, 0.0).astype(BF16)
    prefix = jnp.dot(sel.astype(BF16), upper, preferred_element_type=F32) + run_sc[...]
    rank_rows = [jnp.sum(jnp.where(iota_e == idx[k:k + 1, :], prefix, 0.0), axis=0, keepdims=True)
                 for k in range(TOP_K)]
    run_sc[...] = run_sc[...] + jnp.sum(sel, axis=1, keepdims=True)

    idx_ref[...] = idx.astype(I32)
    w_ref[...] = wk
    rank_ref[...] = jnp.concatenate(rank_rows, axis=0).astype(I32)
    cnt_ref[...] = jnp.broadcast_to(run_sc[...], cnt_ref.shape).astype(I32)


def _router(lgt, bias, tt):
    n = lgt.shape[1]
    tokk = pl.BlockSpec((TOP_K, tt), lambda i: (0, i))
    return pl.pallas_call(
        _router_kernel,
        grid=(n // tt,),
        in_specs=[pl.BlockSpec((N_EXPERTS, tt), lambda i: (0, i)), _const_spec((N_EXPERTS, 1))],
        out_specs=[tokk, tokk, tokk, _const_spec((N_EXPERTS, 128))],
        out_shape=[jax.ShapeDtypeStruct((TOP_K, n), I32), jax.ShapeDtypeStruct((TOP_K, n), F32),
                   jax.ShapeDtypeStruct((TOP_K, n), I32), jax.ShapeDtypeStruct((N_EXPERTS, 128), I32)],
        scratch_shapes=[pltpu.VMEM((N_EXPERTS, 1), F32)],
        compiler_params=_cparams(1),
        name="router",
    )(lgt, bias)


def _dest_kernel(idx_ref, rank_ref, start_ref, o_ref):
    tt = idx_ref.shape[1]
    iota_e = lax.broadcasted_iota(I32, (N_EXPERTS, tt), 0)
    start = start_ref[...]
    rows = []
    for k in range(TOP_K):
        hit = iota_e == idx_ref[k:k + 1, :]
        rows.append(jnp.sum(jnp.where(hit, start, 0), axis=0, keepdims=True))
    o_ref[...] = jnp.concatenate(rows, axis=0) + rank_ref[...]


def _dest(idx, rank, pad_start, tt):
    n = idx.shape[1]
    tokk = pl.BlockSpec((TOP_K, tt), lambda i: (0, i))
    return pl.pallas_call(
        _dest_kernel,
        grid=(n // tt,),
        in_specs=[tokk, tokk, _const_spec((N_EXPERTS, 1))],
        out_specs=tokk,
        out_shape=jax.ShapeDtypeStruct((TOP_K, n), I32),
        compiler_params=_cparams(1),
        name="dest",
    )(idx, rank, pad_start.reshape(N_EXPERTS, 1))


def _slab(ref, tok, rows):
    return ref.at[pl.ds(pl.multiple_of(tok * rows, SUBLANES), rows)]


def _dispatch_kernel(cnt_s, start_s, pend_s, dest_ref, h_ref, xs_ref, zero_sc, sem, *, td):
    i = pl.program_id(0)
    rows = zero_sc.shape[0]

    def slab_copy(t, slot):
        return pltpu.make_async_copy(_slab(h_ref, t, rows), _slab(xs_ref, slot, rows), sem)

    def issue(t, c):
        for k in range(TOP_K):
            slab_copy(t, dest_ref[k, t]).start(priority=k % 2)
        return c

    lax.fori_loop(0, td, issue, 0)

    n_rows = td * TOP_K * rows
    pltpu.make_async_copy(xs_ref.at[pl.ds(0, n_rows)], xs_ref.at[pl.ds(0, n_rows)], sem).wait()

    @pl.when(i == pl.num_programs(0) - 1)
    def _pad():
        zero_sc[...] = jnp.zeros(zero_sc.shape, U32)

        def pad_copy(p):
            return pltpu.make_async_copy(zero_sc, _slab(xs_ref, p, rows), sem)

        def per_expert(e, c):
            lo = start_s[e] + cnt_s[e]
            hi = pend_s[e]

            def go(p, c2):
                pad_copy(p).start()
                return c2

            lax.fori_loop(lo, hi, go, 0)

            def done(p, c2):
                pad_copy(p).wait()
                return c2

            lax.fori_loop(lo, hi, done, 0)
            return c

        lax.fori_loop(0, N_EXPERTS, per_expert, 0)


def _dispatch(counts, pad_start, pad_end, dest3, h2s, n, p_tok, td):
    rows = h2s.shape[0] // n
    gs = pltpu.PrefetchScalarGridSpec(
        num_scalar_prefetch=3,
        grid=(n // td,),
        in_specs=[pl.BlockSpec((None, TOP_K, td), lambda i, *_: (i, 0, 0), memory_space=pltpu.SMEM),
                  pl.BlockSpec((td * rows, LANES), lambda i, *_: (i, 0))],
        out_specs=pl.BlockSpec(memory_space=pl.ANY),
        scratch_shapes=[pltpu.VMEM((rows, LANES), U32), pltpu.SemaphoreType.DMA(())])
    return pl.pallas_call(
        functools.partial(_dispatch_kernel, td=td),
        grid_spec=gs,
        out_shape=jax.ShapeDtypeStruct((p_tok * rows, LANES), U32),
        compiler_params=_cparams(1),
        name="dispatch",
    )(counts, pad_start, pad_end, dest3, h2s)


def _swiglu_slabs(x_ref, wgu_ref, wd_ref, o_ref, *, packed_out):
    de, d = wd_ref.shape
    rows = _slab_rows(d)
    half = x_ref.shape[0] // rows // 2
    ups = [jnp.dot(_load_slabs(x_ref, rows, t0, half), wgu_ref[...],
                   preferred_element_type=F32) for t0 in (0, half)]
    for t0, r in zip((0, half), ups):
        a = r[:, :de]
        act = (a * jax.nn.sigmoid(a) * r[:, de:]).astype(BF16)
        out = jnp.dot(act, wd_ref[...], preferred_element_type=F32)
        if packed_out:
            _store_slabs(o_ref, out, t0)
        else:
            o_ref[t0:t0 + half, :] = out


def _shared_kernel(h_ref, wgu_ref, wd_ref, o_ref):
    _swiglu_slabs(h_ref, wgu_ref, wd_ref, o_ref, packed_out=False)


def _shared(h2s, wgu, wd, tm):
    d = wd.shape[1]
    rows = _slab_rows(d)
    n = h2s.shape[0] // rows
    return pl.pallas_call(
        _shared_kernel,
        grid=(n // tm,),
        in_specs=[pl.BlockSpec((tm * rows, LANES), lambda i: (i, 0)),
                  _const_spec(wgu.shape), _const_spec(wd.shape)],
        out_specs=pl.BlockSpec((tm, d), lambda i: (i, 0)),
        out_shape=jax.ShapeDtypeStruct((n, d), F32),
        compiler_params=_cparams(1),
        name="shared",
    )(h2s, wgu, wd)


def _experts_kernel(be_s, nu_s, x_ref, wg_ref, wu_ref, wd_ref, o_ref, wgu_sc, wd_sc):
    i = pl.program_id(0)
    used = i < nu_s[0]
    prev = be_s[jnp.maximum(i - 1, 0)]
    fresh = jnp.logical_or(i == 0, be_s[i] != prev)
    de = wd_sc.shape[0]

    @pl.when(jnp.logical_and(used, fresh))
    def _cast():
        wgu_sc[:, :de] = wg_ref[...].astype(BF16)
        wgu_sc[:, de:] = wu_ref[...].astype(BF16)
        wd_sc[...] = wd_ref[...].astype(BF16)

    @pl.when(used)
    def _run():
        _swiglu_slabs(x_ref, wgu_sc, wd_sc, o_ref, packed_out=True)


def _experts(block_exp, n_used, xs, w_gate, w_up, w_down, bm):
    d, de = w_gate.shape[1], w_gate.shape[2]
    rows = _slab_rows(d)
    p_tok = xs.shape[0] // rows
    n_blocks = p_tok // bm
    row = lambda i, be, nu: (jnp.minimum(i, nu[0] - 1), 0)
    wsel = lambda i, be, nu: (be[jnp.minimum(i, nu[0] - 1)], 0, 0)
    gs = pltpu.PrefetchScalarGridSpec(
        num_scalar_prefetch=2,
        grid=(n_blocks,),
        in_specs=[pl.BlockSpec((bm * rows, LANES), row),
                  pl.BlockSpec((None, d, de), wsel),
                  pl.BlockSpec((None, d, de), wsel),
                  pl.BlockSpec((None, de, d), wsel)],
        out_specs=pl.BlockSpec((bm * rows, LANES), row),
        scratch_shapes=[pltpu.VMEM((d, 2 * de), BF16), pltpu.VMEM((de, d), BF16)])
    return pl.pallas_call(
        _experts_kernel,
        grid_spec=gs,
        out_shape=jax.ShapeDtypeStruct(xs.shape, U32),
        compiler_params=_cparams(1),
        name="experts",
    )(block_exp, n_used, xs, w_gate, w_up, w_down)


def _combine_kernel(dest_ref, nxt_ref, w_ref, x1_ref, sh_ref, mod_ref, ys_ref, oa_ref, ob_ref,
                    buf, lo_sc, hi_sc, sems, *, tc, n0):
    i = pl.program_id(0)
    rows = lo_sc.shape[0] // tc
    cur = i % 2

    def gather(d_ref, b):
        def issue(t, c):
            for k in range(TOP_K):
                pltpu.make_async_copy(_slab(ys_ref, d_ref[k, t], rows), _slab(buf.at[b, k], t, rows),
                                      sems.at[b]).start(priority=k % 2)
            return c

        lax.fori_loop(0, tc, issue, 0)

    @pl.when(i == 0)
    def _():
        gather(dest_ref, 0)

    @pl.when(i + 1 < pl.num_programs(0))
    def _():
        gather(nxt_ref, 1 - cur)

    pltpu.make_async_copy(buf.at[cur], buf.at[cur], sems.at[cur]).wait()

    def token_sum(t, c):
        r0 = pl.multiple_of(t * rows, SUBLANES)
        lo = hi = None
        for k in range(TOP_K):
            lo_k, hi_k = _unpack_bf16_pair(buf[cur, k, pl.ds(r0, rows), :])
            wk = w_ref[k, t]
            lo = wk * lo_k if lo is None else lo + wk * lo_k
            hi = wk * hi_k if hi is None else hi + wk * hi_k
        lo_sc[pl.ds(r0, rows), :] = lo
        hi_sc[pl.ds(r0, rows), :] = hi
        return c

    lax.fori_loop(0, tc, token_sum, 0, unroll=4)

    routed = jnp.concatenate([lo_sc[pl.ds(r, tc, stride=rows), :] for r in range(rows)]
                             + [hi_sc[pl.ds(r, tc, stride=rows), :] for r in range(rows)], axis=1)
    y = x1_ref[...] + mod_ref[5:6, :] * (sh_ref[...] + routed)

    @pl.when(pl.program_id(0) < n0)
    def _():
        oa_ref[...] = y

    @pl.when(pl.program_id(0) >= n0)
    def _():
        ob_ref[...] = y


def _combine(dest3, w3, x1, shared, modp, ys, grp, tc):
    n, d = x1.shape
    rows = _slab_rows(d)
    bidx, _ = grp.tile_maps(tc)
    tok = pl.BlockSpec((tc, d), lambda i: (i, 0))
    oa_spec, ob_spec, n0 = grp.split_specs(tc, d)
    (b0, s0), (b1, s1) = grp.groups
    return pl.pallas_call(
        functools.partial(_combine_kernel, tc=tc, n0=n0),
        grid=(n // tc,),
        in_specs=[pl.BlockSpec((None, TOP_K, tc), lambda i: (i, 0, 0), memory_space=pltpu.SMEM),
                  pl.BlockSpec((None, TOP_K, tc), lambda i: (jnp.minimum(i + 1, n // tc - 1), 0, 0),
                               memory_space=pltpu.SMEM),
                  pl.BlockSpec((None, TOP_K, tc), lambda i: (i, 0, 0), memory_space=pltpu.SMEM),
                  tok, tok,
                  pl.BlockSpec((None, 8, d), lambda i: (bidx(i), 0, 0)),
                  pl.BlockSpec(memory_space=pl.ANY)],
        out_specs=[oa_spec, ob_spec],
        out_shape=[jax.ShapeDtypeStruct((b0 * s0, d), F32), jax.ShapeDtypeStruct((b1 * s1, d), F32)],
        scratch_shapes=[pltpu.VMEM((2, TOP_K, tc * rows, LANES), U32),
                        pltpu.VMEM((tc * rows, LANES), F32), pltpu.VMEM((tc * rows, LANES), F32),
                        pltpu.SemaphoreType.DMA((2,))],
        compiler_params=_cparams(1),
        name="combine",
    )(dest3, dest3, w3, x1, shared, modp, ys)


def _tok3(a, t):
    k, n = a.shape
    return a.reshape(k, n // t, t).transpose(1, 0, 2)


def _layer(xa, xb, c, grp, norm1_g, w_mod, b_mod, w_in, q_norm_g, k_norm_g, sgu_ln_g, sgu_ln_b, w_s, b_s,
           attn_out_g, sgu_out_g, w_out, norm2_g, w_router, router_bias, w_gate, w_up, w_down,
           ws_gate, ws_up, ws_down):
    n, d = grp.n_tokens, xa.shape[1]
    nb = c.shape[0]
    s_min = min(s for _, s in grp.groups)
    s_max = max(s for _, s in grp.groups)

    c8 = jnp.pad(c, ((0, 8 - nb), (0, 0)))
    mod = _mod(c8, w_mod, b_mod)[:nb].reshape(nb, N_MOD, d)
    modp = jnp.pad(mod, ((0, 0), (0, 8 - N_MOD), (0, 0)))

    perm = np.concatenate([np.arange(0, HEAD_DIM, 2), np.arange(1, HEAD_DIM, 2)])
    c0, c1, c2, c3 = ATTN_WIDTH, ATTN_WIDTH + KV_WIDTH, ATTN_WIDTH + 2 * KV_WIDTH, ATTN_WIDTH + 2 * KV_WIDTH + A_WIDTH
    wq = w_in[:, :c0].reshape(d, N_Q_HEADS, HEAD_DIM)[:, :, perm].reshape(d, ATTN_WIDTH)
    wk = w_in[:, c0:c1].reshape(d, N_KV_HEADS, HEAD_DIM)[:, :, perm].reshape(d, KV_WIDTH)
    wqt = wq.T.astype(BF16)
    wkb = wk.astype(BF16)
    wvt = w_in[:, c1:c2].T.astype(BF16)
    wu = w_in[:, c2:c3].astype(BF16)
    wgv = w_in[:, c3:].astype(BF16)
    qg = q_norm_g[perm].reshape(HEAD_DIM, 1)
    kg = k_norm_g[perm].reshape(1, HEAD_DIM)

    cos2, sin2 = _rope_tables(s_max)
    scale = HEAD_DIM ** -0.5 * LOG2E
    tabs = (jnp.asarray((cos2 * scale).T, F32), jnp.asarray((sin2 * scale).T, F32),
            jnp.asarray(cos2, F32), jnp.asarray(sin2, F32))

    tm = _tile(s_min, 512)
    qt, k, vt, u, gv = _inproj(xa, xb, modp, norm1_g.reshape(1, d), wqt, wkb, wvt, wu, wgv, qg, kg, tabs,
                               grp, tm)

    k_bound2 = HEAD_DIM * jnp.max(jnp.square(k_norm_g)) * BOUND_SLACK
    q_bound = math.sqrt(HEAD_DIM) * scale * BOUND_SLACK * jnp.max(jnp.abs(q_norm_g))
    safe = (q_bound * jnp.sqrt(k_bound2) * BOUND_SLACK < SAFE_SCORE_BOUND).astype(I32).reshape(1)
    attn = _flash(qt, k, vt, jnp.full((8, 128), k_bound2, F32), safe, grp,
                  _tile(s_min, 1024), _tile(s_min, 1024))
    sgun = _sgu(u, gv, sgu_ln_g.reshape(1, A_WIDTH), sgu_ln_b.reshape(1, A_WIDTH), w_s.astype(BF16),
                b_s.T, sgu_out_g.reshape(1, A_WIDTH), _tile(s_min, 512))

    x1, h2, lgt = _outproj(attn, sgun, xa, xb, modp, attn_out_g.reshape(1, ATTN_WIDTH), w_out.astype(BF16),
                           norm2_g.reshape(1, d), w_router.T.astype(BF16), grp, _tile(s_min, 256))

    idx, wts, rank, cnt = _router(lgt, router_bias.reshape(N_EXPERTS, 1), _tile(s_min, 512))

    bm = 512
    counts = cnt[:, 0]
    padded = (counts + bm - 1) // bm * bm
    pad_end = jnp.cumsum(padded).astype(I32)
    pad_start = pad_end - padded
    n_blocks = n * TOP_K // bm + N_EXPERTS
    p_tok = n_blocks * bm
    n_used = (pad_end[-1] // bm).reshape(1).astype(I32)
    block_start = jnp.arange(n_blocks, dtype=I32) * bm
    block_exp = jnp.minimum(jnp.sum((pad_end[None, :] <= block_start[:, None]).astype(I32), axis=1),
                            N_EXPERTS - 1)

    dest = _dest(idx, rank, pad_start, _tile(s_min, 2048))

    td = _tile(s_min, 256)
    xs = _dispatch(counts, pad_start, pad_end, _tok3(dest, td), h2, n, p_tok, td)
    shared = _shared(h2, jnp.concatenate([ws_gate, ws_up], axis=1).astype(BF16), ws_down.astype(BF16),
                     _tile(s_min, 512))
    ys = _experts(block_exp, n_used, xs, w_gate, w_up, w_down, bm)

    tc = _tile(s_min, 128)
    return _combine(_tok3(dest, tc), _tok3(wts, tc), x1, shared, modp, ys, grp, tc)


def kernel(x_prompt, x_sample, c_prompt, c_sample, norm1_g, w_mod, b_mod, w_in, q_norm_g, k_norm_g, sgu_ln_g, sgu_ln_b, w_s, b_s, attn_out_g, sgu_out_g, w_out, norm2_g, w_router, router_bias, w_gate, w_up, w_down, ws_gate, ws_up, ws_down):
    depth = norm1_g.shape[0]
    bp, sp, d = x_prompt.shape
    bs, ss, _ = x_sample.shape
    grp = _Groups([(bp, sp), (bs, ss)])
    xa, xb = x_prompt.reshape(bp * sp, d), x_sample.reshape(bs * ss, d)
    c = jnp.concatenate([c_prompt, c_sample], axis=0)
    for l in range(depth):
        xa, xb = _layer(xa, xb, c, grp, norm1_g[l], w_mod[l], b_mod[l], w_in[l], q_norm_g[l], k_norm_g[l],
                   sgu_ln_g[l], sgu_ln_b[l], w_s[l], b_s[l], attn_out_g[l], sgu_out_g[l], w_out[l],
                   norm2_g[l], w_router[l], router_bias[l], w_gate[l], w_up[l], w_down[l],
                   ws_gate[l], ws_up[l], ws_down[l])
    return (xa.reshape(bp, sp, d), xb.reshape(bs, ss, d))
```

```python
import functools
import math

import numpy as np
import jax
import jax.numpy as jnp
from jax import lax
from jax.experimental import pallas as pl
from jax.experimental.pallas import tpu as pltpu

F32 = jnp.float32
BF16 = jnp.bfloat16
I32 = jnp.int32

HEAD_DIM = 128
N_Q_HEADS = 8
N_KV_HEADS = 2
Q_PER_KV = N_Q_HEADS // N_KV_HEADS
ATTN_WIDTH = N_Q_HEADS * HEAD_DIM
KV_WIDTH = N_KV_HEADS * HEAD_DIM
ROPE_THETA = 10000.0
GRID_W = 64
N_A_HEADS = 8
A_HEAD_DIM = 128
A_WIDTH = N_A_HEADS * A_HEAD_DIM
CHUNK = 128
N_EXPERTS = 64
TOP_K = 8
N_GROUPS = 8
GROUP_SIZE = N_EXPERTS // N_GROUPS
TOPK_GROUPS = 4
ROUTED_SCALE = 2.5
N_MOD = 6
EPS = 1e-6

VMEM_LIMIT_BYTES = 56 * 1024 * 1024
NT_DIMS = (((1,), (1,)), ((), ()))


def _cparams(n_axes):
    return pltpu.CompilerParams(
        dimension_semantics=("arbitrary",) * n_axes, vmem_limit_bytes=VMEM_LIMIT_BYTES)


def _tile(n, pref):
    t = min(n, pref)
    while n % t:
        t //= 2
    return t


def _const_spec(shape):
    nd = len(shape)
    return pl.BlockSpec(shape, lambda *a: (0,) * nd)


def _mod_kernel(c_ref, w_ref, b_ref, o_ref):
    c = c_ref[...]
    a = (c * jax.nn.sigmoid(c)).astype(BF16)
    o_ref[...] = jnp.dot(a, w_ref[...].astype(BF16), preferred_element_type=F32) + b_ref[...]


def _mod(c8, w_mod, b_mod):
    d, n = w_mod.shape
    tn = _tile(n, 1024)
    return pl.pallas_call(
        _mod_kernel,
        grid=(n // tn,),
        in_specs=[_const_spec((8, d)),
                  pl.BlockSpec((d, tn), lambda j: (0, j)),
                  pl.BlockSpec((1, tn), lambda j: (0, j))],
        out_specs=pl.BlockSpec((8, tn), lambda j: (0, j)),
        out_shape=jax.ShapeDtypeStruct((8, n), F32),
        compiler_params=_cparams(1),
        name="mod",
    )(c8, w_mod, b_mod.reshape(1, n))


class _Groups:
    def __init__(self, groups):
        self.groups = groups
        self.n_tokens = sum(b * s for b, s in groups)

    def tile_maps(self, t):
        (b0, s0), (b1, s1) = self.groups
        n0 = b0 * s0 // t
        t0, t1 = s0 // t, s1 // t

        def bidx(i):
            return jnp.where(i < n0, i // t0, b0 + (i - n0) // t1)

        def pidx(i):
            return jnp.where(i < n0, i % t0, (i - n0) % t1)

        return bidx, pidx

    def split_specs(self, t, d):
        (b0, s0), _ = self.groups
        n0 = b0 * s0 // t
        return (pl.BlockSpec((t, d), lambda i, *_: (jnp.minimum(i, n0 - 1), 0)),
                pl.BlockSpec((t, d), lambda i, *_: (jnp.maximum(i - n0, 0), 0)), n0)


def _pick_group(i, n0, a_ref, b_ref):
    return jnp.where(i < n0, a_ref[...], b_ref[...])


LANES = 128
SUBLANES = 8
U32 = jnp.uint32


def _slab_rows(d):
    rows = d // (2 * LANES)
    assert d % (2 * LANES) == 0 and rows % SUBLANES == 0, d
    return rows


def _load_slabs(ref, rows, tok0, n_tok):
    pairs = [_unpack_bf16_pair(ref[pl.ds(tok0 * rows + r, n_tok, stride=rows), :]) for r in range(rows)]
    return jnp.concatenate([p[0] for p in pairs] + [p[1] for p in pairs], axis=1).astype(BF16)


def _pack_bf16_pair(lo, hi):
    def bf16_bits(v):
        return lax.bitcast_convert_type(v.astype(BF16).astype(F32), U32)

    return (bf16_bits(lo) >> 16) | bf16_bits(hi)


def _unpack_bf16_pair(w):
    return (lax.bitcast_convert_type(w << 16, F32),
            lax.bitcast_convert_type(w & jnp.uint32(0xFFFF0000), F32))


def _store_slabs(ref, x, tok0=0):
    n_tok, d = x.shape
    rows = _slab_rows(d)
    for r in range(rows):
        ref[pl.ds(tok0 * rows + r, n_tok, stride=rows), :] = _pack_bf16_pair(
            x[:, r * LANES:(r + 1) * LANES], x[:, (rows + r) * LANES:(rows + r + 1) * LANES])


def _rope_tables(seq_len):
    pos = np.arange(seq_len)
    row = (pos // GRID_W).astype(np.float64)
    col = (pos % GRID_W).astype(np.float64)
    n_pairs = HEAD_DIM // 4
    freqs = ROPE_THETA ** (-np.arange(n_pairs, dtype=np.float64) / n_pairs)
    ang = np.concatenate([row[:, None] * freqs, col[:, None] * freqs], axis=-1)
    c, s = np.cos(ang), np.sin(ang)
    cos2 = np.concatenate([c, c], axis=-1)
    sin2 = np.concatenate([-s, s], axis=-1)
    return cos2, sin2


def _inproj_kernel(xa_ref, xb_ref, mod_ref, g1_ref, wqt_ref, wk_ref, wvt_ref, wu_ref, wgv_ref,
                   qg_ref, kg_ref, cost_ref, sint_ref, cos_ref, sin_ref,
                   qt_ref, k_ref, vt_ref, u_ref, gv_ref, *, n0):
    x = _pick_group(pl.program_id(0), n0, xa_ref, xb_ref)
    shift1 = mod_ref[0:1, :]
    scale1 = mod_ref[1:2, :]
    ms = jnp.mean(x * x, axis=-1, keepdims=True)
    h = (x * lax.rsqrt(ms + EPS) * g1_ref[...]) * (1.0 + scale1) + shift1
    hb = h.astype(BF16)
    half = HEAD_DIM // 2

    qt = lax.dot_general(wqt_ref[...], hb, NT_DIMS, preferred_element_type=F32)
    cost = cost_ref[...]
    sint = sint_ref[...]
    qg = qg_ref[...]
    for hh in range(N_Q_HEADS):
        qh = qt[hh * HEAD_DIM:(hh + 1) * HEAD_DIM, :]
        msq = jnp.mean(qh * qh, axis=0, keepdims=True)
        qn = qh * lax.rsqrt(msq + EPS) * qg
        rot = jnp.concatenate([qn[half:, :], qn[:half, :]], axis=0)
        qt_ref[hh * HEAD_DIM:(hh + 1) * HEAD_DIM, :] = (qn * cost + rot * sint).astype(BF16)

    vt_ref[...] = lax.dot_general(wvt_ref[...], hb, NT_DIMS, preferred_element_type=F32).astype(BF16)

    kk = jnp.dot(hb, wk_ref[...], preferred_element_type=F32)
    cos = cos_ref[...]
    sin = sin_ref[...]
    kg = kg_ref[...]
    for j in range(N_KV_HEADS):
        kh = kk[:, j * HEAD_DIM:(j + 1) * HEAD_DIM]
        msk = jnp.mean(kh * kh, axis=-1, keepdims=True)
        kn = kh * lax.rsqrt(msk + EPS) * kg
        rot = jnp.concatenate([kn[:, half:], kn[:, :half]], axis=1)
        k_ref[:, j * HEAD_DIM:(j + 1) * HEAD_DIM] = (kn * cos + rot * sin).astype(BF16)

    u_ref[...] = jnp.dot(hb, wu_ref[...], preferred_element_type=F32).astype(BF16)
    gv_ref[...] = jnp.dot(hb, wgv_ref[...], preferred_element_type=F32).astype(BF16)


def _inproj(xa, xb, modp, g1, wqt, wk, wvt, wu, wgv, qg, kg, tabs, grp, tm):
    n, d = grp.n_tokens, xa.shape[1]
    bidx, pidx = grp.tile_maps(tm)
    cost, sint, cos, sin = tabs
    tok = lambda w: pl.BlockSpec((tm, w), lambda i: (i, 0))
    tokt = lambda w: pl.BlockSpec((w, tm), lambda i: (0, i))
    xa_spec, xb_spec, n0 = grp.split_specs(tm, d)
    return pl.pallas_call(
        functools.partial(_inproj_kernel, n0=n0),
        grid=(n // tm,),
        in_specs=[xa_spec, xb_spec,
                  pl.BlockSpec((None, 8, d), lambda i: (bidx(i), 0, 0)),
                  _const_spec((1, d)),
                  _const_spec(wqt.shape), _const_spec(wk.shape), _const_spec(wvt.shape),
                  _const_spec(wu.shape), _const_spec(wgv.shape),
                  _const_spec((HEAD_DIM, 1)), _const_spec((1, HEAD_DIM)),
                  pl.BlockSpec((HEAD_DIM, tm), lambda i: (0, pidx(i))),
                  pl.BlockSpec((HEAD_DIM, tm), lambda i: (0, pidx(i))),
                  pl.BlockSpec((tm, HEAD_DIM), lambda i: (pidx(i), 0)),
                  pl.BlockSpec((tm, HEAD_DIM), lambda i: (pidx(i), 0))],
        out_specs=[tokt(ATTN_WIDTH), tok(KV_WIDTH), tokt(KV_WIDTH), tok(A_WIDTH), tok(A_WIDTH)],
        out_shape=[jax.ShapeDtypeStruct((ATTN_WIDTH, n), BF16),
                   jax.ShapeDtypeStruct((n, KV_WIDTH), BF16),
                   jax.ShapeDtypeStruct((KV_WIDTH, n), BF16),
                   jax.ShapeDtypeStruct((n, A_WIDTH), BF16),
                   jax.ShapeDtypeStruct((n, A_WIDTH), BF16)],
        compiler_params=_cparams(1),
        name="inproj",
    )(xa, xb, modp, g1, wqt, wk, wvt, wu, wgv, qg, kg, cost, sint, cos, sin)


LOG2E = 1.4426950408889634
SAFE_SCORE_BOUND = 60.0
BOUND_SLACK = 1.01


def _flash_kernel(qt_s, kt_s, vp_s, fl_s, safe_s, q_ref, k_ref, v_ref, vp_ref, km_ref, o_ref,
                  qs_sc, m_sc, l_sc, acc_sc, p_sc, *, bq, hb):
    s_id = pl.program_id(1)
    flag = fl_s[s_id]
    safe = safe_s[0] != 0
    last = (flag & 2) != 0

    @pl.when((flag & 1) != 0)
    def _init():
        l_sc[...] = jnp.zeros(l_sc.shape, F32)
        acc_sc[...] = jnp.zeros(acc_sc.shape, F32)
        p_sc[...] = jnp.zeros(p_sc.shape, BF16)
        for hh in range(Q_PER_KV):
            qs_sc[:, hh * bq:(hh + 1) * bq] = q_ref[hh * HEAD_DIM:(hh + 1) * HEAD_DIM, :]
        qf = qs_sc[...].astype(F32)
        qn2 = jnp.sum(qf * qf, axis=0, keepdims=True)
        bound = jnp.sqrt(qn2 * km_ref[0:1, 0:1]) * BOUND_SLACK
        m_sc[...] = jnp.where(safe, bound, -jnp.inf)

    @pl.when(safe)
    def _fast():
        q = qs_sc[...]
        m = m_sc[...]
        sa = jnp.dot(k_ref[:hb, :], q, preferred_element_type=F32)
        acc = acc_sc[...] + jnp.dot(vp_ref[...], p_sc[...], preferred_element_type=F32)
        pa = jnp.exp2(sa - m)
        la = jnp.sum(pa, axis=0, keepdims=True)
        sb = jnp.dot(k_ref[hb:, :], q, preferred_element_type=F32)
        acc_sc[...] = acc + jnp.dot(v_ref[:, :hb], pa.astype(BF16), preferred_element_type=F32)
        pb = jnp.exp2(sb - m)
        l_sc[...] += la + jnp.sum(pb, axis=0, keepdims=True)
        p_sc[...] = pb.astype(BF16)

    @pl.when(jnp.logical_and(safe, last))
    def _flush():
        acc_sc[...] += jnp.dot(v_ref[:, hb:], p_sc[...], preferred_element_type=F32)

    @pl.when(jnp.logical_not(safe))
    def _online():
        s = jnp.dot(k_ref[...], qs_sc[...], preferred_element_type=F32)
        m_prev = m_sc[...]
        m_new = jnp.maximum(m_prev, jnp.max(s, axis=0, keepdims=True))
        alpha = jnp.exp2(m_prev - m_new)
        p = jnp.exp2(s - m_new)
        l_sc[...] = alpha * l_sc[...] + jnp.sum(p, axis=0, keepdims=True)
        acc_sc[...] = alpha * acc_sc[...] + jnp.dot(v_ref[...], p.astype(BF16), preferred_element_type=F32)
        m_sc[...] = m_new

    @pl.when((flag & 2) != 0)
    def _fin():
        o = acc_sc[...] / l_sc[...]
        for hh in range(Q_PER_KV):
            o_ref[:, hh * HEAD_DIM:(hh + 1) * HEAD_DIM] = o[:, hh * bq:(hh + 1) * bq].T.astype(BF16)


def _flash_schedule(grp, bq, bkv):
    qt, kt, vp, fl = [], [], [], []
    off = 0
    for b, s in grp.groups:
        for bi in range(b):
            base = off + bi * s
            nk = s // bkv
            for qi in range(s // bq):
                for ki in range(nk):
                    blk = base // bkv + ki
                    qt.append(base // bq + qi)
                    kt.append(blk)
                    vp.append(2 * (blk - 1 if ki else blk) + 1)
                    fl.append((1 if ki == 0 else 0) | (2 if ki == nk - 1 else 0))
        off += b * s
    return tuple(np.asarray(a, np.int32) for a in (qt, kt, vp, fl))


def _flash(qt, k, vt, kmax2, safe, grp, bq, bkv):
    n = k.shape[0]
    hb = bkv // 2
    qt_a, kt_a, vp_a, fl_a = _flash_schedule(grp, bq, bkv)
    n_steps = len(qt_a)
    w = Q_PER_KV * HEAD_DIM
    gs = pltpu.PrefetchScalarGridSpec(
        num_scalar_prefetch=5,
        grid=(N_KV_HEADS, n_steps),
        in_specs=[pl.BlockSpec((w, bq), lambda h, s, qa, ka, va, fa, sa: (h, qa[s])),
                  pl.BlockSpec((bkv, HEAD_DIM), lambda h, s, qa, ka, va, fa, sa: (ka[s], h)),
                  pl.BlockSpec((HEAD_DIM, bkv), lambda h, s, qa, ka, va, fa, sa: (h, ka[s])),
                  pl.BlockSpec((HEAD_DIM, hb), lambda h, s, qa, ka, va, fa, sa: (h, va[s])),
                  pl.BlockSpec((8, 128), lambda h, s, qa, ka, va, fa, sa: (0, 0))],
        out_specs=pl.BlockSpec((bq, w), lambda h, s, qa, ka, va, fa, sa: (qa[s], h)),
        scratch_shapes=[pltpu.VMEM((HEAD_DIM, Q_PER_KV * bq), BF16),
                        pltpu.VMEM((1, Q_PER_KV * bq), F32),
                        pltpu.VMEM((1, Q_PER_KV * bq), F32),
                        pltpu.VMEM((HEAD_DIM, Q_PER_KV * bq), F32),
                        pltpu.VMEM((hb, Q_PER_KV * bq), BF16)])
    return pl.pallas_call(
        functools.partial(_flash_kernel, bq=bq, hb=hb),
        grid_spec=gs,
        out_shape=jax.ShapeDtypeStruct((n, ATTN_WIDTH), BF16),
        compiler_params=_cparams(2),
        name="flash",
    )(jnp.asarray(qt_a), jnp.asarray(kt_a), jnp.asarray(vp_a), jnp.asarray(fl_a), safe, qt, k, vt, vt, kmax2)


def _sgu_kernel(u_ref, gv_ref, lng_ref, lnb_ref, ws_ref, bst_ref, og_ref, o_ref, *, n_chunks):
    v = gv_ref[...].astype(F32)
    mean = jnp.mean(v, axis=-1, keepdims=True)
    vc = v - mean
    var = jnp.mean(vc * vc, axis=-1, keepdims=True)
    vn = (vc * lax.rsqrt(var + EPS) * lng_ref[...] + lnb_ref[...]).astype(BF16)
    u = u_ref[...].astype(F32)
    heads = []
    for hh in range(N_A_HEADS):
        cols = slice(hh * A_HEAD_DIM, (hh + 1) * A_HEAD_DIM)
        rhs = jnp.concatenate([vn[c * CHUNK:(c + 1) * CHUNK, cols] for c in range(n_chunks)], axis=1)
        mixed = jnp.dot(ws_ref[hh], rhs, preferred_element_type=F32) + bst_ref[:, hh:hh + 1]
        heads.append(jnp.concatenate(
            [mixed[:, c * A_HEAD_DIM:(c + 1) * A_HEAD_DIM] for c in range(n_chunks)], axis=0))
    sg = u * jnp.concatenate(heads, axis=1)
    ms = jnp.mean(sg * sg, axis=-1, keepdims=True)
    o_ref[...] = (sg * lax.rsqrt(ms + EPS) * og_ref[...]).astype(BF16)


def _sgu(u, gv, lng, lnb, ws, bst, og, tm):
    n = u.shape[0]
    tok = pl.BlockSpec((tm, A_WIDTH), lambda i: (i, 0))
    return pl.pallas_call(
        functools.partial(_sgu_kernel, n_chunks=tm // CHUNK),
        grid=(n // tm,),
        in_specs=[tok, tok, _const_spec((1, A_WIDTH)), _const_spec((1, A_WIDTH)),
                  _const_spec(ws.shape), _const_spec(bst.shape), _const_spec((1, A_WIDTH))],
        out_specs=tok,
        out_shape=jax.ShapeDtypeStruct((n, A_WIDTH), BF16),
        compiler_params=_cparams(1),
        name="sgu",
    )(u, gv, lng, lnb, ws, bst, og)


def _outproj_kernel(attn_ref, sgu_ref, xa_ref, xb_ref, mod_ref, ag_ref, wo_ref, g2_ref, wrt_ref,
                    x1_ref, h2_ref, lgt_ref, *, n0):
    a = attn_ref[...].astype(F32)
    ms = jnp.mean(a * a, axis=-1, keepdims=True)
    an = (a * lax.rsqrt(ms + EPS) * ag_ref[...]).astype(BF16)
    o = jnp.dot(an, wo_ref[:ATTN_WIDTH, :], preferred_element_type=F32)
    o = o + jnp.dot(sgu_ref[...], wo_ref[ATTN_WIDTH:, :], preferred_element_type=F32)
    gate1 = mod_ref[2:3, :]
    shift2 = mod_ref[3:4, :]
    scale2 = mod_ref[4:5, :]
    x1 = _pick_group(pl.program_id(0), n0, xa_ref, xb_ref) + gate1 * o
    x1_ref[...] = x1
    ms2 = jnp.mean(x1 * x1, axis=-1, keepdims=True)
    h2 = (x1 * lax.rsqrt(ms2 + EPS) * g2_ref[...]) * (1.0 + scale2) + shift2
    _store_slabs(h2_ref, h2)
    lgt_ref[...] = lax.dot_general(wrt_ref[...], h2.astype(BF16), NT_DIMS, preferred_element_type=F32)


def _outproj(attn, sgun, xa, xb, modp, ag, wo, g2, wrt, grp, tm):
    n, d = grp.n_tokens, xa.shape[1]
    bidx, _ = grp.tile_maps(tm)
    tok = lambda w: pl.BlockSpec((tm, w), lambda i: (i, 0))
    xa_spec, xb_spec, n0 = grp.split_specs(tm, d)
    return pl.pallas_call(
        functools.partial(_outproj_kernel, n0=n0),
        grid=(n // tm,),
        in_specs=[tok(ATTN_WIDTH), tok(A_WIDTH), xa_spec, xb_spec,
                  pl.BlockSpec((None, 8, d), lambda i: (bidx(i), 0, 0)),
                  _const_spec((1, ATTN_WIDTH)), _const_spec(wo.shape), _const_spec((1, d)),
                  _const_spec(wrt.shape)],
        out_specs=[tok(d), pl.BlockSpec((tm * _slab_rows(d), LANES), lambda i: (i, 0)),
                   pl.BlockSpec((N_EXPERTS, tm), lambda i: (0, i))],
        out_shape=[jax.ShapeDtypeStruct((n, d), F32),
                   jax.ShapeDtypeStruct((n * _slab_rows(d), LANES), U32),
                   jax.ShapeDtypeStruct((N_EXPERTS, n), F32)],
        compiler_params=_cparams(1),
        name="outproj",
    )(attn, sgun, xa, xb, modp, ag, wo, g2, wrt)


def _first_index(hit, iota, sentinel):
    return jnp.min(jnp.where(hit, iota, sentinel), axis=0, keepdims=True)


def _router_kernel(lg_ref, bias_ref, idx_ref, w_ref, rank_ref, cnt_ref, run_sc):
    i = pl.program_id(0)
    tt = lg_ref.shape[1]

    @pl.when(i == 0)
    def _init():
        run_sc[...] = jnp.zeros(run_sc.shape, F32)

    scores = jax.nn.sigmoid(lg_ref[...])
    biased = scores + bias_ref[...]
    neg = -jnp.inf
    iota_g = lax.broadcasted_iota(I32, (GROUP_SIZE, tt), 0).astype(F32)

    gs_rows = []
    for g in range(N_GROUPS):
        xg = biased[g * GROUP_SIZE:(g + 1) * GROUP_SIZE, :]
        m1 = jnp.max(xg, axis=0, keepdims=True)
        f1 = _first_index(xg == m1, iota_g, float(GROUP_SIZE))
        m2 = jnp.max(jnp.where(iota_g == f1, neg, xg), axis=0, keepdims=True)
        gs_rows.append(m1 + m2)
    gs = jnp.concatenate(gs_rows, axis=0)

    iota_n = lax.broadcasted_iota(I32, (N_GROUPS, tt), 0).astype(F32)
    gsel = jnp.zeros((N_GROUPS, tt), F32)
    cur = gs
    for _ in range(TOPK_GROUPS):
        m = jnp.max(cur, axis=0, keepdims=True)
        f = _first_index(cur == m, iota_n, float(N_GROUPS))
        hit = iota_n == f
        gsel = jnp.where(hit, 1.0, gsel)
        cur = jnp.where(hit, neg, cur)

    masked = jnp.concatenate(
        [jnp.where(gsel[g:g + 1, :] > 0.5, biased[g * GROUP_SIZE:(g + 1) * GROUP_SIZE, :], neg)
         for g in range(N_GROUPS)], axis=0)

    iota_e = lax.broadcasted_iota(I32, (N_EXPERTS, tt), 0).astype(F32)
    sel = jnp.zeros((N_EXPERTS, tt), F32)
    idx_rows, w_rows = [], []
    for _ in range(TOP_K):
        m = jnp.max(masked, axis=0, keepdims=True)
        f = _first_index(masked == m, iota_e, float(N_EXPERTS))
        hit = iota_e == f
        idx_rows.append(f)
        w_rows.append(jnp.sum(jnp.where(hit, scores, 0.0), axis=0, keepdims=True))
        sel = jnp.where(hit, 1.0, sel)
        masked = jnp.where(hit, neg, masked)
    idx = jnp.concatenate(idx_rows, axis=0)
    wk = jnp.concatenate(w_rows, axis=0)
    wk = wk / (jnp.sum(wk, axis=0, keepdims=True) + 1e-20) * ROUTED_SCALE

    r = lax.broadcasted_iota(I32, (tt, tt), 0)
    c = lax.broadcasted_iota(I32, (tt, tt), 1)
    upper = jnp.where(r < c, 1.0, 0.0).astype(BF16)
    prefix = jnp.dot(sel.astype(BF16), upper, preferred_element_type=F32) + run_sc[...]
    rank_rows = [jnp.sum(jnp.where(iota_e == idx[k:k + 1, :], prefix, 0.0), axis=0, keepdims=True)
                 for k in range(TOP_K)]
    run_sc[...] = run_sc[...] + jnp.sum(sel, axis=1, keepdims=True)

    idx_ref[...] = idx.astype(I32)
    w_ref[...] = wk
    rank_ref[...] = jnp.concatenate(rank_rows, axis=0).astype(I32)
    cnt_ref[...] = jnp.broadcast_to(run_sc[...], cnt_ref.shape).astype(I32)


def _router(lgt, bias, tt):
    n = lgt.shape[1]
    tokk = pl.BlockSpec((TOP_K, tt), lambda i: (0, i))
    return pl.pallas_call(
        _router_kernel,
        grid=(n // tt,),
        in_specs=[pl.BlockSpec((N_EXPERTS, tt), lambda i: (0, i)), _const_spec((N_EXPERTS, 1))],
        out_specs=[tokk, tokk, tokk, _const_spec((N_EXPERTS, 128))],
        out_shape=[jax.ShapeDtypeStruct((TOP_K, n), I32), jax.ShapeDtypeStruct((TOP_K, n), F32),
                   jax.ShapeDtypeStruct((TOP_K, n), I32), jax.ShapeDtypeStruct((N_EXPERTS, 128), I32)],
        scratch_shapes=[pltpu.VMEM((N_EXPERTS, 1), F32)],
        compiler_params=_cparams(1),
        name="router",
    )(lgt, bias)


def _dest_kernel(idx_ref, rank_ref, start_ref, o_ref):
    tt = idx_ref.shape[1]
    iota_e = lax.broadcasted_iota(I32, (N_EXPERTS, tt), 0)
    start = start_ref[...]
    rows = []
    for k in range(TOP_K):
        hit = iota_e == idx_ref[k:k + 1, :]
        rows.append(jnp.sum(jnp.where(hit, start, 0), axis=0, keepdims=True))
    o_ref[...] = jnp.concatenate(rows, axis=0) + rank_ref[...]


def _dest(idx, rank, pad_start, tt):
    n = idx.shape[1]
    tokk = pl.BlockSpec((TOP_K, tt), lambda i: (0, i))
    return pl.pallas_call(
        _dest_kernel,
        grid=(n // tt,),
        in_specs=[tokk, tokk, _const_spec((N_EXPERTS, 1))],
        out_specs=tokk,
        out_shape=jax.ShapeDtypeStruct((TOP_K, n), I32),
        compiler_params=_cparams(1),
        name="dest",
    )(idx, rank, pad_start.reshape(N_EXPERTS, 1))


def _slab(ref, tok, rows):
    return ref.at[pl.ds(pl.multiple_of(tok * rows, SUBLANES), rows)]


def _dispatch_kernel(cnt_s, start_s, pend_s, dest_ref, h_ref, xs_ref, zero_sc, sem, *, td, bm, n_blocks):
    i = pl.program_id(0)
    rows = zero_sc.shape[0] // bm

    def slab_copy(t, slot):
        return pltpu.make_async_copy(_slab(h_ref, t, rows), _slab(xs_ref, slot, rows), sem)

    def issue(t, c):
        for k in range(TOP_K):
            slab_copy(t, dest_ref[0, t * TOP_K + k]).start(priority=k % 2)
        return c

    lax.fori_loop(0, td, issue, 0)

    n_rows = td * TOP_K * rows
    pltpu.make_async_copy(xs_ref.at[pl.ds(0, n_rows)], xs_ref.at[pl.ds(0, n_rows)], sem).wait()

    @pl.when(i == pl.num_programs(0) - 1)
    def _pad():
        zero_sc[...] = jnp.zeros(zero_sc.shape, U32)

        def pad_copy(p):
            return pltpu.make_async_copy(zero_sc.at[pl.ds(0, rows)], _slab(xs_ref, p, rows), sem)

        def tail_copy(b):
            return pltpu.make_async_copy(zero_sc, _slab(xs_ref, b, bm * rows), sem)

        first_unused = pend_s[N_EXPERTS - 1] // bm

        def tail_go(b, c):
            tail_copy(b).start()
            return c

        lax.fori_loop(first_unused, n_blocks, tail_go, 0)

        def tail_done(b, c):
            tail_copy(b).wait()
            return c

        lax.fori_loop(first_unused, n_blocks, tail_done, 0)

        def per_expert(e, c):
            lo = start_s[e] + cnt_s[e]
            hi = pend_s[e]

            def go(p, c2):
                pad_copy(p).start()
                return c2

            lax.fori_loop(lo, hi, go, 0)

            def done(p, c2):
                pad_copy(p).wait()
                return c2

            lax.fori_loop(lo, hi, done, 0)
            return c

        lax.fori_loop(0, N_EXPERTS, per_expert, 0)


def _dispatch(counts, pad_start, pad_end, dest3, h2s, n, p_tok, td, bm):
    rows = h2s.shape[0] // n
    gs = pltpu.PrefetchScalarGridSpec(
        num_scalar_prefetch=3,
        grid=(n // td,),
        in_specs=[pl.BlockSpec((None, 1, TOP_K * td), lambda i, *_: (i, 0, 0), memory_space=pltpu.SMEM),
                  pl.BlockSpec((td * rows, LANES), lambda i, *_: (i, 0))],
        out_specs=pl.BlockSpec(memory_space=pl.ANY),
        scratch_shapes=[pltpu.VMEM((bm * rows, LANES), U32), pltpu.SemaphoreType.DMA(())])
    return pl.pallas_call(
        functools.partial(_dispatch_kernel, td=td, bm=bm, n_blocks=p_tok // bm),
        grid_spec=gs,
        out_shape=jax.ShapeDtypeStruct((p_tok * rows, LANES), U32),
        compiler_params=_cparams(1),
        name="dispatch",
    )(counts, pad_start, pad_end, dest3, h2s)


def _swiglu_slabs(x_ref, wgu_ref, wd_ref, o_ref, *, packed_out):
    de, d = wd_ref.shape
    rows = _slab_rows(d)
    half = x_ref.shape[0] // rows // 2
    ups = [jnp.dot(_load_slabs(x_ref, rows, t0, half), wgu_ref[...],
                   preferred_element_type=F32) for t0 in (0, half)]
    for t0, r in zip((0, half), ups):
        a = r[:, :de]
        act = (a * jax.nn.sigmoid(a) * r[:, de:]).astype(BF16)
        out = jnp.dot(act, wd_ref[...], preferred_element_type=F32)
        if packed_out:
            _store_slabs(o_ref, out, t0)
        else:
            o_ref[t0:t0 + half, :] = out


def _shared_kernel(h_ref, wgu_ref, wd_ref, o_ref):
    _swiglu_slabs(h_ref, wgu_ref, wd_ref, o_ref, packed_out=False)


def _shared(h2s, wgu, wd, tm):
    d = wd.shape[1]
    rows = _slab_rows(d)
    n = h2s.shape[0] // rows
    return pl.pallas_call(
        _shared_kernel,
        grid=(n // tm,),
        in_specs=[pl.BlockSpec((tm * rows, LANES), lambda i: (i, 0)),
                  _const_spec(wgu.shape), _const_spec(wd.shape)],
        out_specs=pl.BlockSpec((tm, d), lambda i: (i, 0)),
        out_shape=jax.ShapeDtypeStruct((n, d), F32),
        compiler_params=_cparams(1),
        name="shared",
    )(h2s, wgu, wd)


def _experts_kernel(be_s, nu_s, x_ref, wg_ref, wu_ref, wd_ref, o_ref, wgu_sc, wd_sc):
    i = pl.program_id(0)
    used = i < nu_s[0]
    prev = be_s[jnp.maximum(i - 1, 0)]
    fresh = jnp.logical_or(i == 0, be_s[i] != prev)
    de = wd_sc.shape[0]

    @pl.when(jnp.logical_and(used, fresh))
    def _cast():
        wgu_sc[:, :de] = wg_ref[...].astype(BF16)
        wgu_sc[:, de:] = wu_ref[...].astype(BF16)
        wd_sc[...] = wd_ref[...].astype(BF16)

    @pl.when(used)
    def _run():
        _swiglu_slabs(x_ref, wgu_sc, wd_sc, o_ref, packed_out=True)

    @pl.when(jnp.logical_not(used))
    def _unused():
        o_ref[...] = jnp.zeros(o_ref.shape, U32)


def _experts(block_exp, n_used, xs, w_gate, w_up, w_down, bm):
    d, de = w_gate.shape[1], w_gate.shape[2]
    rows = _slab_rows(d)
    p_tok = xs.shape[0] // rows
    n_blocks = p_tok // bm
    row = lambda i, be, nu: (jnp.minimum(i, nu[0] - 1), 0)
    wsel = lambda i, be, nu: (be[jnp.minimum(i, nu[0] - 1)], 0, 0)
    gs = pltpu.PrefetchScalarGridSpec(
        num_scalar_prefetch=2,
        grid=(n_blocks,),
        in_specs=[pl.BlockSpec((bm * rows, LANES), row),
                  pl.BlockSpec((None, d, de), wsel),
                  pl.BlockSpec((None, d, de), wsel),
                  pl.BlockSpec((None, de, d), wsel)],
        out_specs=pl.BlockSpec((bm * rows, LANES), lambda i, be, nu: (i, 0)),
        scratch_shapes=[pltpu.VMEM((d, 2 * de), BF16), pltpu.VMEM((de, d), BF16)])
    return pl.pallas_call(
        _experts_kernel,
        grid_spec=gs,
        out_shape=jax.ShapeDtypeStruct(xs.shape, U32),
        compiler_params=_cparams(1),
        name="experts",
    )(block_exp, n_used, xs, w_gate, w_up, w_down)


def _combine_kernel(dest_ref, nxt_ref, w_ref, x1_ref, sh_ref, mod_ref, ys_ref, oa_ref, ob_ref,
                    buf, lo_sc, hi_sc, sems, *, tc, n0):
    i = pl.program_id(0)
    rows = lo_sc.shape[0] // tc
    cur = i % 2

    def gather(d_ref, b):
        def issue(t, c):
            for k in range(TOP_K):
                pltpu.make_async_copy(_slab(ys_ref, d_ref[0, t * TOP_K + k], rows),
                                      _slab(buf.at[b, k], t, rows),
                                      sems.at[b]).start(priority=k % 2)
            return c

        lax.fori_loop(0, tc, issue, 0)

    @pl.when(i == 0)
    def _():
        gather(dest_ref, 0)

    @pl.when(i + 1 < pl.num_programs(0))
    def _():
        gather(nxt_ref, 1 - cur)

    pltpu.make_async_copy(buf.at[cur], buf.at[cur], sems.at[cur]).wait()

    def token_sum(t, c):
        r0 = pl.multiple_of(t * rows, SUBLANES)
        lo = hi = None
        for k in range(TOP_K):
            lo_k, hi_k = _unpack_bf16_pair(buf[cur, k, pl.ds(r0, rows), :])
            wk = w_ref[0, t * TOP_K + k]
            lo = wk * lo_k if lo is None else lo + wk * lo_k
            hi = wk * hi_k if hi is None else hi + wk * hi_k
        lo_sc[pl.ds(r0, rows), :] = lo
        hi_sc[pl.ds(r0, rows), :] = hi
        return c

    lax.fori_loop(0, tc, token_sum, 0, unroll=4)

    routed = jnp.concatenate([lo_sc[pl.ds(r, tc, stride=rows), :] for r in range(rows)]
                             + [hi_sc[pl.ds(r, tc, stride=rows), :] for r in range(rows)], axis=1)
    y = x1_ref[...] + mod_ref[5:6, :] * (sh_ref[...] + routed)

    @pl.when(pl.program_id(0) < n0)
    def _():
        oa_ref[...] = y

    @pl.when(pl.program_id(0) >= n0)
    def _():
        ob_ref[...] = y


def _combine(dest3, w3, x1, shared, modp, ys, grp, tc):
    n, d = x1.shape
    rows = _slab_rows(d)
    bidx, _ = grp.tile_maps(tc)
    tok = pl.BlockSpec((tc, d), lambda i: (i, 0))
    oa_spec, ob_spec, n0 = grp.split_specs(tc, d)
    (b0, s0), (b1, s1) = grp.groups
    return pl.pallas_call(
        functools.partial(_combine_kernel, tc=tc, n0=n0),
        grid=(n // tc,),
        in_specs=[pl.BlockSpec((None, 1, TOP_K * tc), lambda i: (i, 0, 0), memory_space=pltpu.SMEM),
                  pl.BlockSpec((None, 1, TOP_K * tc), lambda i: (jnp.minimum(i + 1, n // tc - 1), 0, 0),
                               memory_space=pltpu.SMEM),
                  pl.BlockSpec((None, 1, TOP_K * tc), lambda i: (i, 0, 0), memory_space=pltpu.SMEM),
                  tok, tok,
                  pl.BlockSpec((None, 8, d), lambda i: (bidx(i), 0, 0)),
                  pl.BlockSpec(memory_space=pl.ANY)],
        out_specs=[oa_spec, ob_spec],
        out_shape=[jax.ShapeDtypeStruct((b0 * s0, d), F32), jax.ShapeDtypeStruct((b1 * s1, d), F32)],
        scratch_shapes=[pltpu.VMEM((2, TOP_K, tc * rows, LANES), U32),
                        pltpu.VMEM((tc * rows, LANES), F32), pltpu.VMEM((tc * rows, LANES), F32),
                        pltpu.SemaphoreType.DMA((2,))],
        compiler_params=_cparams(1),
        name="combine",
    )(dest3, dest3, w3, x1, shared, modp, ys)


def _tok3(a, t):
    k, n = a.shape
    return a.reshape(k, n // t, t).transpose(1, 2, 0).reshape(n // t, 1, t * k)


def _layer(xa, xb, c, grp, norm1_g, w_mod, b_mod, w_in, q_norm_g, k_norm_g, sgu_ln_g, sgu_ln_b, w_s, b_s,
           attn_out_g, sgu_out_g, w_out, norm2_g, w_router, router_bias, w_gate, w_up, w_down,
           ws_gate, ws_up, ws_down):
    n, d = grp.n_tokens, xa.shape[1]
    nb = c.shape[0]
    s_min = min(s for _, s in grp.groups)
    s_max = max(s for _, s in grp.groups)

    c8 = jnp.pad(c, ((0, 8 - nb), (0, 0)))
    mod = _mod(c8, w_mod, b_mod)[:nb].reshape(nb, N_MOD, d)
    modp = jnp.pad(mod, ((0, 0), (0, 8 - N_MOD), (0, 0)))

    perm = np.concatenate([np.arange(0, HEAD_DIM, 2), np.arange(1, HEAD_DIM, 2)])
    c0, c1, c2, c3 = ATTN_WIDTH, ATTN_WIDTH + KV_WIDTH, ATTN_WIDTH + 2 * KV_WIDTH, ATTN_WIDTH + 2 * KV_WIDTH + A_WIDTH
    wq = w_in[:, :c0].reshape(d, N_Q_HEADS, HEAD_DIM)[:, :, perm].reshape(d, ATTN_WIDTH)
    wk = w_in[:, c0:c1].reshape(d, N_KV_HEADS, HEAD_DIM)[:, :, perm].reshape(d, KV_WIDTH)
    wqt = wq.T.astype(BF16)
    wkb = wk.astype(BF16)
    wvt = w_in[:, c1:c2].T.astype(BF16)
    wu = w_in[:, c2:c3].astype(BF16)
    wgv = w_in[:, c3:].astype(BF16)
    qg = q_norm_g[perm].reshape(HEAD_DIM, 1)
    kg = k_norm_g[perm].reshape(1, HEAD_DIM)

    cos2, sin2 = _rope_tables(s_max)
    scale = HEAD_DIM ** -0.5 * LOG2E
    tabs = (jnp.asarray((cos2 * scale).T, F32), jnp.asarray((sin2 * scale).T, F32),
            jnp.asarray(cos2, F32), jnp.asarray(sin2, F32))

    tm = _tile(s_min, 512)
    qt, k, vt, u, gv = _inproj(xa, xb, modp, norm1_g.reshape(1, d), wqt, wkb, wvt, wu, wgv, qg, kg, tabs,
                               grp, tm)

    k_bound2 = HEAD_DIM * jnp.max(jnp.square(k_norm_g)) * BOUND_SLACK
    q_bound = math.sqrt(HEAD_DIM) * scale * BOUND_SLACK * jnp.max(jnp.abs(q_norm_g))
    safe = (q_bound * jnp.sqrt(k_bound2) * BOUND_SLACK < SAFE_SCORE_BOUND).astype(I32).reshape(1)
    attn = _flash(qt, k, vt, jnp.full((8, 128), k_bound2, F32), safe, grp,
                  _tile(s_min, 1024), _tile(s_min, 1024))
    sgun = _sgu(u, gv, sgu_ln_g.reshape(1, A_WIDTH), sgu_ln_b.reshape(1, A_WIDTH), w_s.astype(BF16),
                b_s.T, sgu_out_g.reshape(1, A_WIDTH), _tile(s_min, 512))

    x1, h2, lgt = _outproj(attn, sgun, xa, xb, modp, attn_out_g.reshape(1, ATTN_WIDTH), w_out.astype(BF16),
                           norm2_g.reshape(1, d), w_router.T.astype(BF16), grp, _tile(s_min, 256))

    idx, wts, rank, cnt = _router(lgt, router_bias.reshape(N_EXPERTS, 1), _tile(s_min, 512))

    bm = 512
    counts = cnt[:, 0]
    padded = (counts + bm - 1) // bm * bm
    pad_end = jnp.cumsum(padded).astype(I32)
    pad_start = pad_end - padded
    n_blocks = n * TOP_K // bm + N_EXPERTS
    p_tok = n_blocks * bm
    n_used = (pad_end[-1] // bm).reshape(1).astype(I32)
    block_start = jnp.arange(n_blocks, dtype=I32) * bm
    block_exp = jnp.minimum(jnp.sum((pad_end[None, :] <= block_start[:, None]).astype(I32), axis=1),
                            N_EXPERTS - 1)

    dest = _dest(idx, rank, pad_start, _tile(s_min, 2048))

    td = _tile(s_min, 256)
    xs = _dispatch(counts, pad_start, pad_end, _tok3(dest, td), h2, n, p_tok, td, bm)
    shared = _shared(h2, jnp.concatenate([ws_gate, ws_up], axis=1).astype(BF16), ws_down.astype(BF16),
                     _tile(s_min, 512))
    ys = _experts(block_exp, n_used, xs, w_gate, w_up, w_down, bm)

    tc = _tile(s_min, 128)
    return _combine(_tok3(dest, tc), _tok3(wts, tc), x1, shared, modp, ys, grp, tc)


def kernel(x_prompt, x_sample, c_prompt, c_sample, norm1_g, w_mod, b_mod, w_in, q_norm_g, k_norm_g, sgu_ln_g, sgu_ln_b, w_s, b_s, attn_out_g, sgu_out_g, w_out, norm2_g, w_router, router_bias, w_gate, w_up, w_down, ws_gate, ws_up, ws_down):
    depth = norm1_g.shape[0]
    bp, sp, d = x_prompt.shape
    bs, ss, _ = x_sample.shape
    grp = _Groups([(bp, sp), (bs, ss)])
    xa, xb = x_prompt.reshape(bp * sp, d), x_sample.reshape(bs * ss, d)
    c = jnp.concatenate([c_prompt, c_sample], axis=0)
    for l in range(depth):
        xa, xb = _layer(xa, xb, c, grp, norm1_g[l], w_mod[l], b_mod[l], w_in[l], q_norm_g[l], k_norm_g[l],
                   sgu_ln_g[l], sgu_ln_b[l], w_s[l], b_s[l], attn_out_g[l], sgu_out_g[l], w_out[l],
                   norm2_g[l], w_router[l], router_bias[l], w_gate[l], w_up[l], w_down[l],
                   ws_gate[l], ws_up[l], ws_down[l])
    return (xa.reshape(bp, sp, d), xb.reshape(bs, ss, d))
```

```python
import functools
import math

import numpy as np
import jax
import jax.numpy as jnp
from jax import lax
from jax.experimental import pallas as pl
from jax.experimental.pallas import tpu as pltpu

F32 = jnp.float32
BF16 = jnp.bfloat16
I32 = jnp.int32

HEAD_DIM = 128
N_Q_HEADS = 8
N_KV_HEADS = 2
Q_PER_KV = N_Q_HEADS // N_KV_HEADS
ATTN_WIDTH = N_Q_HEADS * HEAD_DIM
KV_WIDTH = N_KV_HEADS * HEAD_DIM
ROPE_THETA = 10000.0
GRID_W = 64
N_A_HEADS = 8
A_HEAD_DIM = 128
A_WIDTH = N_A_HEADS * A_HEAD_DIM
CHUNK = 128
N_EXPERTS = 64
TOP_K = 8
N_GROUPS = 8
GROUP_SIZE = N_EXPERTS // N_GROUPS
TOPK_GROUPS = 4
ROUTED_SCALE = 2.5
N_MOD = 6
EPS = 1e-6

VMEM_LIMIT_BYTES = 56 * 1024 * 1024
NT_DIMS = (((1,), (1,)), ((), ()))


def _cparams(n_axes):
    return pltpu.CompilerParams(
        dimension_semantics=("arbitrary",) * n_axes, vmem_limit_bytes=VMEM_LIMIT_BYTES)


def _tile(n, pref):
    t = min(n, pref)
    while n % t:
        t //= 2
    return t


def _const_spec(shape):
    nd = len(shape)
    return pl.BlockSpec(shape, lambda *a: (0,) * nd, pipeline_mode=pl.Buffered(1))


def _mod_kernel(c_ref, w_ref, b_ref, o_ref):
    c = c_ref[...]
    a = (c * jax.nn.sigmoid(c)).astype(BF16)
    o_ref[...] = jnp.dot(a, w_ref[...].astype(BF16), preferred_element_type=F32) + b_ref[...]


def _mod(c8, w_mod, b_mod):
    d, n = w_mod.shape
    tn = _tile(n, 1024)
    return pl.pallas_call(
        _mod_kernel,
        grid=(n // tn,),
        in_specs=[_const_spec((8, d)),
                  pl.BlockSpec((d, tn), lambda j: (0, j)),
                  pl.BlockSpec((1, tn), lambda j: (0, j))],
        out_specs=pl.BlockSpec((8, tn), lambda j: (0, j)),
        out_shape=jax.ShapeDtypeStruct((8, n), F32),
        compiler_params=_cparams(1),
        name="mod",
    )(c8, w_mod, b_mod.reshape(1, n))


class _Groups:
    def __init__(self, groups):
        self.groups = groups
        self.n_tokens = sum(b * s for b, s in groups)

    def tile_maps(self, t):
        (b0, s0), (b1, s1) = self.groups
        n0 = b0 * s0 // t
        t0, t1 = s0 // t, s1 // t

        def bidx(i):
            return jnp.where(i < n0, i // t0, b0 + (i - n0) // t1)

        def pidx(i):
            return jnp.where(i < n0, i % t0, (i - n0) % t1)

        return bidx, pidx

    def split_specs(self, t, d):
        (b0, s0), _ = self.groups
        n0 = b0 * s0 // t
        return (pl.BlockSpec((t, d), lambda i, *_: (jnp.minimum(i, n0 - 1), 0)),
                pl.BlockSpec((t, d), lambda i, *_: (jnp.maximum(i - n0, 0), 0)), n0)


def _pick_group(i, n0, a_ref, b_ref):
    return jnp.where(i < n0, a_ref[...], b_ref[...])


LANES = 128
SUBLANES = 8
U32 = jnp.uint32


def _slab_rows(d):
    rows = d // (2 * LANES)
    assert d % (2 * LANES) == 0 and rows % SUBLANES == 0, d
    return rows


def _load_slabs(ref, rows, tok0, n_tok):
    pairs = [_unpack_bf16_pair(ref[pl.ds(tok0 * rows + r, n_tok, stride=rows), :]) for r in range(rows)]
    return jnp.concatenate([p[0] for p in pairs] + [p[1] for p in pairs], axis=1).astype(BF16)


def _pack_bf16_pair(lo, hi):
    def bf16_bits(v):
        return lax.bitcast_convert_type(v.astype(BF16).astype(F32), U32)

    return (bf16_bits(lo) >> 16) | bf16_bits(hi)


def _unpack_bf16_pair(w):
    return (lax.bitcast_convert_type(w << 16, F32),
            lax.bitcast_convert_type(w & jnp.uint32(0xFFFF0000), F32))


def _store_slabs(ref, x, tok0=0):
    n_tok, d = x.shape
    rows = _slab_rows(d)
    for r in range(rows):
        ref[pl.ds(tok0 * rows + r, n_tok, stride=rows), :] = _pack_bf16_pair(
            x[:, r * LANES:(r + 1) * LANES], x[:, (rows + r) * LANES:(rows + r + 1) * LANES])


def _rope_tables(seq_len):
    pos = np.arange(seq_len)
    row = (pos // GRID_W).astype(np.float64)
    col = (pos % GRID_W).astype(np.float64)
    n_pairs = HEAD_DIM // 4
    freqs = ROPE_THETA ** (-np.arange(n_pairs, dtype=np.float64) / n_pairs)
    ang = np.concatenate([row[:, None] * freqs, col[:, None] * freqs], axis=-1)
    c, s = np.cos(ang), np.sin(ang)
    cos2 = np.concatenate([c, c], axis=-1)
    sin2 = np.concatenate([-s, s], axis=-1)
    return cos2, sin2


def _inproj_kernel(xa_ref, xb_ref, mod_ref, g1_ref, wqt_ref, wk_ref, wvt_ref, wu_ref, wgv_ref,
                   qg_ref, kg_ref, cost_ref, sint_ref, cos_ref, sin_ref,
                   qt_ref, k_ref, vt_ref, u_ref, gv_ref, *, n0):
    x = _pick_group(pl.program_id(0), n0, xa_ref, xb_ref)
    shift1 = mod_ref[0:1, :]
    scale1 = mod_ref[1:2, :]
    ms = jnp.mean(x * x, axis=-1, keepdims=True)
    h = (x * lax.rsqrt(ms + EPS) * g1_ref[...]) * (1.0 + scale1) + shift1
    hb = h.astype(BF16)
    half = HEAD_DIM // 2

    qt = lax.dot_general(wqt_ref[...], hb, NT_DIMS, preferred_element_type=F32)
    cost = cost_ref[...]
    sint = sint_ref[...]
    qg = qg_ref[...]
    for hh in range(N_Q_HEADS):
        qh = qt[hh * HEAD_DIM:(hh + 1) * HEAD_DIM, :]
        msq = jnp.mean(qh * qh, axis=0, keepdims=True)
        qn = qh * lax.rsqrt(msq + EPS) * qg
        rot = jnp.concatenate([qn[half:, :], qn[:half, :]], axis=0)
        qt_ref[hh * HEAD_DIM:(hh + 1) * HEAD_DIM, :] = (qn * cost + rot * sint).astype(BF16)

    vt_ref[...] = lax.dot_general(wvt_ref[...], hb, NT_DIMS, preferred_element_type=F32).astype(BF16)

    kk = jnp.dot(hb, wk_ref[...], preferred_element_type=F32)
    cos = cos_ref[...]
    sin = sin_ref[...]
    kg = kg_ref[...]
    for j in range(N_KV_HEADS):
        kh = kk[:, j * HEAD_DIM:(j + 1) * HEAD_DIM]
        msk = jnp.mean(kh * kh, axis=-1, keepdims=True)
        kn = kh * lax.rsqrt(msk + EPS) * kg
        rot = jnp.concatenate([kn[:, half:], kn[:, :half]], axis=1)
        k_ref[:, j * HEAD_DIM:(j + 1) * HEAD_DIM] = (kn * cos + rot * sin).astype(BF16)

    u_ref[...] = jnp.dot(hb, wu_ref[...], preferred_element_type=F32).astype(BF16)
    gv_ref[...] = jnp.dot(hb, wgv_ref[...], preferred_element_type=F32).astype(BF16)


def _inproj(xa, xb, modp, g1, wqt, wk, wvt, wu, wgv, qg, kg, tabs, grp, tm):
    n, d = grp.n_tokens, xa.shape[1]
    bidx, pidx = grp.tile_maps(tm)
    cost, sint, cos, sin = tabs
    tok = lambda w: pl.BlockSpec((tm, w), lambda i: (i, 0))
    tokt = lambda w: pl.BlockSpec((w, tm), lambda i: (0, i))
    xa_spec, xb_spec, n0 = grp.split_specs(tm, d)
    return pl.pallas_call(
        functools.partial(_inproj_kernel, n0=n0),
        grid=(n // tm,),
        in_specs=[xa_spec, xb_spec,
                  pl.BlockSpec((None, 8, d), lambda i: (bidx(i), 0, 0)),
                  _const_spec((1, d)),
                  _const_spec(wqt.shape), _const_spec(wk.shape), _const_spec(wvt.shape),
                  _const_spec(wu.shape), _const_spec(wgv.shape),
                  _const_spec((HEAD_DIM, 1)), _const_spec((1, HEAD_DIM)),
                  pl.BlockSpec((HEAD_DIM, tm), lambda i: (0, pidx(i))),
                  pl.BlockSpec((HEAD_DIM, tm), lambda i: (0, pidx(i))),
                  pl.BlockSpec((tm, HEAD_DIM), lambda i: (pidx(i), 0)),
                  pl.BlockSpec((tm, HEAD_DIM), lambda i: (pidx(i), 0))],
        out_specs=[tokt(ATTN_WIDTH), tok(KV_WIDTH), tokt(KV_WIDTH), tok(A_WIDTH), tok(A_WIDTH)],
        out_shape=[jax.ShapeDtypeStruct((ATTN_WIDTH, n), BF16),
                   jax.ShapeDtypeStruct((n, KV_WIDTH), BF16),
                   jax.ShapeDtypeStruct((KV_WIDTH, n), BF16),
                   jax.ShapeDtypeStruct((n, A_WIDTH), BF16),
                   jax.ShapeDtypeStruct((n, A_WIDTH), BF16)],
        compiler_params=_cparams(1),
        name="inproj",
    )(xa, xb, modp, g1, wqt, wk, wvt, wu, wgv, qg, kg, cost, sint, cos, sin)


LOG2E = 1.4426950408889634
SAFE_SCORE_BOUND = 60.0
BOUND_SLACK = 1.01


def _flash_kernel(qt_s, kt_s, vp_s, fl_s, safe_s, q_ref, k_ref, v_ref, vp_ref, km_ref, o_ref,
                  qs_sc, m_sc, l_sc, acc_sc, p_sc, *, bq, hb):
    s_id = pl.program_id(1)
    flag = fl_s[s_id]
    safe = safe_s[0] != 0
    last = (flag & 2) != 0

    @pl.when((flag & 1) != 0)
    def _init():
        l_sc[...] = jnp.zeros(l_sc.shape, F32)
        acc_sc[...] = jnp.zeros(acc_sc.shape, F32)
        p_sc[...] = jnp.zeros(p_sc.shape, BF16)
        for hh in range(Q_PER_KV):
            qs_sc[:, hh * bq:(hh + 1) * bq] = q_ref[hh * HEAD_DIM:(hh + 1) * HEAD_DIM, :]
        qf = qs_sc[...].astype(F32)
        qn2 = jnp.sum(qf * qf, axis=0, keepdims=True)
        bound = jnp.sqrt(qn2 * km_ref[0:1, 0:1]) * BOUND_SLACK
        m_sc[...] = jnp.where(safe, bound, -jnp.inf)

    @pl.when(safe)
    def _fast():
        q = qs_sc[...]
        m = m_sc[...]
        sa = jnp.dot(k_ref[:hb, :], q, preferred_element_type=F32)
        acc = acc_sc[...] + jnp.dot(vp_ref[...], p_sc[...], preferred_element_type=F32)
        pa = jnp.exp2(sa - m)
        la = jnp.sum(pa, axis=0, keepdims=True)
        sb = jnp.dot(k_ref[hb:, :], q, preferred_element_type=F32)
        acc_sc[...] = acc + jnp.dot(v_ref[:, :hb], pa.astype(BF16), preferred_element_type=F32)
        pb = jnp.exp2(sb - m)
        l_sc[...] += la + jnp.sum(pb, axis=0, keepdims=True)
        p_sc[...] = pb.astype(BF16)

    @pl.when(jnp.logical_and(safe, last))
    def _flush():
        acc_sc[...] += jnp.dot(v_ref[:, hb:], p_sc[...], preferred_element_type=F32)

    @pl.when(jnp.logical_not(safe))
    def _online():
        s = jnp.dot(k_ref[...], qs_sc[...], preferred_element_type=F32)
        m_prev = m_sc[...]
        m_new = jnp.maximum(m_prev, jnp.max(s, axis=0, keepdims=True))
        alpha = jnp.exp2(m_prev - m_new)
        p = jnp.exp2(s - m_new)
        l_sc[...] = alpha * l_sc[...] + jnp.sum(p, axis=0, keepdims=True)
        acc_sc[...] = alpha * acc_sc[...] + jnp.dot(v_ref[...], p.astype(BF16), preferred_element_type=F32)
        m_sc[...] = m_new

    @pl.when((flag & 2) != 0)
    def _fin():
        o = acc_sc[...] / l_sc[...]
        for hh in range(Q_PER_KV):
            o_ref[:, hh * HEAD_DIM:(hh + 1) * HEAD_DIM] = o[:, hh * bq:(hh + 1) * bq].T.astype(BF16)


def _flash_schedule(grp, bq, bkv):
    qt, kt, vp, fl = [], [], [], []
    off = 0
    for b, s in grp.groups:
        for bi in range(b):
            base = off + bi * s
            nk = s // bkv
            for qi in range(s // bq):
                for ki in range(nk):
                    blk = base // bkv + ki
                    qt.append(base // bq + qi)
                    kt.append(blk)
                    vp.append(2 * (blk - 1 if ki else blk) + 1)
                    fl.append((1 if ki == 0 else 0) | (2 if ki == nk - 1 else 0))
        off += b * s
    return tuple(np.asarray(a, np.int32) for a in (qt, kt, vp, fl))


def _flash(qt, k, vt, kmax2, safe, grp, bq, bkv):
    n = k.shape[0]
    hb = bkv // 2
    qt_a, kt_a, vp_a, fl_a = _flash_schedule(grp, bq, bkv)
    n_steps = len(qt_a)
    w = Q_PER_KV * HEAD_DIM
    gs = pltpu.PrefetchScalarGridSpec(
        num_scalar_prefetch=5,
        grid=(N_KV_HEADS, n_steps),
        in_specs=[pl.BlockSpec((w, bq), lambda h, s, qa, ka, va, fa, sa: (h, qa[s])),
                  pl.BlockSpec((bkv, HEAD_DIM), lambda h, s, qa, ka, va, fa, sa: (ka[s], h)),
                  pl.BlockSpec((HEAD_DIM, bkv), lambda h, s, qa, ka, va, fa, sa: (h, ka[s])),
                  pl.BlockSpec((HEAD_DIM, hb), lambda h, s, qa, ka, va, fa, sa: (h, va[s])),
                  pl.BlockSpec((8, 128), lambda h, s, qa, ka, va, fa, sa: (0, 0))],
        out_specs=pl.BlockSpec((bq, w), lambda h, s, qa, ka, va, fa, sa: (qa[s], h)),
        scratch_shapes=[pltpu.VMEM((HEAD_DIM, Q_PER_KV * bq), BF16),
                        pltpu.VMEM((1, Q_PER_KV * bq), F32),
                        pltpu.VMEM((1, Q_PER_KV * bq), F32),
                        pltpu.VMEM((HEAD_DIM, Q_PER_KV * bq), F32),
                        pltpu.VMEM((hb, Q_PER_KV * bq), BF16)])
    return pl.pallas_call(
        functools.partial(_flash_kernel, bq=bq, hb=hb),
        grid_spec=gs,
        out_shape=jax.ShapeDtypeStruct((n, ATTN_WIDTH), BF16),
        compiler_params=_cparams(2),
        name="flash",
    )(jnp.asarray(qt_a), jnp.asarray(kt_a), jnp.asarray(vp_a), jnp.asarray(fl_a), safe, qt, k, vt, vt, kmax2)


def _sgu_kernel(u_ref, gv_ref, lng_ref, lnb_ref, ws_ref, bst_ref, og_ref, o_ref, *, n_chunks):
    v = gv_ref[...].astype(F32)
    mean = jnp.mean(v, axis=-1, keepdims=True)
    vc = v - mean
    var = jnp.mean(vc * vc, axis=-1, keepdims=True)
    vn = (vc * lax.rsqrt(var + EPS) * lng_ref[...] + lnb_ref[...]).astype(BF16)
    u = u_ref[...].astype(F32)
    heads = []
    for hh in range(N_A_HEADS):
        cols = slice(hh * A_HEAD_DIM, (hh + 1) * A_HEAD_DIM)
        rhs = jnp.concatenate([vn[c * CHUNK:(c + 1) * CHUNK, cols] for c in range(n_chunks)], axis=1)
        mixed = jnp.dot(ws_ref[hh], rhs, preferred_element_type=F32) + bst_ref[:, hh:hh + 1]
        heads.append(jnp.concatenate(
            [mixed[:, c * A_HEAD_DIM:(c + 1) * A_HEAD_DIM] for c in range(n_chunks)], axis=0))
    sg = u * jnp.concatenate(heads, axis=1)
    ms = jnp.mean(sg * sg, axis=-1, keepdims=True)
    o_ref[...] = (sg * lax.rsqrt(ms + EPS) * og_ref[...]).astype(BF16)


def _sgu(u, gv, lng, lnb, ws, bst, og, tm):
    n = u.shape[0]
    tok = pl.BlockSpec((tm, A_WIDTH), lambda i: (i, 0))
    return pl.pallas_call(
        functools.partial(_sgu_kernel, n_chunks=tm // CHUNK),
        grid=(n // tm,),
        in_specs=[tok, tok, _const_spec((1, A_WIDTH)), _const_spec((1, A_WIDTH)),
                  _const_spec(ws.shape), _const_spec(bst.shape), _const_spec((1, A_WIDTH))],
        out_specs=tok,
        out_shape=jax.ShapeDtypeStruct((n, A_WIDTH), BF16),
        compiler_params=_cparams(1),
        name="sgu",
    )(u, gv, lng, lnb, ws, bst, og)


def _outproj_kernel(attn_ref, sgu_ref, xa_ref, xb_ref, mod_ref, ag_ref, wo_ref, g2_ref, wrt_ref,
                    x1_ref, h2_ref, lgt_ref, *, n0):
    half = attn_ref.shape[0] // 2
    spans = [(0, half), (half, 2 * half)]
    outs = []
    for r0, r1 in spans:
        a = attn_ref[r0:r1, :].astype(F32)
        ms = jnp.mean(a * a, axis=-1, keepdims=True)
        an = (a * lax.rsqrt(ms + EPS) * ag_ref[...]).astype(BF16)
        o = jnp.dot(an, wo_ref[:ATTN_WIDTH, :], preferred_element_type=F32)
        outs.append(o + jnp.dot(sgu_ref[r0:r1, :], wo_ref[ATTN_WIDTH:, :], preferred_element_type=F32))
    gate1 = mod_ref[2:3, :]
    shift2 = mod_ref[3:4, :]
    scale2 = mod_ref[4:5, :]
    first_group = pl.program_id(0) < n0
    for (r0, r1), o in zip(spans, outs):
        x1 = jnp.where(first_group, xa_ref[r0:r1, :], xb_ref[r0:r1, :]) + gate1 * o
        x1_ref[r0:r1, :] = x1
        ms2 = jnp.mean(x1 * x1, axis=-1, keepdims=True)
        h2 = (x1 * lax.rsqrt(ms2 + EPS) * g2_ref[...]) * (1.0 + scale2) + shift2
        _store_slabs(h2_ref, h2, r0)
        lgt_ref[:, r0:r1] = lax.dot_general(wrt_ref[...], h2.astype(BF16), NT_DIMS,
                                            preferred_element_type=F32)


def _outproj(attn, sgun, xa, xb, modp, ag, wo, g2, wrt, grp, tm):
    n, d = grp.n_tokens, xa.shape[1]
    bidx, _ = grp.tile_maps(tm)
    tok = lambda w: pl.BlockSpec((tm, w), lambda i: (i, 0))
    xa_spec, xb_spec, n0 = grp.split_specs(tm, d)
    return pl.pallas_call(
        functools.partial(_outproj_kernel, n0=n0),
        grid=(n // tm,),
        in_specs=[tok(ATTN_WIDTH), tok(A_WIDTH), xa_spec, xb_spec,
                  pl.BlockSpec((None, 8, d), lambda i: (bidx(i), 0, 0)),
                  _const_spec((1, ATTN_WIDTH)), _const_spec(wo.shape), _const_spec((1, d)),
                  _const_spec(wrt.shape)],
        out_specs=[tok(d), pl.BlockSpec((tm * _slab_rows(d), LANES), lambda i: (i, 0)),
                   pl.BlockSpec((N_EXPERTS, tm), lambda i: (0, i))],
        out_shape=[jax.ShapeDtypeStruct((n, d), F32),
                   jax.ShapeDtypeStruct((n * _slab_rows(d), LANES), U32),
                   jax.ShapeDtypeStruct((N_EXPERTS, n), F32)],
        compiler_params=_cparams(1),
        name="outproj",
    )(attn, sgun, xa, xb, modp, ag, wo, g2, wrt)


def _first_index(hit, iota, sentinel):
    return jnp.min(jnp.where(hit, iota, sentinel), axis=0, keepdims=True)


def _router_kernel(lg_ref, bias_ref, idx_ref, w_ref, rank_ref, cnt_ref, run_sc):
    i = pl.program_id(0)
    tt = lg_ref.shape[1]

    @pl.when(i == 0)
    def _init():
        run_sc[...] = jnp.zeros(run_sc.shape, F32)

    scores = jax.nn.sigmoid(lg_ref[...])
    biased = scores + bias_ref[...]
    neg = -jnp.inf
    iota_g = lax.broadcasted_iota(I32, (GROUP_SIZE, tt), 0).astype(F32)

    gs_rows = []
    for g in range(N_GROUPS):
        xg = biased[g * GROUP_SIZE:(g + 1) * GROUP_SIZE, :]
        m1 = jnp.max(xg, axis=0, keepdims=True)
        f1 = _first_index(xg == m1, iota_g, float(GROUP_SIZE))
        m2 = jnp.max(jnp.where(iota_g == f1, neg, xg), axis=0, keepdims=True)
        gs_rows.append(m1 + m2)
    gs = jnp.concatenate(gs_rows, axis=0)

    iota_n = lax.broadcasted_iota(I32, (N_GROUPS, tt), 0).astype(F32)
    gsel = jnp.zeros((N_GROUPS, tt), F32)
    cur = gs
    for _ in range(TOPK_GROUPS):
        m = jnp.max(cur, axis=0, keepdims=True)
        f = _first_index(cur == m, iota_n, float(N_GROUPS))
        hit = iota_n == f
        gsel = jnp.where(hit, 1.0, gsel)
        cur = jnp.where(hit, neg, cur)

    masked = jnp.concatenate(
        [jnp.where(gsel[g:g + 1, :] > 0.5, biased[g * GROUP_SIZE:(g + 1) * GROUP_SIZE, :], neg)
         for g in range(N_GROUPS)], axis=0)

    iota_e = lax.broadcasted_iota(I32, (N_EXPERTS, tt), 0).astype(F32)
    sel = jnp.zeros((N_EXPERTS, tt), F32)
    idx_rows, w_rows = [], []
    for _ in range(TOP_K):
        m = jnp.max(masked, axis=0, keepdims=True)
        f = _first_index(masked == m, iota_e, float(N_EXPERTS))
        hit = iota_e == f
        idx_rows.append(f)
        w_rows.append(jnp.sum(jnp.where(hit, scores, 0.0), axis=0, keepdims=True))
        sel = jnp.where(hit, 1.0, sel)
        masked = jnp.where(hit, neg, masked)
    idx = jnp.concatenate(idx_rows, axis=0)
    wk = jnp.concatenate(w_rows, axis=0)
    wk = wk / (jnp.sum(wk, axis=0, keepdims=True) + 1e-20) * ROUTED_SCALE

    r = lax.broadcasted_iota(I32, (tt, tt), 0)
    c = lax.broadcasted_iota(I32, (tt, tt), 1)
    upper = jnp.where(r < c, 1.0, 0.0).astype(BF16)
    prefix = jnp.dot(sel.astype(BF16), upper, preferred_element_type=F32) + run_sc[...]
    rank_rows = [jnp.sum(jnp.where(iota_e == idx[k:k + 1, :], prefix, 0.0), axis=0, keepdims=True)
                 for k in range(TOP_K)]
    run_sc[...] = run_sc[...] + jnp.sum(sel, axis=1, keepdims=True)

    idx_ref[...] = idx.astype(I32)
    w_ref[...] = wk
    rank_ref[...] = jnp.concatenate(rank_rows, axis=0).astype(I32)
    cnt_ref[...] = jnp.broadcast_to(run_sc[...], cnt_ref.shape).astype(I32)


def _router(lgt, bias, tt):
    n = lgt.shape[1]
    tokk = pl.BlockSpec((TOP_K, tt), lambda i: (0, i))
    return pl.pallas_call(
        _router_kernel,
        grid=(n // tt,),
        in_specs=[pl.BlockSpec((N_EXPERTS, tt), lambda i: (0, i)), _const_spec((N_EXPERTS, 1))],
        out_specs=[tokk, tokk, tokk, _const_spec((N_EXPERTS, 128))],
        out_shape=[jax.ShapeDtypeStruct((TOP_K, n), I32), jax.ShapeDtypeStruct((TOP_K, n), F32),
                   jax.ShapeDtypeStruct((TOP_K, n), I32), jax.ShapeDtypeStruct((N_EXPERTS, 128), I32)],
        scratch_shapes=[pltpu.VMEM((N_EXPERTS, 1), F32)],
        compiler_params=_cparams(1),
        name="router",
    )(lgt, bias)


def _dest_kernel(idx_ref, rank_ref, start_ref, o_ref):
    tt = idx_ref.shape[1]
    iota_e = lax.broadcasted_iota(I32, (N_EXPERTS, tt), 0)
    start = start_ref[...]
    rows = []
    for k in range(TOP_K):
        hit = iota_e == idx_ref[k:k + 1, :]
        rows.append(jnp.sum(jnp.where(hit, start, 0), axis=0, keepdims=True))
    o_ref[...] = jnp.concatenate(rows, axis=0) + rank_ref[...]


def _dest(idx, rank, pad_start, tt):
    n = idx.shape[1]
    tokk = pl.BlockSpec((TOP_K, tt), lambda i: (0, i))
    return pl.pallas_call(
        _dest_kernel,
        grid=(n // tt,),
        in_specs=[tokk, tokk, _const_spec((N_EXPERTS, 1))],
        out_specs=tokk,
        out_shape=jax.ShapeDtypeStruct((TOP_K, n), I32),
        compiler_params=_cparams(1),
        name="dest",
    )(idx, rank, pad_start.reshape(N_EXPERTS, 1))


def _slab(ref, tok, rows):
    return ref.at[pl.ds(pl.multiple_of(tok * rows, SUBLANES), rows)]


def _dispatch_kernel(cnt_s, start_s, pend_s, dest_ref, h_ref, xs_ref, zero_sc, sem, *, td, bm, n_blocks):
    i = pl.program_id(0)
    rows = zero_sc.shape[0] // bm

    def slab_copy(t, slot):
        return pltpu.make_async_copy(_slab(h_ref, t, rows), _slab(xs_ref, slot, rows), sem)

    def issue(t, c):
        for k in range(TOP_K):
            slab_copy(t, dest_ref[0, t * TOP_K + k]).start(priority=k % 2)
        return c

    lax.fori_loop(0, td, issue, 0)

    n_rows = td * TOP_K * rows
    pltpu.make_async_copy(xs_ref.at[pl.ds(0, n_rows)], xs_ref.at[pl.ds(0, n_rows)], sem).wait()

    @pl.when(i == pl.num_programs(0) - 1)
    def _pad():
        zero_sc[...] = jnp.zeros(zero_sc.shape, U32)

        def pad_copy(p):
            return pltpu.make_async_copy(zero_sc.at[pl.ds(0, rows)], _slab(xs_ref, p, rows), sem)

        def tail_copy(b):
            return pltpu.make_async_copy(zero_sc, _slab(xs_ref, b, bm * rows), sem)

        first_unused = pend_s[N_EXPERTS - 1] // bm

        def tail_go(b, c):
            tail_copy(b).start()
            return c

        lax.fori_loop(first_unused, n_blocks, tail_go, 0)

        def tail_done(b, c):
            tail_copy(b).wait()
            return c

        lax.fori_loop(first_unused, n_blocks, tail_done, 0)

        def per_expert(e, c):
            lo = start_s[e] + cnt_s[e]
            hi = pend_s[e]

            def go(p, c2):
                pad_copy(p).start()
                return c2

            lax.fori_loop(lo, hi, go, 0)

            def done(p, c2):
                pad_copy(p).wait()
                return c2

            lax.fori_loop(lo, hi, done, 0)
            return c

        lax.fori_loop(0, N_EXPERTS, per_expert, 0)


def _dispatch(counts, pad_start, pad_end, dest3, h2s, n, p_tok, td, bm):
    rows = h2s.shape[0] // n
    gs = pltpu.PrefetchScalarGridSpec(
        num_scalar_prefetch=3,
        grid=(n // td,),
        in_specs=[pl.BlockSpec((None, 1, TOP_K * td), lambda i, *_: (i, 0, 0), memory_space=pltpu.SMEM),
                  pl.BlockSpec((td * rows, LANES), lambda i, *_: (i, 0))],
        out_specs=pl.BlockSpec(memory_space=pl.ANY),
        scratch_shapes=[pltpu.VMEM((bm * rows, LANES), U32), pltpu.SemaphoreType.DMA(())])
    return pl.pallas_call(
        functools.partial(_dispatch_kernel, td=td, bm=bm, n_blocks=p_tok // bm),
        grid_spec=gs,
        out_shape=jax.ShapeDtypeStruct((p_tok * rows, LANES), U32),
        compiler_params=_cparams(1),
        name="dispatch",
    )(counts, pad_start, pad_end, dest3, h2s)


def _swiglu_slabs(x_ref, wgu_ref, wd_ref, o_ref, *, packed_out):
    de, d = wd_ref.shape
    rows = _slab_rows(d)
    half = x_ref.shape[0] // rows // 2
    ups = [jnp.dot(_load_slabs(x_ref, rows, t0, half), wgu_ref[...],
                   preferred_element_type=F32) for t0 in (0, half)]
    for t0, r in zip((0, half), ups):
        a = r[:, :de]
        act = (a * jax.nn.sigmoid(a) * r[:, de:]).astype(BF16)
        out = jnp.dot(act, wd_ref[...], preferred_element_type=F32)
        if packed_out:
            _store_slabs(o_ref, out, t0)
        else:
            o_ref[t0:t0 + half, :] = out


def _shared_kernel(h_ref, wgu_ref, wd_ref, o_ref):
    _swiglu_slabs(h_ref, wgu_ref, wd_ref, o_ref, packed_out=False)


def _shared(h2s, wgu, wd, tm):
    d = wd.shape[1]
    rows = _slab_rows(d)
    n = h2s.shape[0] // rows
    return pl.pallas_call(
        _shared_kernel,
        grid=(n // tm,),
        in_specs=[pl.BlockSpec((tm * rows, LANES), lambda i: (i, 0)),
                  _const_spec(wgu.shape), _const_spec(wd.shape)],
        out_specs=pl.BlockSpec((tm, d), lambda i: (i, 0)),
        out_shape=jax.ShapeDtypeStruct((n, d), F32),
        compiler_params=_cparams(1),
        name="shared",
    )(h2s, wgu, wd)


def _experts_kernel(be_s, nu_s, x_ref, wg_ref, wu_ref, wd_ref, o_ref, wgu_sc, wd_sc):
    i = pl.program_id(0)
    used = i < nu_s[0]
    prev = be_s[jnp.maximum(i - 1, 0)]
    fresh = jnp.logical_or(i == 0, be_s[i] != prev)
    de = wd_sc.shape[0]

    @pl.when(jnp.logical_and(used, fresh))
    def _cast():
        wgu_sc[:, :de] = wg_ref[...].astype(BF16)
        wgu_sc[:, de:] = wu_ref[...].astype(BF16)
        wd_sc[...] = wd_ref[...].astype(BF16)

    @pl.when(used)
    def _run():
        _swiglu_slabs(x_ref, wgu_sc, wd_sc, o_ref, packed_out=True)

    @pl.when(jnp.logical_not(used))
    def _unused():
        o_ref[...] = jnp.zeros(o_ref.shape, U32)


def _experts(block_exp, n_used, xs, w_gate, w_up, w_down, bm):
    d, de = w_gate.shape[1], w_gate.shape[2]
    rows = _slab_rows(d)
    p_tok = xs.shape[0] // rows
    n_blocks = p_tok // bm
    row = lambda i, be, nu: (jnp.minimum(i, nu[0] - 1), 0)
    wsel = lambda i, be, nu: (be[jnp.minimum(i, nu[0] - 1)], 0, 0)
    gs = pltpu.PrefetchScalarGridSpec(
        num_scalar_prefetch=2,
        grid=(n_blocks,),
        in_specs=[pl.BlockSpec((bm * rows, LANES), row),
                  pl.BlockSpec((None, d, de), wsel),
                  pl.BlockSpec((None, d, de), wsel),
                  pl.BlockSpec((None, de, d), wsel)],
        out_specs=pl.BlockSpec((bm * rows, LANES), lambda i, be, nu: (i, 0)),
        scratch_shapes=[pltpu.VMEM((d, 2 * de), BF16), pltpu.VMEM((de, d), BF16)])
    return pl.pallas_call(
        _experts_kernel,
        grid_spec=gs,
        out_shape=jax.ShapeDtypeStruct(xs.shape, U32),
        compiler_params=_cparams(1),
        name="experts",
    )(block_exp, n_used, xs, w_gate, w_up, w_down)


def _combine_kernel(dest_ref, nxt_ref, w_ref, x1_ref, sh_ref, mod_ref, ys_ref, oa_ref, ob_ref,
                    buf, lo_sc, hi_sc, sems, *, tc, n0):
    i = pl.program_id(0)
    rows = lo_sc.shape[0] // tc
    cur = i % 2

    def gather(d_ref, b):
        def issue(t, c):
            for k in range(TOP_K):
                pltpu.make_async_copy(_slab(ys_ref, d_ref[0, t * TOP_K + k], rows),
                                      _slab(buf.at[b, k], t, rows),
                                      sems.at[b]).start(priority=k % 2)
            return c

        lax.fori_loop(0, tc, issue, 0)

    @pl.when(i == 0)
    def _():
        gather(dest_ref, 0)

    @pl.when(i + 1 < pl.num_programs(0))
    def _():
        gather(nxt_ref, 1 - cur)

    pltpu.make_async_copy(buf.at[cur], buf.at[cur], sems.at[cur]).wait()

    def token_sum(t, c):
        r0 = pl.multiple_of(t * rows, SUBLANES)
        lo = hi = None
        for k in range(TOP_K):
            lo_k, hi_k = _unpack_bf16_pair(buf[cur, k, pl.ds(r0, rows), :])
            wk = w_ref[0, t * TOP_K + k]
            lo = wk * lo_k if lo is None else lo + wk * lo_k
            hi = wk * hi_k if hi is None else hi + wk * hi_k
        lo_sc[pl.ds(r0, rows), :] = lo
        hi_sc[pl.ds(r0, rows), :] = hi
        return c

    lax.fori_loop(0, tc, token_sum, 0, unroll=4)

    routed = jnp.concatenate([lo_sc[pl.ds(r, tc, stride=rows), :] for r in range(rows)]
                             + [hi_sc[pl.ds(r, tc, stride=rows), :] for r in range(rows)], axis=1)
    y = x1_ref[...] + mod_ref[5:6, :] * (sh_ref[...] + routed)

    @pl.when(pl.program_id(0) < n0)
    def _():
        oa_ref[...] = y

    @pl.when(pl.program_id(0) >= n0)
    def _():
        ob_ref[...] = y


def _combine(dest3, w3, x1, shared, modp, ys, grp, tc):
    n, d = x1.shape
    rows = _slab_rows(d)
    bidx, _ = grp.tile_maps(tc)
    tok = pl.BlockSpec((tc, d), lambda i: (i, 0))
    oa_spec, ob_spec, n0 = grp.split_specs(tc, d)
    (b0, s0), (b1, s1) = grp.groups
    return pl.pallas_call(
        functools.partial(_combine_kernel, tc=tc, n0=n0),
        grid=(n // tc,),
        in_specs=[pl.BlockSpec((None, 1, TOP_K * tc), lambda i: (i, 0, 0), memory_space=pltpu.SMEM),
                  pl.BlockSpec((None, 1, TOP_K * tc), lambda i: (jnp.minimum(i + 1, n // tc - 1), 0, 0),
                               memory_space=pltpu.SMEM),
                  pl.BlockSpec((None, 1, TOP_K * tc), lambda i: (i, 0, 0), memory_space=pltpu.SMEM),
                  tok, tok,
                  pl.BlockSpec((None, 8, d), lambda i: (bidx(i), 0, 0)),
                  pl.BlockSpec(memory_space=pl.ANY)],
        out_specs=[oa_spec, ob_spec],
        out_shape=[jax.ShapeDtypeStruct((b0 * s0, d), F32), jax.ShapeDtypeStruct((b1 * s1, d), F32)],
        scratch_shapes=[pltpu.VMEM((2, TOP_K, tc * rows, LANES), U32),
                        pltpu.VMEM((tc * rows, LANES), F32), pltpu.VMEM((tc * rows, LANES), F32),
                        pltpu.SemaphoreType.DMA((2,))],
        compiler_params=_cparams(1),
        name="combine",
    )(dest3, dest3, w3, x1, shared, modp, ys)


def _tok3(a, t):
    k, n = a.shape
    return a.reshape(k, n // t, t).transpose(1, 2, 0).reshape(n // t, 1, t * k)


def _layer(xa, xb, c, grp, norm1_g, w_mod, b_mod, w_in, q_norm_g, k_norm_g, sgu_ln_g, sgu_ln_b, w_s, b_s,
           attn_out_g, sgu_out_g, w_out, norm2_g, w_router, router_bias, w_gate, w_up, w_down,
           ws_gate, ws_up, ws_down):
    n, d = grp.n_tokens, xa.shape[1]
    nb = c.shape[0]
    s_min = min(s for _, s in grp.groups)
    s_max = max(s for _, s in grp.groups)

    c8 = jnp.pad(c, ((0, 8 - nb), (0, 0)))
    mod = _mod(c8, w_mod, b_mod)[:nb].reshape(nb, N_MOD, d)
    modp = jnp.pad(mod, ((0, 0), (0, 8 - N_MOD), (0, 0)))

    perm = np.concatenate([np.arange(0, HEAD_DIM, 2), np.arange(1, HEAD_DIM, 2)])
    c0, c1, c2, c3 = ATTN_WIDTH, ATTN_WIDTH + KV_WIDTH, ATTN_WIDTH + 2 * KV_WIDTH, ATTN_WIDTH + 2 * KV_WIDTH + A_WIDTH
    wq = w_in[:, :c0].reshape(d, N_Q_HEADS, HEAD_DIM)[:, :, perm].reshape(d, ATTN_WIDTH)
    wk = w_in[:, c0:c1].reshape(d, N_KV_HEADS, HEAD_DIM)[:, :, perm].reshape(d, KV_WIDTH)
    wqt = wq.T.astype(BF16)
    wkb = wk.astype(BF16)
    wvt = w_in[:, c1:c2].T.astype(BF16)
    wu = w_in[:, c2:c3].astype(BF16)
    wgv = w_in[:, c3:].astype(BF16)
    qg = q_norm_g[perm].reshape(HEAD_DIM, 1)
    kg = k_norm_g[perm].reshape(1, HEAD_DIM)

    cos2, sin2 = _rope_tables(s_max)
    scale = HEAD_DIM ** -0.5 * LOG2E
    tabs = (jnp.asarray((cos2 * scale).T, F32), jnp.asarray((sin2 * scale).T, F32),
            jnp.asarray(cos2, F32), jnp.asarray(sin2, F32))

    tm = _tile(s_min, 512)
    qt, k, vt, u, gv = _inproj(xa, xb, modp, norm1_g.reshape(1, d), wqt, wkb, wvt, wu, wgv, qg, kg, tabs,
                               grp, tm)

    k_bound2 = HEAD_DIM * jnp.max(jnp.square(k_norm_g)) * BOUND_SLACK
    q_bound = math.sqrt(HEAD_DIM) * scale * BOUND_SLACK * jnp.max(jnp.abs(q_norm_g))
    safe = (q_bound * jnp.sqrt(k_bound2) * BOUND_SLACK < SAFE_SCORE_BOUND).astype(I32).reshape(1)
    attn = _flash(qt, k, vt, jnp.full((8, 128), k_bound2, F32), safe, grp,
                  _tile(s_min, 1024), _tile(s_min, 1024))
    sgun = _sgu(u, gv, sgu_ln_g.reshape(1, A_WIDTH), sgu_ln_b.reshape(1, A_WIDTH), w_s.astype(BF16),
                b_s.T, sgu_out_g.reshape(1, A_WIDTH), _tile(s_min, 512))

    x1, h2, lgt = _outproj(attn, sgun, xa, xb, modp, attn_out_g.reshape(1, ATTN_WIDTH), w_out.astype(BF16),
                           norm2_g.reshape(1, d), w_router.T.astype(BF16), grp, _tile(s_min, 512))

    idx, wts, rank, cnt = _router(lgt, router_bias.reshape(N_EXPERTS, 1), _tile(s_min, 512))

    bm = 512
    counts = cnt[:, 0]
    padded = (counts + bm - 1) // bm * bm
    pad_end = jnp.cumsum(padded).astype(I32)
    pad_start = pad_end - padded
    n_blocks = n * TOP_K // bm + N_EXPERTS
    p_tok = n_blocks * bm
    n_used = (pad_end[-1] // bm).reshape(1).astype(I32)
    block_start = jnp.arange(n_blocks, dtype=I32) * bm
    block_exp = jnp.minimum(jnp.sum((pad_end[None, :] <= block_start[:, None]).astype(I32), axis=1),
                            N_EXPERTS - 1)

    dest = _dest(idx, rank, pad_start, _tile(s_min, 2048))

    td = _tile(s_min, 256)
    xs = _dispatch(counts, pad_start, pad_end, _tok3(dest, td), h2, n, p_tok, td, bm)
    shared = _shared(h2, jnp.concatenate([ws_gate, ws_up], axis=1).astype(BF16), ws_down.astype(BF16),
                     _tile(s_min, 512))
    ys = _experts(block_exp, n_used, xs, w_gate, w_up, w_down, bm)

    tc = _tile(s_min, 128)
    return _combine(_tok3(dest, tc), _tok3(wts, tc), x1, shared, modp, ys, grp, tc)


def kernel(x_prompt, x_sample, c_prompt, c_sample, norm1_g, w_mod, b_mod, w_in, q_norm_g, k_norm_g, sgu_ln_g, sgu_ln_b, w_s, b_s, attn_out_g, sgu_out_g, w_out, norm2_g, w_router, router_bias, w_gate, w_up, w_down, ws_gate, ws_up, ws_down):
    depth = norm1_g.shape[0]
    bp, sp, d = x_prompt.shape
    bs, ss, _ = x_sample.shape
    grp = _Groups([(bp, sp), (bs, ss)])
    xa, xb = x_prompt.reshape(bp * sp, d), x_sample.reshape(bs * ss, d)
    c = jnp.concatenate([c_prompt, c_sample], axis=0)
    for l in range(depth):
        xa, xb = _layer(xa, xb, c, grp, norm1_g[l], w_mod[l], b_mod[l], w_in[l], q_norm_g[l], k_norm_g[l],
                   sgu_ln_g[l], sgu_ln_b[l], w_s[l], b_s[l], attn_out_g[l], sgu_out_g[l], w_out[l],
                   norm2_g[l], w_router[l], router_bias[l], w_gate[l], w_up[l], w_down[l],
                   ws_gate[l], ws_up[l], ws_down[l])
    return (xa.reshape(bp, sp, d), xb.reshape(bs, ss, d))
```

```python
import functools
import math

import numpy as np
import jax
import jax.numpy as jnp
from jax import lax
from jax.experimental import pallas as pl
from jax.experimental.pallas import tpu as pltpu

F32 = jnp.float32
BF16 = jnp.bfloat16
I32 = jnp.int32

HEAD_DIM = 128
N_Q_HEADS = 8
N_KV_HEADS = 2
Q_PER_KV = N_Q_HEADS // N_KV_HEADS
ATTN_WIDTH = N_Q_HEADS * HEAD_DIM
KV_WIDTH = N_KV_HEADS * HEAD_DIM
ROPE_THETA = 10000.0
GRID_W = 64
N_A_HEADS = 8
A_HEAD_DIM = 128
A_WIDTH = N_A_HEADS * A_HEAD_DIM
CHUNK = 128
N_EXPERTS = 64
TOP_K = 8
N_GROUPS = 8
GROUP_SIZE = N_EXPERTS // N_GROUPS
TOPK_GROUPS = 4
ROUTED_SCALE = 2.5
N_MOD = 6
EPS = 1e-6

VMEM_LIMIT_BYTES = 56 * 1024 * 1024
NT_DIMS = (((1,), (1,)), ((), ()))


def _cparams(n_axes):
    return pltpu.CompilerParams(
        dimension_semantics=("arbitrary",) * n_axes, vmem_limit_bytes=VMEM_LIMIT_BYTES)


def _tile(n, pref):
    t = min(n, pref)
    while n % t:
        t //= 2
    return t


def _const_spec(shape):
    nd = len(shape)
    return pl.BlockSpec(shape, lambda *a: (0,) * nd, pipeline_mode=pl.Buffered(1))


def _mod_kernel(c_ref, w_ref, b_ref, o_ref):
    c = c_ref[...]
    a = (c * jax.nn.sigmoid(c)).astype(BF16)
    o_ref[...] = jnp.dot(a, w_ref[...].astype(BF16), preferred_element_type=F32) + b_ref[...]


def _mod(c8, w_mod, b_mod):
    d, n = w_mod.shape
    tn = _tile(n, 1024)
    return pl.pallas_call(
        _mod_kernel,
        grid=(n // tn,),
        in_specs=[_const_spec((8, d)),
                  pl.BlockSpec((d, tn), lambda j: (0, j)),
                  pl.BlockSpec((1, tn), lambda j: (0, j))],
        out_specs=pl.BlockSpec((8, tn), lambda j: (0, j)),
        out_shape=jax.ShapeDtypeStruct((8, n), F32),
        compiler_params=_cparams(1),
        name="mod",
    )(c8, w_mod, b_mod.reshape(1, n))


class _Groups:
    def __init__(self, groups):
        self.groups = groups
        self.n_tokens = sum(b * s for b, s in groups)

    def tile_maps(self, t):
        (b0, s0), (b1, s1) = self.groups
        n0 = b0 * s0 // t
        t0, t1 = s0 // t, s1 // t

        def bidx(i):
            return jnp.where(i < n0, i // t0, b0 + (i - n0) // t1)

        def pidx(i):
            return jnp.where(i < n0, i % t0, (i - n0) % t1)

        return bidx, pidx

    def split_specs(self, t, d):
        (b0, s0), _ = self.groups
        n0 = b0 * s0 // t
        return (pl.BlockSpec((t, d), lambda i, *_: (jnp.minimum(i, n0 - 1), 0)),
                pl.BlockSpec((t, d), lambda i, *_: (jnp.maximum(i - n0, 0), 0)), n0)


def _pick_group(i, n0, a_ref, b_ref):
    return jnp.where(i < n0, a_ref[...], b_ref[...])


LANES = 128
SUBLANES = 8
U32 = jnp.uint32


def _slab_rows(d):
    rows = d // (2 * LANES)
    assert d % (2 * LANES) == 0 and rows % SUBLANES == 0, d
    return rows


def _load_slabs(ref, rows, tok0, n_tok):
    pairs = [_unpack_bf16_pair(ref[pl.ds(tok0 * rows + r, n_tok, stride=rows), :]) for r in range(rows)]
    return jnp.concatenate([p[0] for p in pairs] + [p[1] for p in pairs], axis=1).astype(BF16)


def _pack_bf16_pair(lo, hi):
    def bf16_bits(v):
        return lax.bitcast_convert_type(v.astype(BF16).astype(F32), U32)

    return (bf16_bits(lo) >> 16) | bf16_bits(hi)


def _unpack_bf16_pair(w):
    return (lax.bitcast_convert_type(w << 16, F32),
            lax.bitcast_convert_type(w & jnp.uint32(0xFFFF0000), F32))


def _store_slabs(ref, x, tok0=0):
    n_tok, d = x.shape
    rows = _slab_rows(d)
    for r in range(rows):
        ref[pl.ds(tok0 * rows + r, n_tok, stride=rows), :] = _pack_bf16_pair(
            x[:, r * LANES:(r + 1) * LANES], x[:, (rows + r) * LANES:(rows + r + 1) * LANES])


def _rope_tables(seq_len):
    pos = np.arange(seq_len)
    row = (pos // GRID_W).astype(np.float64)
    col = (pos % GRID_W).astype(np.float64)
    n_pairs = HEAD_DIM // 4
    freqs = ROPE_THETA ** (-np.arange(n_pairs, dtype=np.float64) / n_pairs)
    ang = np.concatenate([row[:, None] * freqs, col[:, None] * freqs], axis=-1)
    c, s = np.cos(ang), np.sin(ang)
    cos2 = np.concatenate([c, c], axis=-1)
    sin2 = np.concatenate([-s, s], axis=-1)
    return cos2, sin2


def _inproj_kernel(xa_ref, xb_ref, mod_ref, g1_ref, wqt_ref, wk_ref, wvt_ref, wu_ref, wgv_ref,
                   qg_ref, kg_ref, cost_ref, sint_ref, cos_ref, sin_ref,
                   qt_ref, k_ref, vt_ref, u_ref, gv_ref, *, n0):
    x = _pick_group(pl.program_id(0), n0, xa_ref, xb_ref)
    shift1 = mod_ref[0:1, :]
    scale1 = mod_ref[1:2, :]
    ms = jnp.mean(x * x, axis=-1, keepdims=True)
    h = (x * lax.rsqrt(ms + EPS) * g1_ref[...]) * (1.0 + scale1) + shift1
    hb = h.astype(BF16)
    half = HEAD_DIM // 2

    qt = lax.dot_general(wqt_ref[...], hb, NT_DIMS, preferred_element_type=F32)
    cost = cost_ref[...]
    sint = sint_ref[...]
    qg = qg_ref[...]
    for hh in range(N_Q_HEADS):
        qh = qt[hh * HEAD_DIM:(hh + 1) * HEAD_DIM, :]
        msq = jnp.mean(qh * qh, axis=0, keepdims=True)
        qn = qh * lax.rsqrt(msq + EPS) * qg
        rot = jnp.concatenate([qn[half:, :], qn[:half, :]], axis=0)
        qt_ref[hh * HEAD_DIM:(hh + 1) * HEAD_DIM, :] = (qn * cost + rot * sint).astype(BF16)

    vt_ref[...] = lax.dot_general(wvt_ref[...], hb, NT_DIMS, preferred_element_type=F32).astype(BF16)

    kk = jnp.dot(hb, wk_ref[...], preferred_element_type=F32)
    cos = cos_ref[...]
    sin = sin_ref[...]
    kg = kg_ref[...]
    for j in range(N_KV_HEADS):
        kh = kk[:, j * HEAD_DIM:(j + 1) * HEAD_DIM]
        msk = jnp.mean(kh * kh, axis=-1, keepdims=True)
        kn = kh * lax.rsqrt(msk + EPS) * kg
        rot = jnp.concatenate([kn[:, half:], kn[:, :half]], axis=1)
        k_ref[:, j * HEAD_DIM:(j + 1) * HEAD_DIM] = (kn * cos + rot * sin).astype(BF16)

    u_ref[...] = jnp.dot(hb, wu_ref[...], preferred_element_type=F32).astype(BF16)
    gv_ref[...] = jnp.dot(hb, wgv_ref[...], preferred_element_type=F32).astype(BF16)


def _inproj(xa, xb, modp, g1, wqt, wk, wvt, wu, wgv, qg, kg, tabs, grp, tm):
    n, d = grp.n_tokens, xa.shape[1]
    bidx, pidx = grp.tile_maps(tm)
    cost, sint, cos, sin = tabs
    tok = lambda w: pl.BlockSpec((tm, w), lambda i: (i, 0))
    tokt = lambda w: pl.BlockSpec((w, tm), lambda i: (0, i))
    xa_spec, xb_spec, n0 = grp.split_specs(tm, d)
    return pl.pallas_call(
        functools.partial(_inproj_kernel, n0=n0),
        grid=(n // tm,),
        in_specs=[xa_spec, xb_spec,
                  pl.BlockSpec((None, 8, d), lambda i: (bidx(i), 0, 0)),
                  _const_spec((1, d)),
                  _const_spec(wqt.shape), _const_spec(wk.shape), _const_spec(wvt.shape),
                  _const_spec(wu.shape), _const_spec(wgv.shape),
                  _const_spec((HEAD_DIM, 1)), _const_spec((1, HEAD_DIM)),
                  pl.BlockSpec((HEAD_DIM, tm), lambda i: (0, pidx(i))),
                  pl.BlockSpec((HEAD_DIM, tm), lambda i: (0, pidx(i))),
                  pl.BlockSpec((tm, HEAD_DIM), lambda i: (pidx(i), 0)),
                  pl.BlockSpec((tm, HEAD_DIM), lambda i: (pidx(i), 0))],
        out_specs=[tokt(ATTN_WIDTH), tok(KV_WIDTH), tokt(KV_WIDTH), tok(A_WIDTH), tok(A_WIDTH)],
        out_shape=[jax.ShapeDtypeStruct((ATTN_WIDTH, n), BF16),
                   jax.ShapeDtypeStruct((n, KV_WIDTH), BF16),
                   jax.ShapeDtypeStruct((KV_WIDTH, n), BF16),
                   jax.ShapeDtypeStruct((n, A_WIDTH), BF16),
                   jax.ShapeDtypeStruct((n, A_WIDTH), BF16)],
        compiler_params=_cparams(1),
        name="inproj",
    )(xa, xb, modp, g1, wqt, wk, wvt, wu, wgv, qg, kg, cost, sint, cos, sin)


LOG2E = 1.4426950408889634
SAFE_SCORE_BOUND = 60.0
BOUND_SLACK = 1.01


def _flash_kernel(qt_s, kt_s, vp_s, fl_s, safe_s, q_ref, k_ref, v_ref, vp_ref, km_ref, o_ref,
                  qs_sc, m_sc, l_sc, acc_sc, p_sc, *, bq, hb):
    s_id = pl.program_id(1)
    flag = fl_s[s_id]
    safe = safe_s[0] != 0
    last = (flag & 2) != 0

    @pl.when((flag & 1) != 0)
    def _init():
        l_sc[...] = jnp.zeros(l_sc.shape, F32)
        acc_sc[...] = jnp.zeros(acc_sc.shape, F32)
        p_sc[...] = jnp.zeros(p_sc.shape, BF16)
        for hh in range(Q_PER_KV):
            qs_sc[:, hh * bq:(hh + 1) * bq] = q_ref[hh * HEAD_DIM:(hh + 1) * HEAD_DIM, :]
        qf = qs_sc[...].astype(F32)
        qn2 = jnp.sum(qf * qf, axis=0, keepdims=True)
        bound = jnp.sqrt(qn2 * km_ref[0:1, 0:1]) * BOUND_SLACK
        m_sc[...] = jnp.where(safe, bound, -jnp.inf)

    @pl.when(safe)
    def _fast():
        q = qs_sc[...]
        m = m_sc[...]
        sa = jnp.dot(k_ref[:hb, :], q, preferred_element_type=F32)
        acc = acc_sc[...] + jnp.dot(vp_ref[...], p_sc[...], preferred_element_type=F32)
        pa = jnp.exp2(sa - m)
        la = jnp.sum(pa, axis=0, keepdims=True)
        sb = jnp.dot(k_ref[hb:, :], q, preferred_element_type=F32)
        acc_sc[...] = acc + jnp.dot(v_ref[:, :hb], pa.astype(BF16), preferred_element_type=F32)
        pb = jnp.exp2(sb - m)
        l_sc[...] += la + jnp.sum(pb, axis=0, keepdims=True)
        p_sc[...] = pb.astype(BF16)

    @pl.when(jnp.logical_and(safe, last))
    def _flush():
        acc_sc[...] += jnp.dot(v_ref[:, hb:], p_sc[...], preferred_element_type=F32)

    @pl.when(jnp.logical_not(safe))
    def _online():
        s = jnp.dot(k_ref[...], qs_sc[...], preferred_element_type=F32)
        m_prev = m_sc[...]
        m_new = jnp.maximum(m_prev, jnp.max(s, axis=0, keepdims=True))
        alpha = jnp.exp2(m_prev - m_new)
        p = jnp.exp2(s - m_new)
        l_sc[...] = alpha * l_sc[...] + jnp.sum(p, axis=0, keepdims=True)
        acc_sc[...] = alpha * acc_sc[...] + jnp.dot(v_ref[...], p.astype(BF16), preferred_element_type=F32)
        m_sc[...] = m_new

    @pl.when((flag & 2) != 0)
    def _fin():
        o = acc_sc[...] / l_sc[...]
        for hh in range(Q_PER_KV):
            o_ref[:, hh * HEAD_DIM:(hh + 1) * HEAD_DIM] = o[:, hh * bq:(hh + 1) * bq].T.astype(BF16)


def _flash_schedule(grp, bq, bkv):
    qt, kt, vp, fl = [], [], [], []
    off = 0
    for b, s in grp.groups:
        for bi in range(b):
            base = off + bi * s
            nk = s // bkv
            for qi in range(s // bq):
                for ki in range(nk):
                    blk = base // bkv + ki
                    qt.append(base // bq + qi)
                    kt.append(blk)
                    vp.append(2 * (blk - 1 if ki else blk) + 1)
                    fl.append((1 if ki == 0 else 0) | (2 if ki == nk - 1 else 0))
        off += b * s
    return tuple(np.asarray(a, np.int32) for a in (qt, kt, vp, fl))


def _flash(qt, k, vt, kmax2, safe, grp, bq, bkv):
    n = k.shape[0]
    hb = bkv // 2
    qt_a, kt_a, vp_a, fl_a = _flash_schedule(grp, bq, bkv)
    n_steps = len(qt_a)
    w = Q_PER_KV * HEAD_DIM
    gs = pltpu.PrefetchScalarGridSpec(
        num_scalar_prefetch=5,
        grid=(N_KV_HEADS, n_steps),
        in_specs=[pl.BlockSpec((w, bq), lambda h, s, qa, ka, va, fa, sa: (h, qa[s])),
                  pl.BlockSpec((bkv, HEAD_DIM), lambda h, s, qa, ka, va, fa, sa: (ka[s], h)),
                  pl.BlockSpec((HEAD_DIM, bkv), lambda h, s, qa, ka, va, fa, sa: (h, ka[s])),
                  pl.BlockSpec((HEAD_DIM, hb), lambda h, s, qa, ka, va, fa, sa: (h, va[s])),
                  pl.BlockSpec((8, 128), lambda h, s, qa, ka, va, fa, sa: (0, 0))],
        out_specs=pl.BlockSpec((bq, w), lambda h, s, qa, ka, va, fa, sa: (qa[s], h)),
        scratch_shapes=[pltpu.VMEM((HEAD_DIM, Q_PER_KV * bq), BF16),
                        pltpu.VMEM((1, Q_PER_KV * bq), F32),
                        pltpu.VMEM((1, Q_PER_KV * bq), F32),
                        pltpu.VMEM((HEAD_DIM, Q_PER_KV * bq), F32),
                        pltpu.VMEM((hb, Q_PER_KV * bq), BF16)])
    return pl.pallas_call(
        functools.partial(_flash_kernel, bq=bq, hb=hb),
        grid_spec=gs,
        out_shape=jax.ShapeDtypeStruct((n, ATTN_WIDTH), BF16),
        compiler_params=_cparams(2),
        name="flash",
    )(jnp.asarray(qt_a), jnp.asarray(kt_a), jnp.asarray(vp_a), jnp.asarray(fl_a), safe, qt, k, vt, vt, kmax2)


def _sgu_kernel(u_ref, gv_ref, lng_ref, lnb_ref, ws_ref, bst_ref, og_ref, o_ref, *, n_chunks):
    v = gv_ref[...].astype(F32)
    mean = jnp.mean(v, axis=-1, keepdims=True)
    vc = v - mean
    var = jnp.mean(vc * vc, axis=-1, keepdims=True)
    vn = (vc * lax.rsqrt(var + EPS) * lng_ref[...] + lnb_ref[...]).astype(BF16)
    u = u_ref[...].astype(F32)
    heads = []
    for hh in range(N_A_HEADS):
        cols = slice(hh * A_HEAD_DIM, (hh + 1) * A_HEAD_DIM)
        rhs = jnp.concatenate([vn[c * CHUNK:(c + 1) * CHUNK, cols] for c in range(n_chunks)], axis=1)
        mixed = jnp.dot(ws_ref[hh], rhs, preferred_element_type=F32) + bst_ref[:, hh:hh + 1]
        heads.append(jnp.concatenate(
            [mixed[:, c * A_HEAD_DIM:(c + 1) * A_HEAD_DIM] for c in range(n_chunks)], axis=0))
    sg = u * jnp.concatenate(heads, axis=1)
    ms = jnp.mean(sg * sg, axis=-1, keepdims=True)
    o_ref[...] = (sg * lax.rsqrt(ms + EPS) * og_ref[...]).astype(BF16)


def _sgu(u, gv, lng, lnb, ws, bst, og, tm):
    n = u.shape[0]
    tok = pl.BlockSpec((tm, A_WIDTH), lambda i: (i, 0))
    return pl.pallas_call(
        functools.partial(_sgu_kernel, n_chunks=tm // CHUNK),
        grid=(n // tm,),
        in_specs=[tok, tok, _const_spec((1, A_WIDTH)), _const_spec((1, A_WIDTH)),
                  _const_spec(ws.shape), _const_spec(bst.shape), _const_spec((1, A_WIDTH))],
        out_specs=tok,
        out_shape=jax.ShapeDtypeStruct((n, A_WIDTH), BF16),
        compiler_params=_cparams(1),
        name="sgu",
    )(u, gv, lng, lnb, ws, bst, og)


def _outproj_kernel(attn_ref, sgu_ref, xa_ref, xb_ref, mod_ref, ag_ref, wo_ref, g2_ref, wrt_ref,
                    x1_ref, h2_ref, lgt_ref, *, n0):
    half = attn_ref.shape[0] // 2
    spans = [(0, half), (half, 2 * half)]
    outs = []
    for r0, r1 in spans:
        a = attn_ref[r0:r1, :].astype(F32)
        ms = jnp.mean(a * a, axis=-1, keepdims=True)
        an = (a * lax.rsqrt(ms + EPS) * ag_ref[...]).astype(BF16)
        o = jnp.dot(an, wo_ref[:ATTN_WIDTH, :], preferred_element_type=F32)
        outs.append(o + jnp.dot(sgu_ref[r0:r1, :], wo_ref[ATTN_WIDTH:, :], preferred_element_type=F32))
    gate1 = mod_ref[2:3, :]
    shift2 = mod_ref[3:4, :]
    scale2 = mod_ref[4:5, :]
    first_group = pl.program_id(0) < n0
    for (r0, r1), o in zip(spans, outs):
        x1 = jnp.where(first_group, xa_ref[r0:r1, :], xb_ref[r0:r1, :]) + gate1 * o
        x1_ref[r0:r1, :] = x1
        ms2 = jnp.mean(x1 * x1, axis=-1, keepdims=True)
        h2 = (x1 * lax.rsqrt(ms2 + EPS) * g2_ref[...]) * (1.0 + scale2) + shift2
        _store_slabs(h2_ref, h2, r0)
        lgt_ref[:, r0:r1] = lax.dot_general(wrt_ref[...], h2.astype(BF16), NT_DIMS,
                                            preferred_element_type=F32)


def _outproj(attn, sgun, xa, xb, modp, ag, wo, g2, wrt, grp, tm):
    n, d = grp.n_tokens, xa.shape[1]
    bidx, _ = grp.tile_maps(tm)
    tok = lambda w: pl.BlockSpec((tm, w), lambda i: (i, 0))
    xa_spec, xb_spec, n0 = grp.split_specs(tm, d)
    return pl.pallas_call(
        functools.partial(_outproj_kernel, n0=n0),
        grid=(n // tm,),
        in_specs=[tok(ATTN_WIDTH), tok(A_WIDTH), xa_spec, xb_spec,
                  pl.BlockSpec((None, 8, d), lambda i: (bidx(i), 0, 0)),
                  _const_spec((1, ATTN_WIDTH)), _const_spec(wo.shape), _const_spec((1, d)),
                  _const_spec(wrt.shape)],
        out_specs=[tok(d), pl.BlockSpec((tm * _slab_rows(d), LANES), lambda i: (i, 0)),
                   pl.BlockSpec((N_EXPERTS, tm), lambda i: (0, i))],
        out_shape=[jax.ShapeDtypeStruct((n, d), F32),
                   jax.ShapeDtypeStruct((n * _slab_rows(d), LANES), U32),
                   jax.ShapeDtypeStruct((N_EXPERTS, n), F32)],
        compiler_params=_cparams(1),
        name="outproj",
    )(attn, sgun, xa, xb, modp, ag, wo, g2, wrt)


def _first_index(hit, iota, sentinel):
    return jnp.min(jnp.where(hit, iota, sentinel), axis=0, keepdims=True)


def _router_kernel(lg_ref, bias_ref, idx_ref, w_ref, rank_ref, cnt_ref, run_sc):
    i = pl.program_id(0)
    tt = lg_ref.shape[1]

    @pl.when(i == 0)
    def _init():
        run_sc[...] = jnp.zeros(run_sc.shape, F32)

    scores = jax.nn.sigmoid(lg_ref[...])
    biased = scores + bias_ref[...]
    neg = -jnp.inf
    iota_g = lax.broadcasted_iota(I32, (GROUP_SIZE, tt), 0).astype(F32)

    gs_rows = []
    for g in range(N_GROUPS):
        xg = biased[g * GROUP_SIZE:(g + 1) * GROUP_SIZE, :]
        m1 = jnp.max(xg, axis=0, keepdims=True)
        f1 = _first_index(xg == m1, iota_g, float(GROUP_SIZE))
        m2 = jnp.max(jnp.where(iota_g == f1, neg, xg), axis=0, keepdims=True)
        gs_rows.append(m1 + m2)
    gs = jnp.concatenate(gs_rows, axis=0)

    iota_n = lax.broadcasted_iota(I32, (N_GROUPS, tt), 0).astype(F32)
    gsel = jnp.zeros((N_GROUPS, tt), F32)
    cur = gs
    for _ in range(TOPK_GROUPS):
        m = jnp.max(cur, axis=0, keepdims=True)
        f = _first_index(cur == m, iota_n, float(N_GROUPS))
        hit = iota_n == f
        gsel = jnp.where(hit, 1.0, gsel)
        cur = jnp.where(hit, neg, cur)

    masked = jnp.concatenate(
        [jnp.where(gsel[g:g + 1, :] > 0.5, biased[g * GROUP_SIZE:(g + 1) * GROUP_SIZE, :], neg)
         for g in range(N_GROUPS)], axis=0)

    iota_e = lax.broadcasted_iota(I32, (N_EXPERTS, tt), 0).astype(F32)
    sel = jnp.zeros((N_EXPERTS, tt), F32)
    idx_rows, w_rows = [], []
    for _ in range(TOP_K):
        m = jnp.max(masked, axis=0, keepdims=True)
        f = _first_index(masked == m, iota_e, float(N_EXPERTS))
        hit = iota_e == f
        idx_rows.append(f)
        w_rows.append(jnp.sum(jnp.where(hit, scores, 0.0), axis=0, keepdims=True))
        sel = jnp.where(hit, 1.0, sel)
        masked = jnp.where(hit, neg, masked)
    idx = jnp.concatenate(idx_rows, axis=0)
    wk = jnp.concatenate(w_rows, axis=0)
    wk = wk / (jnp.sum(wk, axis=0, keepdims=True) + 1e-20) * ROUTED_SCALE

    r = lax.broadcasted_iota(I32, (tt, tt), 0)
    c = lax.broadcasted_iota(I32, (tt, tt), 1)
    upper = jnp.where(r < c, 1.0, 0.0).astype(BF16)
    prefix = jnp.dot(sel.astype(BF16), upper, preferred_element_type=F32) + run_sc[...]
    rank_rows = [jnp.sum(jnp.where(iota_e == idx[k:k + 1, :], prefix, 0.0), axis=0, keepdims=True)
                 for k in range(TOP_K)]
    run_sc[...] = run_sc[...] + jnp.sum(sel, axis=1, keepdims=True)

    idx_ref[...] = idx.astype(I32)
    w_ref[...] = wk
    rank_ref[...] = jnp.concatenate(rank_rows, axis=0).astype(I32)
    cnt_ref[...] = jnp.broadcast_to(run_sc[...], cnt_ref.shape).astype(I32)


def _router(lgt, bias, tt):
    n = lgt.shape[1]
    tokk = pl.BlockSpec((TOP_K, tt), lambda i: (0, i))
    return pl.pallas_call(
        _router_kernel,
        grid=(n // tt,),
        in_specs=[pl.BlockSpec((N_EXPERTS, tt), lambda i: (0, i)), _const_spec((N_EXPERTS, 1))],
        out_specs=[tokk, tokk, tokk, _const_spec((N_EXPERTS, 128))],
        out_shape=[jax.ShapeDtypeStruct((TOP_K, n), I32), jax.ShapeDtypeStruct((TOP_K, n), F32),
                   jax.ShapeDtypeStruct((TOP_K, n), I32), jax.ShapeDtypeStruct((N_EXPERTS, 128), I32)],
        scratch_shapes=[pltpu.VMEM((N_EXPERTS, 1), F32)],
        compiler_params=_cparams(1),
        name="router",
    )(lgt, bias)


def _dest_kernel(idx_ref, rank_ref, start_ref, o_ref):
    tt = idx_ref.shape[1]
    iota_e = lax.broadcasted_iota(I32, (N_EXPERTS, tt), 0)
    start = start_ref[...]
    rows = []
    for k in range(TOP_K):
        hit = iota_e == idx_ref[k:k + 1, :]
        rows.append(jnp.sum(jnp.where(hit, start, 0), axis=0, keepdims=True))
    o_ref[...] = jnp.concatenate(rows, axis=0) + rank_ref[...]


def _dest(idx, rank, pad_start, tt):
    n = idx.shape[1]
    tokk = pl.BlockSpec((TOP_K, tt), lambda i: (0, i))
    return pl.pallas_call(
        _dest_kernel,
        grid=(n // tt,),
        in_specs=[tokk, tokk, _const_spec((N_EXPERTS, 1))],
        out_specs=tokk,
        out_shape=jax.ShapeDtypeStruct((TOP_K, n), I32),
        compiler_params=_cparams(1),
        name="dest",
    )(idx, rank, pad_start.reshape(N_EXPERTS, 1))


def _slab(ref, tok, rows):
    return ref.at[pl.ds(pl.multiple_of(tok * rows, SUBLANES), rows)]


DISPATCH_RING = 3


def _dispatch_kernel(cnt_s, start_s, pend_s, dest_ref, h_hbm, xs_ref, hbuf, zero_sc, in_sems, out_sems, sem,
                     *, td, bm, n_blocks):
    i = pl.program_id(0)
    n_steps = pl.num_programs(0)
    rows = zero_sc.shape[0] // bm
    t_rows = td * rows
    slot = i % DISPATCH_RING
    nxt = (i + 1) % DISPATCH_RING

    def load(tile, s):
        return pltpu.make_async_copy(h_hbm.at[pl.ds(pl.multiple_of(tile * t_rows, SUBLANES), t_rows)],
                                     hbuf.at[s], in_sems.at[s])

    def drain(s):
        n_rows = TOP_K * t_rows
        pltpu.make_async_copy(xs_ref.at[pl.ds(0, n_rows)], xs_ref.at[pl.ds(0, n_rows)], out_sems.at[s]).wait()

    @pl.when(i == 0)
    def _():
        load(0, 0).start()

    @pl.when(i + 1 < n_steps)
    def _():
        @pl.when(i + 1 >= DISPATCH_RING)
        def _():
            drain(nxt)

        load(i + 1, nxt).start()

    load(i, slot).wait()

    def issue(t, c):
        for k in range(TOP_K):
            pltpu.make_async_copy(_slab(hbuf.at[slot], t, rows), _slab(xs_ref, dest_ref[0, t * TOP_K + k], rows),
                                  out_sems.at[slot]).start(priority=k % 2)
        return c

    lax.fori_loop(0, td, issue, 0)

    @pl.when(i == n_steps - 1)
    def _pad():
        for back in range(DISPATCH_RING):
            @pl.when(i - back >= 0)
            def _():
                drain((i - back) % DISPATCH_RING)

        zero_sc[...] = jnp.zeros(zero_sc.shape, U32)

        def pad_copy(p):
            return pltpu.make_async_copy(zero_sc.at[pl.ds(0, rows)], _slab(xs_ref, p, rows), sem)

        def tail_copy(b):
            return pltpu.make_async_copy(zero_sc, _slab(xs_ref, b, bm * rows), sem)

        first_unused = pend_s[N_EXPERTS - 1] // bm

        def tail_go(b, c):
            tail_copy(b).start()
            return c

        lax.fori_loop(first_unused, n_blocks, tail_go, 0)

        def tail_done(b, c):
            tail_copy(b).wait()
            return c

        lax.fori_loop(first_unused, n_blocks, tail_done, 0)

        def per_expert(e, c):
            lo = start_s[e] + cnt_s[e]
            hi = pend_s[e]

            def go(p, c2):
                pad_copy(p).start()
                return c2

            lax.fori_loop(lo, hi, go, 0)

            def done(p, c2):
                pad_copy(p).wait()
                return c2

            lax.fori_loop(lo, hi, done, 0)
            return c

        lax.fori_loop(0, N_EXPERTS, per_expert, 0)


def _dispatch(counts, pad_start, pad_end, dest3, h2s, n, p_tok, td, bm):
    rows = h2s.shape[0] // n
    gs = pltpu.PrefetchScalarGridSpec(
        num_scalar_prefetch=3,
        grid=(n // td,),
        in_specs=[pl.BlockSpec((None, 1, TOP_K * td), lambda i, *_: (i, 0, 0), memory_space=pltpu.SMEM),
                  pl.BlockSpec(memory_space=pl.ANY)],
        out_specs=pl.BlockSpec(memory_space=pl.ANY),
        scratch_shapes=[pltpu.VMEM((DISPATCH_RING, td * rows, LANES), U32),
                        pltpu.VMEM((bm * rows, LANES), U32),
                        pltpu.SemaphoreType.DMA((DISPATCH_RING,)), pltpu.SemaphoreType.DMA((DISPATCH_RING,)),
                        pltpu.SemaphoreType.DMA(())])
    return pl.pallas_call(
        functools.partial(_dispatch_kernel, td=td, bm=bm, n_blocks=p_tok // bm),
        grid_spec=gs,
        out_shape=jax.ShapeDtypeStruct((p_tok * rows, LANES), U32),
        compiler_params=_cparams(1),
        name="dispatch",
    )(counts, pad_start, pad_end, dest3, h2s)


def _swiglu_slabs(x_ref, wgu_ref, wd_ref, o_ref, *, packed_out):
    de, d = wd_ref.shape
    rows = _slab_rows(d)
    half = x_ref.shape[0] // rows // 2
    ups = [jnp.dot(_load_slabs(x_ref, rows, t0, half), wgu_ref[...],
                   preferred_element_type=F32) for t0 in (0, half)]
    for t0, r in zip((0, half), ups):
        a = r[:, :de]
        act = (a * jax.nn.sigmoid(a) * r[:, de:]).astype(BF16)
        out = jnp.dot(act, wd_ref[...], preferred_element_type=F32)
        if packed_out:
            _store_slabs(o_ref, out, t0)
        else:
            o_ref[t0:t0 + half, :] = out


def _shared_kernel(h_ref, wgu_ref, wd_ref, o_ref):
    _swiglu_slabs(h_ref, wgu_ref, wd_ref, o_ref, packed_out=False)


def _shared(h2s, wgu, wd, tm):
    d = wd.shape[1]
    rows = _slab_rows(d)
    n = h2s.shape[0] // rows
    return pl.pallas_call(
        _shared_kernel,
        grid=(n // tm,),
        in_specs=[pl.BlockSpec((tm * rows, LANES), lambda i: (i, 0)),
                  _const_spec(wgu.shape), _const_spec(wd.shape)],
        out_specs=pl.BlockSpec((tm, d), lambda i: (i, 0)),
        out_shape=jax.ShapeDtypeStruct((n, d), F32),
        compiler_params=_cparams(1),
        name="shared",
    )(h2s, wgu, wd)


def _experts_kernel(be_s, nu_s, x_ref, wg_ref, wu_ref, wd_ref, o_ref, wgu_sc, wd_sc):
    i = pl.program_id(0)
    used = i < nu_s[0]
    prev = be_s[jnp.maximum(i - 1, 0)]
    fresh = jnp.logical_or(i == 0, be_s[i] != prev)
    de = wd_sc.shape[0]

    @pl.when(jnp.logical_and(used, fresh))
    def _cast():
        wgu_sc[:, :de] = wg_ref[...].astype(BF16)
        wgu_sc[:, de:] = wu_ref[...].astype(BF16)
        wd_sc[...] = wd_ref[...].astype(BF16)

    @pl.when(used)
    def _run():
        _swiglu_slabs(x_ref, wgu_sc, wd_sc, o_ref, packed_out=True)

    @pl.when(jnp.logical_not(used))
    def _unused():
        o_ref[...] = jnp.zeros(o_ref.shape, U32)


def _experts(block_exp, n_used, xs, w_gate, w_up, w_down, bm):
    d, de = w_gate.shape[1], w_gate.shape[2]
    rows = _slab_rows(d)
    p_tok = xs.shape[0] // rows
    n_blocks = p_tok // bm
    row = lambda i, be, nu: (jnp.minimum(i, nu[0] - 1), 0)
    wsel = lambda i, be, nu: (be[jnp.minimum(i, nu[0] - 1)], 0, 0)
    gs = pltpu.PrefetchScalarGridSpec(
        num_scalar_prefetch=2,
        grid=(n_blocks,),
        in_specs=[pl.BlockSpec((bm * rows, LANES), row),
                  pl.BlockSpec((None, d, de), wsel),
                  pl.BlockSpec((None, d, de), wsel),
                  pl.BlockSpec((None, de, d), wsel)],
        out_specs=pl.BlockSpec((bm * rows, LANES), lambda i, be, nu: (i, 0)),
        scratch_shapes=[pltpu.VMEM((d, 2 * de), BF16), pltpu.VMEM((de, d), BF16)])
    return pl.pallas_call(
        _experts_kernel,
        grid_spec=gs,
        out_shape=jax.ShapeDtypeStruct(xs.shape, U32),
        compiler_params=_cparams(1),
        name="experts",
    )(block_exp, n_used, xs, w_gate, w_up, w_down)


def _combine_kernel(dest_ref, nxt_ref, w_ref, x1_ref, sh_ref, mod_ref, ys_ref, oa_ref, ob_ref,
                    buf, lo_sc, hi_sc, sems, *, tc, n0):
    i = pl.program_id(0)
    rows = lo_sc.shape[0] // tc
    cur = i % 2

    def gather(d_ref, b):
        def issue(t, c):
            for k in range(TOP_K):
                pltpu.make_async_copy(_slab(ys_ref, d_ref[0, t * TOP_K + k], rows),
                                      _slab(buf.at[b, k], t, rows),
                                      sems.at[b]).start(priority=k % 2)
            return c

        lax.fori_loop(0, tc, issue, 0)

    @pl.when(i == 0)
    def _():
        gather(dest_ref, 0)

    @pl.when(i + 1 < pl.num_programs(0))
    def _():
        gather(nxt_ref, 1 - cur)

    pltpu.make_async_copy(buf.at[cur], buf.at[cur], sems.at[cur]).wait()

    def token_sum(t, c):
        r0 = pl.multiple_of(t * rows, SUBLANES)
        lo = hi = None
        for k in range(TOP_K):
            lo_k, hi_k = _unpack_bf16_pair(buf[cur, k, pl.ds(r0, rows), :])
            wk = w_ref[0, t * TOP_K + k]
            lo = wk * lo_k if lo is None else lo + wk * lo_k
            hi = wk * hi_k if hi is None else hi + wk * hi_k
        lo_sc[pl.ds(r0, rows), :] = lo
        hi_sc[pl.ds(r0, rows), :] = hi
        return c

    lax.fori_loop(0, tc, token_sum, 0, unroll=4)

    routed = jnp.concatenate([lo_sc[pl.ds(r, tc, stride=rows), :] for r in range(rows)]
                             + [hi_sc[pl.ds(r, tc, stride=rows), :] for r in range(rows)], axis=1)
    y = x1_ref[...] + mod_ref[5:6, :] * (sh_ref[...] + routed)

    @pl.when(pl.program_id(0) < n0)
    def _():
        oa_ref[...] = y

    @pl.when(pl.program_id(0) >= n0)
    def _():
        ob_ref[...] = y


def _combine(dest3, w3, x1, shared, modp, ys, grp, tc):
    n, d = x1.shape
    rows = _slab_rows(d)
    bidx, _ = grp.tile_maps(tc)
    tok = pl.BlockSpec((tc, d), lambda i: (i, 0))
    oa_spec, ob_spec, n0 = grp.split_specs(tc, d)
    (b0, s0), (b1, s1) = grp.groups
    return pl.pallas_call(
        functools.partial(_combine_kernel, tc=tc, n0=n0),
        grid=(n // tc,),
        in_specs=[pl.BlockSpec((None, 1, TOP_K * tc), lambda i: (i, 0, 0), memory_space=pltpu.SMEM),
                  pl.BlockSpec((None, 1, TOP_K * tc), lambda i: (jnp.minimum(i + 1, n // tc - 1), 0, 0),
                               memory_space=pltpu.SMEM),
                  pl.BlockSpec((None, 1, TOP_K * tc), lambda i: (i, 0, 0), memory_space=pltpu.SMEM),
                  tok, tok,
                  pl.BlockSpec((None, 8, d), lambda i: (bidx(i), 0, 0)),
                  pl.BlockSpec(memory_space=pl.ANY)],
        out_specs=[oa_spec, ob_spec],
        out_shape=[jax.ShapeDtypeStruct((b0 * s0, d), F32), jax.ShapeDtypeStruct((b1 * s1, d), F32)],
        scratch_shapes=[pltpu.VMEM((2, TOP_K, tc * rows, LANES), U32),
                        pltpu.VMEM((tc * rows, LANES), F32), pltpu.VMEM((tc * rows, LANES), F32),
                        pltpu.SemaphoreType.DMA((2,))],
        compiler_params=_cparams(1),
        name="combine",
    )(dest3, dest3, w3, x1, shared, modp, ys)


def _tok3(a, t):
    k, n = a.shape
    return a.reshape(k, n // t, t).transpose(1, 2, 0).reshape(n // t, 1, t * k)


def _layer(xa, xb, c, grp, norm1_g, w_mod, b_mod, w_in, q_norm_g, k_norm_g, sgu_ln_g, sgu_ln_b, w_s, b_s,
           attn_out_g, sgu_out_g, w_out, norm2_g, w_router, router_bias, w_gate, w_up, w_down,
           ws_gate, ws_up, ws_down):
    n, d = grp.n_tokens, xa.shape[1]
    nb = c.shape[0]
    s_min = min(s for _, s in grp.groups)
    s_max = max(s for _, s in grp.groups)

    c8 = jnp.pad(c, ((0, 8 - nb), (0, 0)))
    mod = _mod(c8, w_mod, b_mod)[:nb].reshape(nb, N_MOD, d)
    modp = jnp.pad(mod, ((0, 0), (0, 8 - N_MOD), (0, 0)))

    perm = np.concatenate([np.arange(0, HEAD_DIM, 2), np.arange(1, HEAD_DIM, 2)])
    c0, c1, c2, c3 = ATTN_WIDTH, ATTN_WIDTH + KV_WIDTH, ATTN_WIDTH + 2 * KV_WIDTH, ATTN_WIDTH + 2 * KV_WIDTH + A_WIDTH
    wq = w_in[:, :c0].reshape(d, N_Q_HEADS, HEAD_DIM)[:, :, perm].reshape(d, ATTN_WIDTH)
    wk = w_in[:, c0:c1].reshape(d, N_KV_HEADS, HEAD_DIM)[:, :, perm].reshape(d, KV_WIDTH)
    wqt = wq.T.astype(BF16)
    wkb = wk.astype(BF16)
    wvt = w_in[:, c1:c2].T.astype(BF16)
    wu = w_in[:, c2:c3].astype(BF16)
    wgv = w_in[:, c3:].astype(BF16)
    qg = q_norm_g[perm].reshape(HEAD_DIM, 1)
    kg = k_norm_g[perm].reshape(1, HEAD_DIM)

    cos2, sin2 = _rope_tables(s_max)
    scale = HEAD_DIM ** -0.5 * LOG2E
    tabs = (jnp.asarray((cos2 * scale).T, F32), jnp.asarray((sin2 * scale).T, F32),
            jnp.asarray(cos2, F32), jnp.asarray(sin2, F32))

    tm = _tile(s_min, 512)
    qt, k, vt, u, gv = _inproj(xa, xb, modp, norm1_g.reshape(1, d), wqt, wkb, wvt, wu, wgv, qg, kg, tabs,
                               grp, tm)

    k_bound2 = HEAD_DIM * jnp.max(jnp.square(k_norm_g)) * BOUND_SLACK
    q_bound = math.sqrt(HEAD_DIM) * scale * BOUND_SLACK * jnp.max(jnp.abs(q_norm_g))
    safe = (q_bound * jnp.sqrt(k_bound2) * BOUND_SLACK < SAFE_SCORE_BOUND).astype(I32).reshape(1)
    attn = _flash(qt, k, vt, jnp.full((8, 128), k_bound2, F32), safe, grp,
                  _tile(s_min, 1024), _tile(s_min, 1024))
    sgun = _sgu(u, gv, sgu_ln_g.reshape(1, A_WIDTH), sgu_ln_b.reshape(1, A_WIDTH), w_s.astype(BF16),
                b_s.T, sgu_out_g.reshape(1, A_WIDTH), _tile(s_min, 512))

    x1, h2, lgt = _outproj(attn, sgun, xa, xb, modp, attn_out_g.reshape(1, ATTN_WIDTH), w_out.astype(BF16),
                           norm2_g.reshape(1, d), w_router.T.astype(BF16), grp, _tile(s_min, 512))

    idx, wts, rank, cnt = _router(lgt, router_bias.reshape(N_EXPERTS, 1), _tile(s_min, 512))

    bm = 512
    counts = cnt[:, 0]
    padded = (counts + bm - 1) // bm * bm
    pad_end = jnp.cumsum(padded).astype(I32)
    pad_start = pad_end - padded
    n_blocks = n * TOP_K // bm + N_EXPERTS
    p_tok = n_blocks * bm
    n_used = (pad_end[-1] // bm).reshape(1).astype(I32)
    block_start = jnp.arange(n_blocks, dtype=I32) * bm
    block_exp = jnp.minimum(jnp.sum((pad_end[None, :] <= block_start[:, None]).astype(I32), axis=1),
                            N_EXPERTS - 1)

    dest = _dest(idx, rank, pad_start, _tile(s_min, 2048))

    td = _tile(s_min, 256)
    xs = _dispatch(counts, pad_start, pad_end, _tok3(dest, td), h2, n, p_tok, td, bm)
    shared = _shared(h2, jnp.concatenate([ws_gate, ws_up], axis=1).astype(BF16), ws_down.astype(BF16),
                     _tile(s_min, 512))
    ys = _experts(block_exp, n_used, xs, w_gate, w_up, w_down, bm)

    tc = _tile(s_min, 128)
    return _combine(_tok3(dest, tc), _tok3(wts, tc), x1, shared, modp, ys, grp, tc)


def kernel(x_prompt, x_sample, c_prompt, c_sample, norm1_g, w_mod, b_mod, w_in, q_norm_g, k_norm_g, sgu_ln_g, sgu_ln_b, w_s, b_s, attn_out_g, sgu_out_g, w_out, norm2_g, w_router, router_bias, w_gate, w_up, w_down, ws_gate, ws_up, ws_down):
    depth = norm1_g.shape[0]
    bp, sp, d = x_prompt.shape
    bs, ss, _ = x_sample.shape
    grp = _Groups([(bp, sp), (bs, ss)])
    xa, xb = x_prompt.reshape(bp * sp, d), x_sample.reshape(bs * ss, d)
    c = jnp.concatenate([c_prompt, c_sample], axis=0)
    for l in range(depth):
        xa, xb = _layer(xa, xb, c, grp, norm1_g[l], w_mod[l], b_mod[l], w_in[l], q_norm_g[l], k_norm_g[l],
                   sgu_ln_g[l], sgu_ln_b[l], w_s[l], b_s[l], attn_out_g[l], sgu_out_g[l], w_out[l],
                   norm2_g[l], w_router[l], router_bias[l], w_gate[l], w_up[l], w_down[l],
                   ws_gate[l], ws_up[l], ws_down[l])
    return (xa.reshape(bp, sp, d), xb.reshape(bs, ss, d))
```

```python
import functools
import math

import numpy as np
import jax
import jax.numpy as jnp
from jax import lax
from jax.experimental import pallas as pl
from jax.experimental.pallas import tpu as pltpu

F32 = jnp.float32
BF16 = jnp.bfloat16
I32 = jnp.int32

HEAD_DIM = 128
N_Q_HEADS = 8
N_KV_HEADS = 2
Q_PER_KV = N_Q_HEADS // N_KV_HEADS
ATTN_WIDTH = N_Q_HEADS * HEAD_DIM
KV_WIDTH = N_KV_HEADS * HEAD_DIM
ROPE_THETA = 10000.0
GRID_W = 64
N_A_HEADS = 8
A_HEAD_DIM = 128
A_WIDTH = N_A_HEADS * A_HEAD_DIM
CHUNK = 128
N_EXPERTS = 64
TOP_K = 8
N_GROUPS = 8
GROUP_SIZE = N_EXPERTS // N_GROUPS
TOPK_GROUPS = 4
ROUTED_SCALE = 2.5
N_MOD = 6
EPS = 1e-6

VMEM_LIMIT_BYTES = 56 * 1024 * 1024
NT_DIMS = (((1,), (1,)), ((), ()))


def _cparams(n_axes):
    return pltpu.CompilerParams(
        dimension_semantics=("arbitrary",) * n_axes, vmem_limit_bytes=VMEM_LIMIT_BYTES)


def _tile(n, pref):
    t = min(n, pref)
    while n % t:
        t //= 2
    return t


def _const_spec(shape):
    nd = len(shape)
    return pl.BlockSpec(shape, lambda *a: (0,) * nd, pipeline_mode=pl.Buffered(1))


def _mod_kernel(c_ref, w_ref, b_ref, o_ref):
    c = c_ref[...]
    a = (c * jax.nn.sigmoid(c)).astype(BF16)
    o_ref[...] = jnp.dot(a, w_ref[...].astype(BF16), preferred_element_type=F32) + b_ref[...]


def _mod(c8, w_mod, b_mod):
    d, n = w_mod.shape
    tn = _tile(n, 1024)
    return pl.pallas_call(
        _mod_kernel,
        grid=(n // tn,),
        in_specs=[_const_spec((8, d)),
                  pl.BlockSpec((d, tn), lambda j: (0, j)),
                  pl.BlockSpec((1, tn), lambda j: (0, j))],
        out_specs=pl.BlockSpec((8, tn), lambda j: (0, j)),
        out_shape=jax.ShapeDtypeStruct((8, n), F32),
        compiler_params=_cparams(1),
        name="mod",
    )(c8, w_mod, b_mod.reshape(1, n))


class _Groups:
    def __init__(self, groups):
        self.groups = groups
        self.n_tokens = sum(b * s for b, s in groups)

    def tile_maps(self, t):
        (b0, s0), (b1, s1) = self.groups
        n0 = b0 * s0 // t
        t0, t1 = s0 // t, s1 // t

        def bidx(i):
            return jnp.where(i < n0, i // t0, b0 + (i - n0) // t1)

        def pidx(i):
            return jnp.where(i < n0, i % t0, (i - n0) % t1)

        return bidx, pidx

    def split_specs(self, t, d):
        (b0, s0), _ = self.groups
        n0 = b0 * s0 // t
        return (pl.BlockSpec((t, d), lambda i, *_: (jnp.minimum(i, n0 - 1), 0)),
                pl.BlockSpec((t, d), lambda i, *_: (jnp.maximum(i - n0, 0), 0)), n0)


def _pick_group(i, n0, a_ref, b_ref):
    return jnp.where(i < n0, a_ref[...], b_ref[...])


LANES = 128
SUBLANES = 8
U32 = jnp.uint32


def _slab_rows(d):
    rows = d // (2 * LANES)
    assert d % (2 * LANES) == 0 and rows % SUBLANES == 0, d
    return rows


def _load_slabs(ref, rows, tok0, n_tok):
    pairs = [_unpack_bf16_pair(ref[pl.ds(tok0 * rows + r, n_tok, stride=rows), :]) for r in range(rows)]
    return jnp.concatenate([p[0] for p in pairs] + [p[1] for p in pairs], axis=1).astype(BF16)


def _pack_bf16_pair(lo, hi):
    def bf16_bits(v):
        return lax.bitcast_convert_type(v.astype(BF16).astype(F32), U32)

    return (bf16_bits(lo) >> 16) | bf16_bits(hi)


def _unpack_bf16_pair(w):
    return (lax.bitcast_convert_type(w << 16, F32),
            lax.bitcast_convert_type(w & jnp.uint32(0xFFFF0000), F32))


def _store_slabs(ref, x, tok0=0):
    n_tok, d = x.shape
    rows = _slab_rows(d)
    for r in range(rows):
        ref[pl.ds(tok0 * rows + r, n_tok, stride=rows), :] = _pack_bf16_pair(
            x[:, r * LANES:(r + 1) * LANES], x[:, (rows + r) * LANES:(rows + r + 1) * LANES])


def _rope_tables(seq_len):
    pos = np.arange(seq_len)
    row = (pos // GRID_W).astype(np.float64)
    col = (pos % GRID_W).astype(np.float64)
    n_pairs = HEAD_DIM // 4
    freqs = ROPE_THETA ** (-np.arange(n_pairs, dtype=np.float64) / n_pairs)
    ang = np.concatenate([row[:, None] * freqs, col[:, None] * freqs], axis=-1)
    c, s = np.cos(ang), np.sin(ang)
    cos2 = np.concatenate([c, c], axis=-1)
    sin2 = np.concatenate([-s, s], axis=-1)
    return cos2, sin2


def _inproj_kernel(xa_ref, xb_ref, mod_ref, g1_ref, wqt_ref, wk_ref, wvt_ref, wu_ref, wgv_ref,
                   qg_ref, kg_ref, cost_ref, sint_ref, cos_ref, sin_ref,
                   qt_ref, k_ref, vt_ref, u_ref, gv_ref, *, n0):
    x = _pick_group(pl.program_id(0), n0, xa_ref, xb_ref)
    shift1 = mod_ref[0:1, :]
    scale1 = mod_ref[1:2, :]
    ms = jnp.mean(x * x, axis=-1, keepdims=True)
    h = (x * lax.rsqrt(ms + EPS) * g1_ref[...]) * (1.0 + scale1) + shift1
    hb = h.astype(BF16)
    half = HEAD_DIM // 2

    qt = lax.dot_general(wqt_ref[...], hb, NT_DIMS, preferred_element_type=F32)
    cost = cost_ref[...]
    sint = sint_ref[...]
    qg = qg_ref[...]
    for hh in range(N_Q_HEADS):
        qh = qt[hh * HEAD_DIM:(hh + 1) * HEAD_DIM, :]
        msq = jnp.mean(qh * qh, axis=0, keepdims=True)
        qn = qh * lax.rsqrt(msq + EPS) * qg
        rot = jnp.concatenate([qn[half:, :], qn[:half, :]], axis=0)
        qt_ref[hh * HEAD_DIM:(hh + 1) * HEAD_DIM, :] = (qn * cost + rot * sint).astype(BF16)

    vt_ref[...] = lax.dot_general(wvt_ref[...], hb, NT_DIMS, preferred_element_type=F32).astype(BF16)

    kk = jnp.dot(hb, wk_ref[...], preferred_element_type=F32)
    cos = cos_ref[...]
    sin = sin_ref[...]
    kg = kg_ref[...]
    for j in range(N_KV_HEADS):
        kh = kk[:, j * HEAD_DIM:(j + 1) * HEAD_DIM]
        msk = jnp.mean(kh * kh, axis=-1, keepdims=True)
        kn = kh * lax.rsqrt(msk + EPS) * kg
        rot = jnp.concatenate([kn[:, half:], kn[:, :half]], axis=1)
        k_ref[:, j * HEAD_DIM:(j + 1) * HEAD_DIM] = (kn * cos + rot * sin).astype(BF16)

    u_ref[...] = jnp.dot(hb, wu_ref[...], preferred_element_type=F32).astype(BF16)
    gv_ref[...] = jnp.dot(hb, wgv_ref[...], preferred_element_type=F32).astype(BF16)


def _inproj(xa, xb, modp, g1, wqt, wk, wvt, wu, wgv, qg, kg, tabs, grp, tm):
    n, d = grp.n_tokens, xa.shape[1]
    bidx, pidx = grp.tile_maps(tm)
    cost, sint, cos, sin = tabs
    tok = lambda w: pl.BlockSpec((tm, w), lambda i: (i, 0))
    tokt = lambda w: pl.BlockSpec((w, tm), lambda i: (0, i))
    xa_spec, xb_spec, n0 = grp.split_specs(tm, d)
    return pl.pallas_call(
        functools.partial(_inproj_kernel, n0=n0),
        grid=(n // tm,),
        in_specs=[xa_spec, xb_spec,
                  pl.BlockSpec((None, 8, d), lambda i: (bidx(i), 0, 0)),
                  _const_spec((1, d)),
                  _const_spec(wqt.shape), _const_spec(wk.shape), _const_spec(wvt.shape),
                  _const_spec(wu.shape), _const_spec(wgv.shape),
                  _const_spec((HEAD_DIM, 1)), _const_spec((1, HEAD_DIM)),
                  pl.BlockSpec((HEAD_DIM, tm), lambda i: (0, pidx(i))),
                  pl.BlockSpec((HEAD_DIM, tm), lambda i: (0, pidx(i))),
                  pl.BlockSpec((tm, HEAD_DIM), lambda i: (pidx(i), 0)),
                  pl.BlockSpec((tm, HEAD_DIM), lambda i: (pidx(i), 0))],
        out_specs=[tokt(ATTN_WIDTH), tok(KV_WIDTH), tokt(KV_WIDTH), tok(A_WIDTH), tok(A_WIDTH)],
        out_shape=[jax.ShapeDtypeStruct((ATTN_WIDTH, n), BF16),
                   jax.ShapeDtypeStruct((n, KV_WIDTH), BF16),
                   jax.ShapeDtypeStruct((KV_WIDTH, n), BF16),
                   jax.ShapeDtypeStruct((n, A_WIDTH), BF16),
                   jax.ShapeDtypeStruct((n, A_WIDTH), BF16)],
        compiler_params=_cparams(1),
        name="inproj",
    )(xa, xb, modp, g1, wqt, wk, wvt, wu, wgv, qg, kg, cost, sint, cos, sin)


LOG2E = 1.4426950408889634
SAFE_SCORE_BOUND = 60.0
BOUND_SLACK = 1.01


def _flash_kernel(qt_s, kt_s, vp_s, fl_s, safe_s, q_ref, k_ref, v_ref, vp_ref, km_ref, o_ref,
                  qs_sc, m_sc, l_sc, acc_sc, p_sc, *, bq, hb):
    s_id = pl.program_id(1)
    flag = fl_s[s_id]
    safe = safe_s[0] != 0
    last = (flag & 2) != 0

    @pl.when((flag & 1) != 0)
    def _init():
        l_sc[...] = jnp.zeros(l_sc.shape, F32)
        acc_sc[...] = jnp.zeros(acc_sc.shape, F32)
        p_sc[...] = jnp.zeros(p_sc.shape, BF16)
        for hh in range(Q_PER_KV):
            qs_sc[:, hh * bq:(hh + 1) * bq] = q_ref[hh * HEAD_DIM:(hh + 1) * HEAD_DIM, :]
        qf = qs_sc[...].astype(F32)
        qn2 = jnp.sum(qf * qf, axis=0, keepdims=True)
        bound = jnp.sqrt(qn2 * km_ref[0:1, 0:1]) * BOUND_SLACK
        m_sc[...] = jnp.where(safe, bound, -jnp.inf)

    @pl.when(safe)
    def _fast():
        q = qs_sc[...]
        m = m_sc[...]
        sa = jnp.dot(k_ref[:hb, :], q, preferred_element_type=F32)
        acc = acc_sc[...] + jnp.dot(vp_ref[...], p_sc[...], preferred_element_type=F32)
        pa = jnp.exp2(sa - m)
        la = jnp.sum(pa, axis=0, keepdims=True)
        sb = jnp.dot(k_ref[hb:, :], q, preferred_element_type=F32)
        acc_sc[...] = acc + jnp.dot(v_ref[:, :hb], pa.astype(BF16), preferred_element_type=F32)
        pb = jnp.exp2(sb - m)
        l_sc[...] += la + jnp.sum(pb, axis=0, keepdims=True)
        p_sc[...] = pb.astype(BF16)

    @pl.when(jnp.logical_and(safe, last))
    def _flush():
        acc_sc[...] += jnp.dot(v_ref[:, hb:], p_sc[...], preferred_element_type=F32)

    @pl.when(jnp.logical_not(safe))
    def _online():
        s = jnp.dot(k_ref[...], qs_sc[...], preferred_element_type=F32)
        m_prev = m_sc[...]
        m_new = jnp.maximum(m_prev, jnp.max(s, axis=0, keepdims=True))
        alpha = jnp.exp2(m_prev - m_new)
        p = jnp.exp2(s - m_new)
        l_sc[...] = alpha * l_sc[...] + jnp.sum(p, axis=0, keepdims=True)
        acc_sc[...] = alpha * acc_sc[...] + jnp.dot(v_ref[...], p.astype(BF16), preferred_element_type=F32)
        m_sc[...] = m_new

    @pl.when((flag & 2) != 0)
    def _fin():
        o = acc_sc[...] / l_sc[...]
        for hh in range(Q_PER_KV):
            o_ref[:, hh * HEAD_DIM:(hh + 1) * HEAD_DIM] = o[:, hh * bq:(hh + 1) * bq].T.astype(BF16)


def _flash_schedule(grp, bq, bkv):
    qt, kt, vp, fl = [], [], [], []
    off = 0
    for b, s in grp.groups:
        for bi in range(b):
            base = off + bi * s
            nk = s // bkv
            for qi in range(s // bq):
                for ki in range(nk):
                    blk = base // bkv + ki
                    qt.append(base // bq + qi)
                    kt.append(blk)
                    vp.append(2 * (blk - 1 if ki else blk) + 1)
                    fl.append((1 if ki == 0 else 0) | (2 if ki == nk - 1 else 0))
        off += b * s
    return tuple(np.asarray(a, np.int32) for a in (qt, kt, vp, fl))


def _flash(qt, k, vt, kmax2, safe, grp, bq, bkv):
    n = k.shape[0]
    hb = bkv // 2
    qt_a, kt_a, vp_a, fl_a = _flash_schedule(grp, bq, bkv)
    n_steps = len(qt_a)
    w = Q_PER_KV * HEAD_DIM
    gs = pltpu.PrefetchScalarGridSpec(
        num_scalar_prefetch=5,
        grid=(N_KV_HEADS, n_steps),
        in_specs=[pl.BlockSpec((w, bq), lambda h, s, qa, ka, va, fa, sa: (h, qa[s])),
                  pl.BlockSpec((bkv, HEAD_DIM), lambda h, s, qa, ka, va, fa, sa: (ka[s], h)),
                  pl.BlockSpec((HEAD_DIM, bkv), lambda h, s, qa, ka, va, fa, sa: (h, ka[s])),
                  pl.BlockSpec((HEAD_DIM, hb), lambda h, s, qa, ka, va, fa, sa: (h, va[s])),
                  pl.BlockSpec((8, 128), lambda h, s, qa, ka, va, fa, sa: (0, 0))],
        out_specs=pl.BlockSpec((bq, w), lambda h, s, qa, ka, va, fa, sa: (qa[s], h)),
        scratch_shapes=[pltpu.VMEM((HEAD_DIM, Q_PER_KV * bq), BF16),
                        pltpu.VMEM((1, Q_PER_KV * bq), F32),
                        pltpu.VMEM((1, Q_PER_KV * bq), F32),
                        pltpu.VMEM((HEAD_DIM, Q_PER_KV * bq), F32),
                        pltpu.VMEM((hb, Q_PER_KV * bq), BF16)])
    return pl.pallas_call(
        functools.partial(_flash_kernel, bq=bq, hb=hb),
        grid_spec=gs,
        out_shape=jax.ShapeDtypeStruct((n, ATTN_WIDTH), BF16),
        compiler_params=_cparams(2),
        name="flash",
    )(jnp.asarray(qt_a), jnp.asarray(kt_a), jnp.asarray(vp_a), jnp.asarray(fl_a), safe, qt, k, vt, vt, kmax2)


def _sgu_kernel(u_ref, gv_ref, lng_ref, lnb_ref, ws_ref, bst_ref, og_ref, o_ref, *, n_chunks):
    v = gv_ref[...].astype(F32)
    mean = jnp.mean(v, axis=-1, keepdims=True)
    vc = v - mean
    var = jnp.mean(vc * vc, axis=-1, keepdims=True)
    vn = (vc * lax.rsqrt(var + EPS) * lng_ref[...] + lnb_ref[...]).astype(BF16)
    u = u_ref[...].astype(F32)
    heads = []
    for hh in range(N_A_HEADS):
        cols = slice(hh * A_HEAD_DIM, (hh + 1) * A_HEAD_DIM)
        rhs = jnp.concatenate([vn[c * CHUNK:(c + 1) * CHUNK, cols] for c in range(n_chunks)], axis=1)
        mixed = jnp.dot(ws_ref[hh], rhs, preferred_element_type=F32) + bst_ref[:, hh:hh + 1]
        heads.append(jnp.concatenate(
            [mixed[:, c * A_HEAD_DIM:(c + 1) * A_HEAD_DIM] for c in range(n_chunks)], axis=0))
    sg = u * jnp.concatenate(heads, axis=1)
    ms = jnp.mean(sg * sg, axis=-1, keepdims=True)
    o_ref[...] = (sg * lax.rsqrt(ms + EPS) * og_ref[...]).astype(BF16)


def _sgu(u, gv, lng, lnb, ws, bst, og, tm):
    n = u.shape[0]
    tok = pl.BlockSpec((tm, A_WIDTH), lambda i: (i, 0))
    return pl.pallas_call(
        functools.partial(_sgu_kernel, n_chunks=tm // CHUNK),
        grid=(n // tm,),
        in_specs=[tok, tok, _const_spec((1, A_WIDTH)), _const_spec((1, A_WIDTH)),
                  _const_spec(ws.shape), _const_spec(bst.shape), _const_spec((1, A_WIDTH))],
        out_specs=tok,
        out_shape=jax.ShapeDtypeStruct((n, A_WIDTH), BF16),
        compiler_params=_cparams(1),
        name="sgu",
    )(u, gv, lng, lnb, ws, bst, og)


def _outproj_kernel(attn_ref, sgu_ref, xa_ref, xb_ref, mod_ref, ag_ref, wo_ref, g2_ref, wrt_ref,
                    x1_ref, h2_ref, lgt_ref, *, n0):
    half = attn_ref.shape[0] // 2
    spans = [(0, half), (half, 2 * half)]
    outs = []
    for r0, r1 in spans:
        a = attn_ref[r0:r1, :].astype(F32)
        ms = jnp.mean(a * a, axis=-1, keepdims=True)
        an = (a * lax.rsqrt(ms + EPS) * ag_ref[...]).astype(BF16)
        o = jnp.dot(an, wo_ref[:ATTN_WIDTH, :], preferred_element_type=F32)
        outs.append(o + jnp.dot(sgu_ref[r0:r1, :], wo_ref[ATTN_WIDTH:, :], preferred_element_type=F32))
    gate1 = mod_ref[2:3, :]
    shift2 = mod_ref[3:4, :]
    scale2 = mod_ref[4:5, :]
    first_group = pl.program_id(0) < n0
    for (r0, r1), o in zip(spans, outs):
        x1 = jnp.where(first_group, xa_ref[r0:r1, :], xb_ref[r0:r1, :]) + gate1 * o
        x1_ref[r0:r1, :] = x1
        ms2 = jnp.mean(x1 * x1, axis=-1, keepdims=True)
        h2 = (x1 * lax.rsqrt(ms2 + EPS) * g2_ref[...]) * (1.0 + scale2) + shift2
        _store_slabs(h2_ref, h2, r0)
        lgt_ref[:, r0:r1] = lax.dot_general(wrt_ref[...], h2.astype(BF16), NT_DIMS,
                                            preferred_element_type=F32)


def _outproj(attn, sgun, xa, xb, modp, ag, wo, g2, wrt, grp, tm):
    n, d = grp.n_tokens, xa.shape[1]
    bidx, _ = grp.tile_maps(tm)
    tok = lambda w: pl.BlockSpec((tm, w), lambda i: (i, 0))
    xa_spec, xb_spec, n0 = grp.split_specs(tm, d)
    return pl.pallas_call(
        functools.partial(_outproj_kernel, n0=n0),
        grid=(n // tm,),
        in_specs=[tok(ATTN_WIDTH), tok(A_WIDTH), xa_spec, xb_spec,
                  pl.BlockSpec((None, 8, d), lambda i: (bidx(i), 0, 0)),
                  _const_spec((1, ATTN_WIDTH)), _const_spec(wo.shape), _const_spec((1, d)),
                  _const_spec(wrt.shape)],
        out_specs=[tok(d), pl.BlockSpec((tm * _slab_rows(d), LANES), lambda i: (i, 0)),
                   pl.BlockSpec((N_EXPERTS, tm), lambda i: (0, i))],
        out_shape=[jax.ShapeDtypeStruct((n, d), F32),
                   jax.ShapeDtypeStruct((n * _slab_rows(d), LANES), U32),
                   jax.ShapeDtypeStruct((N_EXPERTS, n), F32)],
        compiler_params=_cparams(1),
        name="outproj",
    )(attn, sgun, xa, xb, modp, ag, wo, g2, wrt)


def _first_index(hit, iota, sentinel):
    return jnp.min(jnp.where(hit, iota, sentinel), axis=0, keepdims=True)


def _router_kernel(lg_ref, bias_ref, idx_ref, w_ref, rank_ref, cnt_ref, run_sc):
    i = pl.program_id(0)
    tt = lg_ref.shape[1]

    @pl.when(i == 0)
    def _init():
        run_sc[...] = jnp.zeros(run_sc.shape, F32)

    scores = jax.nn.sigmoid(lg_ref[...])
    biased = scores + bias_ref[...]
    neg = -jnp.inf
    iota_g = lax.broadcasted_iota(I32, (GROUP_SIZE, tt), 0).astype(F32)

    gs_rows = []
    for g in range(N_GROUPS):
        xg = biased[g * GROUP_SIZE:(g + 1) * GROUP_SIZE, :]
        m1 = jnp.max(xg, axis=0, keepdims=True)
        f1 = _first_index(xg == m1, iota_g, float(GROUP_SIZE))
        m2 = jnp.max(jnp.where(iota_g == f1, neg, xg), axis=0, keepdims=True)
        gs_rows.append(m1 + m2)
    gs = jnp.concatenate(gs_rows, axis=0)

    iota_n = lax.broadcasted_iota(I32, (N_GROUPS, tt), 0).astype(F32)
    gsel = jnp.zeros((N_GROUPS, tt), F32)
    cur = gs
    for _ in range(TOPK_GROUPS):
        m = jnp.max(cur, axis=0, keepdims=True)
        f = _first_index(cur == m, iota_n, float(N_GROUPS))
        hit = iota_n == f
        gsel = jnp.where(hit, 1.0, gsel)
        cur = jnp.where(hit, neg, cur)

    masked = jnp.concatenate(
        [jnp.where(gsel[g:g + 1, :] > 0.5, biased[g * GROUP_SIZE:(g + 1) * GROUP_SIZE, :], neg)
         for g in range(N_GROUPS)], axis=0)

    iota_e = lax.broadcasted_iota(I32, (N_EXPERTS, tt), 0).astype(F32)
    sel = jnp.zeros((N_EXPERTS, tt), F32)
    idx_rows, w_rows = [], []
    for _ in range(TOP_K):
        m = jnp.max(masked, axis=0, keepdims=True)
        f = _first_index(masked == m, iota_e, float(N_EXPERTS))
        hit = iota_e == f
        idx_rows.append(f)
        w_rows.append(jnp.sum(jnp.where(hit, scores, 0.0), axis=0, keepdims=True))
        sel = jnp.where(hit, 1.0, sel)
        masked = jnp.where(hit, neg, masked)
    idx = jnp.concatenate(idx_rows, axis=0)
    wk = jnp.concatenate(w_rows, axis=0)
    wk = wk / (jnp.sum(wk, axis=0, keepdims=True) + 1e-20) * ROUTED_SCALE

    r = lax.broadcasted_iota(I32, (tt, tt), 0)
    c = lax.broadcasted_iota(I32, (tt, tt), 1)
    upper = jnp.where(r < c, 1.0, 0.0).astype(BF16)
    prefix = jnp.dot(sel.astype(BF16), upper, preferred_element_type=F32) + run_sc[...]
    rank_rows = [jnp.sum(jnp.where(iota_e == idx[k:k + 1, :], prefix, 0.0), axis=0, keepdims=True)
                 for k in range(TOP_K)]
    run_sc[...] = run_sc[...] + jnp.sum(sel, axis=1, keepdims=True)

    idx_ref[...] = idx.astype(I32)
    w_ref[...] = wk
    rank_ref[...] = jnp.concatenate(rank_rows, axis=0).astype(I32)
    cnt_ref[...] = jnp.broadcast_to(run_sc[...], cnt_ref.shape).astype(I32)


def _router(lgt, bias, tt):
    n = lgt.shape[1]
    tokk = pl.BlockSpec((TOP_K, tt), lambda i: (0, i))
    return pl.pallas_call(
        _router_kernel,
        grid=(n // tt,),
        in_specs=[pl.BlockSpec((N_EXPERTS, tt), lambda i: (0, i)), _const_spec((N_EXPERTS, 1))],
        out_specs=[tokk, tokk, tokk, _const_spec((N_EXPERTS, 128))],
        out_shape=[jax.ShapeDtypeStruct((TOP_K, n), I32), jax.ShapeDtypeStruct((TOP_K, n), F32),
                   jax.ShapeDtypeStruct((TOP_K, n), I32), jax.ShapeDtypeStruct((N_EXPERTS, 128), I32)],
        scratch_shapes=[pltpu.VMEM((N_EXPERTS, 1), F32)],
        compiler_params=_cparams(1),
        name="router",
    )(lgt, bias)


def _dest_kernel(idx_ref, rank_ref, start_ref, o_ref):
    tt = idx_ref.shape[1]
    iota_e = lax.broadcasted_iota(I32, (N_EXPERTS, tt), 0)
    start = start_ref[...]
    rows = []
    for k in range(TOP_K):
        hit = iota_e == idx_ref[k:k + 1, :]
        rows.append(jnp.sum(jnp.where(hit, start, 0), axis=0, keepdims=True))
    o_ref[...] = jnp.concatenate(rows, axis=0) + rank_ref[...]


def _dest(idx, rank, pad_start, tt):
    n = idx.shape[1]
    tokk = pl.BlockSpec((TOP_K, tt), lambda i: (0, i))
    return pl.pallas_call(
        _dest_kernel,
        grid=(n // tt,),
        in_specs=[tokk, tokk, _const_spec((N_EXPERTS, 1))],
        out_specs=tokk,
        out_shape=jax.ShapeDtypeStruct((TOP_K, n), I32),
        compiler_params=_cparams(1),
        name="dest",
    )(idx, rank, pad_start.reshape(N_EXPERTS, 1))


def _slab(ref, tok, rows):
    return ref.at[pl.ds(pl.multiple_of(tok * rows, SUBLANES), rows)]


DISPATCH_RING = 3


def _dispatch_kernel(cnt_s, start_s, pend_s, dest_ref, h_hbm, xs_ref, hbuf, zero_sc, in_sems, out_sems, sem,
                     *, td, bm, n_blocks):
    i = pl.program_id(0)
    n_steps = pl.num_programs(0)
    rows = zero_sc.shape[0] // bm
    t_rows = td * rows
    slot = i % DISPATCH_RING
    nxt = (i + 1) % DISPATCH_RING

    def load(tile, s):
        return pltpu.make_async_copy(h_hbm.at[pl.ds(pl.multiple_of(tile * t_rows, SUBLANES), t_rows)],
                                     hbuf.at[s], in_sems.at[s])

    def drain(s):
        n_rows = TOP_K * t_rows
        pltpu.make_async_copy(xs_ref.at[pl.ds(0, n_rows)], xs_ref.at[pl.ds(0, n_rows)], out_sems.at[s]).wait()

    @pl.when(i == 0)
    def _():
        load(0, 0).start()

    @pl.when(i + 1 < n_steps)
    def _():
        @pl.when(i + 1 >= DISPATCH_RING)
        def _():
            drain(nxt)

        load(i + 1, nxt).start()

    load(i, slot).wait()

    def issue(t, c):
        for k in range(TOP_K):
            pltpu.make_async_copy(_slab(hbuf.at[slot], t, rows), _slab(xs_ref, dest_ref[0, t * TOP_K + k], rows),
                                  out_sems.at[slot]).start(priority=k % 2)
        return c

    lax.fori_loop(0, td, issue, 0)

    @pl.when(i == n_steps - 1)
    def _pad():
        for back in range(DISPATCH_RING):
            @pl.when(i - back >= 0)
            def _():
                drain((i - back) % DISPATCH_RING)

        zero_sc[...] = jnp.zeros(zero_sc.shape, U32)

        def pad_copy(p):
            return pltpu.make_async_copy(zero_sc.at[pl.ds(0, rows)], _slab(xs_ref, p, rows), sem)

        def tail_copy(b):
            return pltpu.make_async_copy(zero_sc, _slab(xs_ref, b, bm * rows), sem)

        first_unused = pend_s[N_EXPERTS - 1] // bm

        def tail_go(b, c):
            tail_copy(b).start()
            return c

        lax.fori_loop(first_unused, n_blocks, tail_go, 0)

        def tail_done(b, c):
            tail_copy(b).wait()
            return c

        lax.fori_loop(first_unused, n_blocks, tail_done, 0)

        def per_expert(e, c):
            lo = start_s[e] + cnt_s[e]
            hi = pend_s[e]

            def go(p, c2):
                pad_copy(p).start()
                return c2

            lax.fori_loop(lo, hi, go, 0)

            def done(p, c2):
                pad_copy(p).wait()
                return c2

            lax.fori_loop(lo, hi, done, 0)
            return c

        lax.fori_loop(0, N_EXPERTS, per_expert, 0)


def _dispatch(counts, pad_start, pad_end, dest3, h2s, n, p_tok, td, bm):
    rows = h2s.shape[0] // n
    gs = pltpu.PrefetchScalarGridSpec(
        num_scalar_prefetch=3,
        grid=(n // td,),
        in_specs=[pl.BlockSpec((None, 1, TOP_K * td), lambda i, *_: (i, 0, 0), memory_space=pltpu.SMEM),
                  pl.BlockSpec(memory_space=pl.ANY)],
        out_specs=pl.BlockSpec(memory_space=pl.ANY),
        scratch_shapes=[pltpu.VMEM((DISPATCH_RING, td * rows, LANES), U32),
                        pltpu.VMEM((bm * rows, LANES), U32),
                        pltpu.SemaphoreType.DMA((DISPATCH_RING,)), pltpu.SemaphoreType.DMA((DISPATCH_RING,)),
                        pltpu.SemaphoreType.DMA(())])
    return pl.pallas_call(
        functools.partial(_dispatch_kernel, td=td, bm=bm, n_blocks=p_tok // bm),
        grid_spec=gs,
        out_shape=jax.ShapeDtypeStruct((p_tok * rows, LANES), U32),
        compiler_params=_cparams(1),
        name="dispatch",
    )(counts, pad_start, pad_end, dest3, h2s)


def _swiglu_slabs(x_ref, wgu_ref, wd_ref, o_ref, *, packed_out):
    de, d = wd_ref.shape
    rows = _slab_rows(d)
    half = x_ref.shape[0] // rows // 2
    ups = [jnp.dot(_load_slabs(x_ref, rows, t0, half), wgu_ref[...],
                   preferred_element_type=F32) for t0 in (0, half)]
    for t0, r in zip((0, half), ups):
        a = r[:, :de]
        act = (a * jax.nn.sigmoid(a) * r[:, de:]).astype(BF16)
        out = jnp.dot(act, wd_ref[...], preferred_element_type=F32)
        if packed_out:
            _store_slabs(o_ref, out, t0)
        else:
            o_ref[t0:t0 + half, :] = out


def _shared_kernel(h_ref, wgu_ref, wd_ref, o_ref):
    _swiglu_slabs(h_ref, wgu_ref, wd_ref, o_ref, packed_out=False)


def _shared(h2s, wgu, wd, tm):
    d = wd.shape[1]
    rows = _slab_rows(d)
    n = h2s.shape[0] // rows
    return pl.pallas_call(
        _shared_kernel,
        grid=(n // tm,),
        in_specs=[pl.BlockSpec((tm * rows, LANES), lambda i: (i, 0)),
                  _const_spec(wgu.shape), _const_spec(wd.shape)],
        out_specs=pl.BlockSpec((tm, d), lambda i: (i, 0)),
        out_shape=jax.ShapeDtypeStruct((n, d), F32),
        compiler_params=_cparams(1),
        name="shared",
    )(h2s, wgu, wd)


def _experts_kernel(be_s, nu_s, x_ref, wg_ref, wu_ref, wd_ref, o_ref, wgu_sc, wd_sc):
    i = pl.program_id(0)
    used = i < nu_s[0]
    prev = be_s[jnp.maximum(i - 1, 0)]
    fresh = jnp.logical_or(i == 0, be_s[i] != prev)
    de = wd_sc.shape[0]

    @pl.when(jnp.logical_and(used, fresh))
    def _cast():
        wgu_sc[:, :de] = wg_ref[...].astype(BF16)
        wgu_sc[:, de:] = wu_ref[...].astype(BF16)
        wd_sc[...] = wd_ref[...].astype(BF16)

    @pl.when(used)
    def _run():
        _swiglu_slabs(x_ref, wgu_sc, wd_sc, o_ref, packed_out=True)

    @pl.when(jnp.logical_not(used))
    def _unused():
        o_ref[...] = jnp.zeros(o_ref.shape, U32)


def _experts(block_exp, n_used, xs, w_gate, w_up, w_down, bm):
    d, de = w_gate.shape[1], w_gate.shape[2]
    rows = _slab_rows(d)
    p_tok = xs.shape[0] // rows
    n_blocks = p_tok // bm
    row = lambda i, be, nu: (jnp.minimum(i, nu[0] - 1), 0)
    wsel = lambda i, be, nu: (be[jnp.minimum(i, nu[0] - 1)], 0, 0)
    gs = pltpu.PrefetchScalarGridSpec(
        num_scalar_prefetch=2,
        grid=(n_blocks,),
        in_specs=[pl.BlockSpec((bm * rows, LANES), row),
                  pl.BlockSpec((None, d, de), wsel),
                  pl.BlockSpec((None, d, de), wsel),
                  pl.BlockSpec((None, de, d), wsel)],
        out_specs=pl.BlockSpec((bm * rows, LANES), lambda i, be, nu: (i, 0)),
        scratch_shapes=[pltpu.VMEM((d, 2 * de), BF16), pltpu.VMEM((de, d), BF16)])
    return pl.pallas_call(
        _experts_kernel,
        grid_spec=gs,
        out_shape=jax.ShapeDtypeStruct(xs.shape, U32),
        compiler_params=_cparams(1),
        name="experts",
    )(block_exp, n_used, xs, w_gate, w_up, w_down)


def _combine_kernel(dest_ref, nxt_ref, w_ref, x1_ref, sh_ref, mod_ref, ys_ref, oa_ref, ob_ref,
                    buf, lo_sc, hi_sc, sems, *, tc, n0):
    i = pl.program_id(0)
    rows = lo_sc.shape[0] // tc
    cur = i % 2

    def gather(d_ref, b):
        def issue(t, c):
            for k in range(TOP_K):
                pltpu.make_async_copy(_slab(ys_ref, d_ref[0, t * TOP_K + k], rows),
                                      _slab(buf.at[b, k], t, rows),
                                      sems.at[b]).start(priority=k % 2)
            return c

        lax.fori_loop(0, tc, issue, 0)

    @pl.when(i == 0)
    def _():
        gather(dest_ref, 0)

    @pl.when(i + 1 < pl.num_programs(0))
    def _():
        gather(nxt_ref, 1 - cur)

    pltpu.make_async_copy(buf.at[cur], buf.at[cur], sems.at[cur]).wait()

    def token_sum(t, c):
        r0 = pl.multiple_of(t * rows, SUBLANES)
        lo = hi = None
        for k in range(TOP_K):
            lo_k, hi_k = _unpack_bf16_pair(buf[cur, k, pl.ds(r0, rows), :])
            wk = w_ref[0, t * TOP_K + k]
            lo = wk * lo_k if lo is None else lo + wk * lo_k
            hi = wk * hi_k if hi is None else hi + wk * hi_k
        lo_sc[pl.ds(r0, rows), :] = lo
        hi_sc[pl.ds(r0, rows), :] = hi
        return c

    lax.fori_loop(0, tc, token_sum, 0, unroll=4)

    routed = jnp.concatenate([lo_sc[pl.ds(r, tc, stride=rows), :] for r in range(rows)]
                             + [hi_sc[pl.ds(r, tc, stride=rows), :] for r in range(rows)], axis=1)
    y = x1_ref[...] + mod_ref[5:6, :] * (sh_ref[...] + routed)

    @pl.when(pl.program_id(0) < n0)
    def _():
        oa_ref[...] = y

    @pl.when(pl.program_id(0) >= n0)
    def _():
        ob_ref[...] = y


def _combine(dest3, w3, x1, shared, modp, ys, grp, tc):
    n, d = x1.shape
    rows = _slab_rows(d)
    bidx, _ = grp.tile_maps(tc)
    tok = pl.BlockSpec((tc, d), lambda i: (i, 0))
    oa_spec, ob_spec, n0 = grp.split_specs(tc, d)
    (b0, s0), (b1, s1) = grp.groups
    return pl.pallas_call(
        functools.partial(_combine_kernel, tc=tc, n0=n0),
        grid=(n // tc,),
        in_specs=[pl.BlockSpec((None, 1, TOP_K * tc), lambda i: (i, 0, 0), memory_space=pltpu.SMEM),
                  pl.BlockSpec((None, 1, TOP_K * tc), lambda i: (jnp.minimum(i + 1, n // tc - 1), 0, 0),
                               memory_space=pltpu.SMEM),
                  pl.BlockSpec((None, 1, TOP_K * tc), lambda i: (i, 0, 0), memory_space=pltpu.SMEM),
                  tok, tok,
                  pl.BlockSpec((None, 8, d), lambda i: (bidx(i), 0, 0)),
                  pl.BlockSpec(memory_space=pl.ANY)],
        out_specs=[oa_spec, ob_spec],
        out_shape=[jax.ShapeDtypeStruct((b0 * s0, d), F32), jax.ShapeDtypeStruct((b1 * s1, d), F32)],
        scratch_shapes=[pltpu.VMEM((2, TOP_K, tc * rows, LANES), U32),
                        pltpu.VMEM((tc * rows, LANES), F32), pltpu.VMEM((tc * rows, LANES), F32),
                        pltpu.SemaphoreType.DMA((2,))],
        compiler_params=_cparams(1),
        name="combine",
    )(dest3, dest3, w3, x1, shared, modp, ys)


def _tok3(a, t):
    k, n = a.shape
    return a.reshape(k, n // t, t).transpose(1, 2, 0).reshape(n // t, 1, t * k)


def _layer(xa, xb, c, grp, norm1_g, w_mod, b_mod, w_in, q_norm_g, k_norm_g, sgu_ln_g, sgu_ln_b, w_s, b_s,
           attn_out_g, sgu_out_g, w_out, norm2_g, w_router, router_bias, w_gate, w_up, w_down,
           ws_gate, ws_up, ws_down):
    n, d = grp.n_tokens, xa.shape[1]
    nb = c.shape[0]
    s_min = min(s for _, s in grp.groups)
    s_max = max(s for _, s in grp.groups)

    c8 = jnp.pad(c, ((0, 8 - nb), (0, 0)))
    mod = _mod(c8, w_mod, b_mod)[:nb].reshape(nb, N_MOD, d)
    modp = jnp.pad(mod, ((0, 0), (0, 8 - N_MOD), (0, 0)))

    perm = np.concatenate([np.arange(0, HEAD_DIM, 2), np.arange(1, HEAD_DIM, 2)])
    c0, c1, c2, c3 = ATTN_WIDTH, ATTN_WIDTH + KV_WIDTH, ATTN_WIDTH + 2 * KV_WIDTH, ATTN_WIDTH + 2 * KV_WIDTH + A_WIDTH
    wq = w_in[:, :c0].reshape(d, N_Q_HEADS, HEAD_DIM)[:, :, perm].reshape(d, ATTN_WIDTH)
    wk = w_in[:, c0:c1].reshape(d, N_KV_HEADS, HEAD_DIM)[:, :, perm].reshape(d, KV_WIDTH)
    wqt = wq.T.astype(BF16)
    wkb = wk.astype(BF16)
    wvt = w_in[:, c1:c2].T.astype(BF16)
    wu = w_in[:, c2:c3].astype(BF16)
    wgv = w_in[:, c3:].astype(BF16)
    qg = q_norm_g[perm].reshape(HEAD_DIM, 1)
    kg = k_norm_g[perm].reshape(1, HEAD_DIM)

    cos2, sin2 = _rope_tables(s_max)
    scale = HEAD_DIM ** -0.5 * LOG2E
    tabs = (jnp.asarray((cos2 * scale).T, F32), jnp.asarray((sin2 * scale).T, F32),
            jnp.asarray(cos2, F32), jnp.asarray(sin2, F32))

    tm = _tile(s_min, 512)
    qt, k, vt, u, gv = _inproj(xa, xb, modp, norm1_g.reshape(1, d), wqt, wkb, wvt, wu, wgv, qg, kg, tabs,
                               grp, tm)

    k_bound2 = HEAD_DIM * jnp.max(jnp.square(k_norm_g)) * BOUND_SLACK
    q_bound = math.sqrt(HEAD_DIM) * scale * BOUND_SLACK * jnp.max(jnp.abs(q_norm_g))
    safe = (q_bound * jnp.sqrt(k_bound2) * BOUND_SLACK < SAFE_SCORE_BOUND).astype(I32).reshape(1)
    attn = _flash(qt, k, vt, jnp.full((8, 128), k_bound2, F32), safe, grp,
                  _tile(s_min, 1024), _tile(s_min, 2048))
    sgun = _sgu(u, gv, sgu_ln_g.reshape(1, A_WIDTH), sgu_ln_b.reshape(1, A_WIDTH), w_s.astype(BF16),
                b_s.T, sgu_out_g.reshape(1, A_WIDTH), _tile(s_min, 512))

    x1, h2, lgt = _outproj(attn, sgun, xa, xb, modp, attn_out_g.reshape(1, ATTN_WIDTH), w_out.astype(BF16),
                           norm2_g.reshape(1, d), w_router.T.astype(BF16), grp, _tile(s_min, 512))

    idx, wts, rank, cnt = _router(lgt, router_bias.reshape(N_EXPERTS, 1), _tile(s_min, 512))

    bm = 512
    counts = cnt[:, 0]
    padded = (counts + bm - 1) // bm * bm
    pad_end = jnp.cumsum(padded).astype(I32)
    pad_start = pad_end - padded
    n_blocks = n * TOP_K // bm + N_EXPERTS
    p_tok = n_blocks * bm
    n_used = (pad_end[-1] // bm).reshape(1).astype(I32)
    block_start = jnp.arange(n_blocks, dtype=I32) * bm
    block_exp = jnp.minimum(jnp.sum((pad_end[None, :] <= block_start[:, None]).astype(I32), axis=1),
                            N_EXPERTS - 1)

    dest = _dest(idx, rank, pad_start, _tile(s_min, 2048))

    td = _tile(s_min, 256)
    xs = _dispatch(counts, pad_start, pad_end, _tok3(dest, td), h2, n, p_tok, td, bm)
    shared = _shared(h2, jnp.concatenate([ws_gate, ws_up], axis=1).astype(BF16), ws_down.astype(BF16),
                     _tile(s_min, 512))
    ys = _experts(block_exp, n_used, xs, w_gate, w_up, w_down, bm)

    tc = _tile(s_min, 128)
    return _combine(_tok3(dest, tc), _tok3(wts, tc), x1, shared, modp, ys, grp, tc)


def kernel(x_prompt, x_sample, c_prompt, c_sample, norm1_g, w_mod, b_mod, w_in, q_norm_g, k_norm_g, sgu_ln_g, sgu_ln_b, w_s, b_s, attn_out_g, sgu_out_g, w_out, norm2_g, w_router, router_bias, w_gate, w_up, w_down, ws_gate, ws_up, ws_down):
    depth = norm1_g.shape[0]
    bp, sp, d = x_prompt.shape
    bs, ss, _ = x_sample.shape
    grp = _Groups([(bp, sp), (bs, ss)])
    xa, xb = x_prompt.reshape(bp * sp, d), x_sample.reshape(bs * ss, d)
    c = jnp.concatenate([c_prompt, c_sample], axis=0)
    for l in range(depth):
        xa, xb = _layer(xa, xb, c, grp, norm1_g[l], w_mod[l], b_mod[l], w_in[l], q_norm_g[l], k_norm_g[l],
                   sgu_ln_g[l], sgu_ln_b[l], w_s[l], b_s[l], attn_out_g[l], sgu_out_g[l], w_out[l],
                   norm2_g[l], w_router[l], router_bias[l], w_gate[l], w_up[l], w_down[l],
                   ws_gate[l], ws_up[l], ws_down[l])
    return (xa.reshape(bp, sp, d), xb.reshape(bs, ss, d))
```

```python
import functools
import math

import numpy as np
import jax
import jax.numpy as jnp
from jax import lax
from jax.experimental import pallas as pl
from jax.experimental.pallas import tpu as pltpu

F32 = jnp.float32
BF16 = jnp.bfloat16
I32 = jnp.int32

HEAD_DIM = 128
N_Q_HEADS = 8
N_KV_HEADS = 2
Q_PER_KV = N_Q_HEADS // N_KV_HEADS
ATTN_WIDTH = N_Q_HEADS * HEAD_DIM
KV_WIDTH = N_KV_HEADS * HEAD_DIM
ROPE_THETA = 10000.0
GRID_W = 64
N_A_HEADS = 8
A_HEAD_DIM = 128
A_WIDTH = N_A_HEADS * A_HEAD_DIM
CHUNK = 128
N_EXPERTS = 64
TOP_K = 8
N_GROUPS = 8
GROUP_SIZE = N_EXPERTS // N_GROUPS
TOPK_GROUPS = 4
ROUTED_SCALE = 2.5
N_MOD = 6
EPS = 1e-6

VMEM_LIMIT_BYTES = 56 * 1024 * 1024
NT_DIMS = (((1,), (1,)), ((), ()))


def _cparams(n_axes):
    return pltpu.CompilerParams(
        dimension_semantics=("arbitrary",) * n_axes, vmem_limit_bytes=VMEM_LIMIT_BYTES)


def _tile(n, pref):
    t = min(n, pref)
    while n % t:
        t //= 2
    return t


def _const_spec(shape):
    nd = len(shape)
    return pl.BlockSpec(shape, lambda *a: (0,) * nd, pipeline_mode=pl.Buffered(1))


def _mod_kernel(c_ref, w_ref, b_ref, o_ref):
    c = c_ref[...]
    a = (c * jax.nn.sigmoid(c)).astype(BF16)
    o_ref[...] = jnp.dot(a, w_ref[...].astype(BF16), preferred_element_type=F32) + b_ref[...]


def _mod(c8, w_mod, b_mod):
    d, n = w_mod.shape
    tn = _tile(n, 1024)
    return pl.pallas_call(
        _mod_kernel,
        grid=(n // tn,),
        in_specs=[_const_spec((8, d)),
                  pl.BlockSpec((d, tn), lambda j: (0, j)),
                  pl.BlockSpec((1, tn), lambda j: (0, j))],
        out_specs=pl.BlockSpec((8, tn), lambda j: (0, j)),
        out_shape=jax.ShapeDtypeStruct((8, n), F32),
        compiler_params=_cparams(1),
        name="mod",
    )(c8, w_mod, b_mod.reshape(1, n))


class _Groups:
    def __init__(self, groups):
        self.groups = groups
        self.n_tokens = sum(b * s for b, s in groups)

    def tile_maps(self, t):
        (b0, s0), (b1, s1) = self.groups
        n0 = b0 * s0 // t
        t0, t1 = s0 // t, s1 // t

        def bidx(i):
            return jnp.where(i < n0, i // t0, b0 + (i - n0) // t1)

        def pidx(i):
            return jnp.where(i < n0, i % t0, (i - n0) % t1)

        return bidx, pidx

    def split_specs(self, t, d):
        (b0, s0), _ = self.groups
        n0 = b0 * s0 // t
        return (pl.BlockSpec((t, d), lambda i, *_: (jnp.minimum(i, n0 - 1), 0)),
                pl.BlockSpec((t, d), lambda i, *_: (jnp.maximum(i - n0, 0), 0)), n0)


def _pick_group(i, n0, a_ref, b_ref):
    return jnp.where(i < n0, a_ref[...], b_ref[...])


LANES = 128
SUBLANES = 8
U32 = jnp.uint32


def _slab_rows(d):
    rows = d // (2 * LANES)
    assert d % (2 * LANES) == 0 and rows % SUBLANES == 0, d
    return rows


def _load_slabs(ref, rows, tok0, n_tok):
    pairs = [_unpack_bf16_pair(ref[pl.ds(tok0 * rows + r, n_tok, stride=rows), :]) for r in range(rows)]
    return jnp.concatenate([p[0] for p in pairs] + [p[1] for p in pairs], axis=1).astype(BF16)


def _pack_bf16_pair(lo, hi):
    def bf16_bits(v):
        return lax.bitcast_convert_type(v.astype(BF16).astype(F32), U32)

    return (bf16_bits(lo) >> 16) | bf16_bits(hi)


def _unpack_bf16_pair(w):
    return (lax.bitcast_convert_type(w << 16, F32),
            lax.bitcast_convert_type(w & jnp.uint32(0xFFFF0000), F32))


def _store_slabs(ref, x, tok0=0):
    n_tok, d = x.shape
    rows = _slab_rows(d)
    for r in range(rows):
        ref[pl.ds(tok0 * rows + r, n_tok, stride=rows), :] = _pack_bf16_pair(
            x[:, r * LANES:(r + 1) * LANES], x[:, (rows + r) * LANES:(rows + r + 1) * LANES])


def _rope_tables(seq_len):
    pos = np.arange(seq_len)
    row = (pos // GRID_W).astype(np.float64)
    col = (pos % GRID_W).astype(np.float64)
    n_pairs = HEAD_DIM // 4
    freqs = ROPE_THETA ** (-np.arange(n_pairs, dtype=np.float64) / n_pairs)
    ang = np.concatenate([row[:, None] * freqs, col[:, None] * freqs], axis=-1)
    c, s = np.cos(ang), np.sin(ang)
    cos2 = np.concatenate([c, c], axis=-1)
    sin2 = np.concatenate([-s, s], axis=-1)
    return cos2, sin2


def _inproj_kernel(xa_ref, xb_ref, mod_ref, g1_ref, wqt_ref, wk_ref, wvt_ref, wu_ref, wgv_ref,
                   qg_ref, kg_ref, cost_ref, sint_ref, cos_ref, sin_ref,
                   qt_ref, k_ref, vt_ref, u_ref, gv_ref, *, n0):
    x = _pick_group(pl.program_id(0), n0, xa_ref, xb_ref)
    shift1 = mod_ref[0:1, :]
    scale1 = mod_ref[1:2, :]
    ms = jnp.mean(x * x, axis=-1, keepdims=True)
    h = (x * lax.rsqrt(ms + EPS) * g1_ref[...]) * (1.0 + scale1) + shift1
    hb = h.astype(BF16)
    half = HEAD_DIM // 2

    qt = lax.dot_general(wqt_ref[...], hb, NT_DIMS, preferred_element_type=F32)
    cost = cost_ref[...]
    sint = sint_ref[...]
    qg = qg_ref[...]
    for hh in range(N_Q_HEADS):
        qh = qt[hh * HEAD_DIM:(hh + 1) * HEAD_DIM, :]
        msq = jnp.mean(qh * qh, axis=0, keepdims=True)
        qn = qh * lax.rsqrt(msq + EPS) * qg
        rot = jnp.concatenate([qn[half:, :], qn[:half, :]], axis=0)
        qt_ref[hh * HEAD_DIM:(hh + 1) * HEAD_DIM, :] = (qn * cost + rot * sint).astype(BF16)

    vt_ref[...] = lax.dot_general(wvt_ref[...], hb, NT_DIMS, preferred_element_type=F32).astype(BF16)

    kk = jnp.dot(hb, wk_ref[...], preferred_element_type=F32)
    cos = cos_ref[...]
    sin = sin_ref[...]
    kg = kg_ref[...]
    for j in range(N_KV_HEADS):
        kh = kk[:, j * HEAD_DIM:(j + 1) * HEAD_DIM]
        msk = jnp.mean(kh * kh, axis=-1, keepdims=True)
        kn = kh * lax.rsqrt(msk + EPS) * kg
        rot = jnp.concatenate([kn[:, half:], kn[:, :half]], axis=1)
        k_ref[:, j * HEAD_DIM:(j + 1) * HEAD_DIM] = (kn * cos + rot * sin).astype(BF16)

    u_ref[...] = jnp.dot(hb, wu_ref[...], preferred_element_type=F32).astype(BF16)
    gv_ref[...] = jnp.dot(hb, wgv_ref[...], preferred_element_type=F32).astype(BF16)


def _inproj(xa, xb, modp, g1, wqt, wk, wvt, wu, wgv, qg, kg, tabs, grp, tm):
    n, d = grp.n_tokens, xa.shape[1]
    bidx, pidx = grp.tile_maps(tm)
    cost, sint, cos, sin = tabs
    tok = lambda w: pl.BlockSpec((tm, w), lambda i: (i, 0))
    tokt = lambda w: pl.BlockSpec((w, tm), lambda i: (0, i))
    xa_spec, xb_spec, n0 = grp.split_specs(tm, d)
    return pl.pallas_call(
        functools.partial(_inproj_kernel, n0=n0),
        grid=(n // tm,),
        in_specs=[xa_spec, xb_spec,
                  pl.BlockSpec((None, 8, d), lambda i: (bidx(i), 0, 0)),
                  _const_spec((1, d)),
                  _const_spec(wqt.shape), _const_spec(wk.shape), _const_spec(wvt.shape),
                  _const_spec(wu.shape), _const_spec(wgv.shape),
                  _const_spec((HEAD_DIM, 1)), _const_spec((1, HEAD_DIM)),
                  pl.BlockSpec((HEAD_DIM, tm), lambda i: (0, pidx(i))),
                  pl.BlockSpec((HEAD_DIM, tm), lambda i: (0, pidx(i))),
                  pl.BlockSpec((tm, HEAD_DIM), lambda i: (pidx(i), 0)),
                  pl.BlockSpec((tm, HEAD_DIM), lambda i: (pidx(i), 0))],
        out_specs=[tokt(ATTN_WIDTH), tok(KV_WIDTH), tokt(KV_WIDTH), tok(A_WIDTH), tok(A_WIDTH)],
        out_shape=[jax.ShapeDtypeStruct((ATTN_WIDTH, n), BF16),
                   jax.ShapeDtypeStruct((n, KV_WIDTH), BF16),
                   jax.ShapeDtypeStruct((KV_WIDTH, n), BF16),
                   jax.ShapeDtypeStruct((n, A_WIDTH), BF16),
                   jax.ShapeDtypeStruct((n, A_WIDTH), BF16)],
        compiler_params=_cparams(1),
        name="inproj",
    )(xa, xb, modp, g1, wqt, wk, wvt, wu, wgv, qg, kg, cost, sint, cos, sin)


LOG2E = 1.4426950408889634
SAFE_SCORE_BOUND = 60.0
BOUND_SLACK = 1.01


def _flash_kernel(qt_s, kt_s, vp_s, fl_s, safe_s, q_ref, k_ref, v_ref, vp_ref, km_ref, o_ref,
                  qs_sc, m_sc, l_sc, acc_sc, p_sc, *, bq, hb):
    s_id = pl.program_id(1)
    flag = fl_s[s_id]
    safe = safe_s[0] != 0
    last = (flag & 2) != 0

    @pl.when((flag & 1) != 0)
    def _init():
        l_sc[...] = jnp.zeros(l_sc.shape, F32)
        acc_sc[...] = jnp.zeros(acc_sc.shape, F32)
        p_sc[...] = jnp.zeros(p_sc.shape, BF16)
        for hh in range(Q_PER_KV):
            qs_sc[:, hh * bq:(hh + 1) * bq] = q_ref[hh * HEAD_DIM:(hh + 1) * HEAD_DIM, :]
        qf = qs_sc[...].astype(F32)
        qn2 = jnp.sum(qf * qf, axis=0, keepdims=True)
        bound = jnp.sqrt(qn2 * km_ref[0:1, 0:1]) * BOUND_SLACK
        m_sc[...] = jnp.where(safe, bound, -jnp.inf)

    @pl.when(safe)
    def _fast():
        q = qs_sc[...]
        m = m_sc[...]
        sa = jnp.dot(k_ref[:hb, :], q, preferred_element_type=F32)
        acc = acc_sc[...] + jnp.dot(vp_ref[...], p_sc[...], preferred_element_type=F32)
        pa = jnp.exp2(sa - m)
        la = jnp.sum(pa, axis=0, keepdims=True)
        sb = jnp.dot(k_ref[hb:, :], q, preferred_element_type=F32)
        acc_sc[...] = acc + jnp.dot(v_ref[:, :hb], pa.astype(BF16), preferred_element_type=F32)
        pb = jnp.exp2(sb - m)
        l_sc[...] += la + jnp.sum(pb, axis=0, keepdims=True)
        p_sc[...] = pb.astype(BF16)

    @pl.when(jnp.logical_and(safe, last))
    def _flush():
        acc_sc[...] += jnp.dot(v_ref[:, hb:], p_sc[...], preferred_element_type=F32)

    @pl.when(jnp.logical_not(safe))
    def _online():
        s = jnp.dot(k_ref[...], qs_sc[...], preferred_element_type=F32)
        m_prev = m_sc[...]
        m_new = jnp.maximum(m_prev, jnp.max(s, axis=0, keepdims=True))
        alpha = jnp.exp2(m_prev - m_new)
        p = jnp.exp2(s - m_new)
        l_sc[...] = alpha * l_sc[...] + jnp.sum(p, axis=0, keepdims=True)
        acc_sc[...] = alpha * acc_sc[...] + jnp.dot(v_ref[...], p.astype(BF16), preferred_element_type=F32)
        m_sc[...] = m_new

    @pl.when((flag & 2) != 0)
    def _fin():
        o = acc_sc[...] / l_sc[...]
        for hh in range(Q_PER_KV):
            o_ref[:, hh * HEAD_DIM:(hh + 1) * HEAD_DIM] = o[:, hh * bq:(hh + 1) * bq].T.astype(BF16)


def _flash_schedule(grp, bq, bkv):
    qt, kt, vp, fl = [], [], [], []
    off = 0
    for b, s in grp.groups:
        for bi in range(b):
            base = off + bi * s
            nk = s // bkv
            for qi in range(s // bq):
                for ki in range(nk):
                    blk = base // bkv + ki
                    qt.append(base // bq + qi)
                    kt.append(blk)
                    vp.append(2 * (blk - 1 if ki else blk) + 1)
                    fl.append((1 if ki == 0 else 0) | (2 if ki == nk - 1 else 0))
        off += b * s
    return tuple(np.asarray(a, np.int32) for a in (qt, kt, vp, fl))


def _flash(qt, k, vt, kmax2, safe, grp, bq, bkv):
    n = k.shape[0]
    hb = bkv // 2
    qt_a, kt_a, vp_a, fl_a = _flash_schedule(grp, bq, bkv)
    n_steps = len(qt_a)
    w = Q_PER_KV * HEAD_DIM
    gs = pltpu.PrefetchScalarGridSpec(
        num_scalar_prefetch=5,
        grid=(N_KV_HEADS, n_steps),
        in_specs=[pl.BlockSpec((w, bq), lambda h, s, qa, ka, va, fa, sa: (h, qa[s])),
                  pl.BlockSpec((bkv, HEAD_DIM), lambda h, s, qa, ka, va, fa, sa: (ka[s], h)),
                  pl.BlockSpec((HEAD_DIM, bkv), lambda h, s, qa, ka, va, fa, sa: (h, ka[s])),
                  pl.BlockSpec((HEAD_DIM, hb), lambda h, s, qa, ka, va, fa, sa: (h, va[s])),
                  pl.BlockSpec((8, 128), lambda h, s, qa, ka, va, fa, sa: (0, 0))],
        out_specs=pl.BlockSpec((bq, w), lambda h, s, qa, ka, va, fa, sa: (qa[s], h)),
        scratch_shapes=[pltpu.VMEM((HEAD_DIM, Q_PER_KV * bq), BF16),
                        pltpu.VMEM((1, Q_PER_KV * bq), F32),
                        pltpu.VMEM((1, Q_PER_KV * bq), F32),
                        pltpu.VMEM((HEAD_DIM, Q_PER_KV * bq), F32),
                        pltpu.VMEM((hb, Q_PER_KV * bq), BF16)])
    return pl.pallas_call(
        functools.partial(_flash_kernel, bq=bq, hb=hb),
        grid_spec=gs,
        out_shape=jax.ShapeDtypeStruct((n, ATTN_WIDTH), BF16),
        compiler_params=_cparams(2),
        name="flash",
    )(jnp.asarray(qt_a), jnp.asarray(kt_a), jnp.asarray(vp_a), jnp.asarray(fl_a), safe, qt, k, vt, vt, kmax2)


def _sgu_kernel(u_ref, gv_ref, lng_ref, lnb_ref, ws_ref, bst_ref, og_ref, o_ref, *, n_chunks):
    v = gv_ref[...].astype(F32)
    mean = jnp.mean(v, axis=-1, keepdims=True)
    vc = v - mean
    var = jnp.mean(vc * vc, axis=-1, keepdims=True)
    vn = (vc * lax.rsqrt(var + EPS) * lng_ref[...] + lnb_ref[...]).astype(BF16)
    u = u_ref[...].astype(F32)
    heads = []
    for hh in range(N_A_HEADS):
        cols = slice(hh * A_HEAD_DIM, (hh + 1) * A_HEAD_DIM)
        rhs = jnp.concatenate([vn[c * CHUNK:(c + 1) * CHUNK, cols] for c in range(n_chunks)], axis=1)
        mixed = jnp.dot(ws_ref[hh], rhs, preferred_element_type=F32) + bst_ref[:, hh:hh + 1]
        heads.append(jnp.concatenate(
            [mixed[:, c * A_HEAD_DIM:(c + 1) * A_HEAD_DIM] for c in range(n_chunks)], axis=0))
    sg = u * jnp.concatenate(heads, axis=1)
    ms = jnp.mean(sg * sg, axis=-1, keepdims=True)
    o_ref[...] = (sg * lax.rsqrt(ms + EPS) * og_ref[...]).astype(BF16)


def _sgu(u, gv, lng, lnb, ws, bst, og, tm):
    n = u.shape[0]
    tok = pl.BlockSpec((tm, A_WIDTH), lambda i: (i, 0))
    return pl.pallas_call(
        functools.partial(_sgu_kernel, n_chunks=tm // CHUNK),
        grid=(n // tm,),
        in_specs=[tok, tok, _const_spec((1, A_WIDTH)), _const_spec((1, A_WIDTH)),
                  _const_spec(ws.shape), _const_spec(bst.shape), _const_spec((1, A_WIDTH))],
        out_specs=tok,
        out_shape=jax.ShapeDtypeStruct((n, A_WIDTH), BF16),
        compiler_params=_cparams(1),
        name="sgu",
    )(u, gv, lng, lnb, ws, bst, og)


def _outproj_kernel(attn_ref, sgu_ref, xa_ref, xb_ref, mod_ref, ag_ref, wo_ref, g2_ref, wrt_ref,
                    x1_ref, h2_ref, lgt_ref, *, n0):
    half = attn_ref.shape[0] // 2
    spans = [(0, half), (half, 2 * half)]
    outs = []
    for r0, r1 in spans:
        a = attn_ref[r0:r1, :].astype(F32)
        ms = jnp.mean(a * a, axis=-1, keepdims=True)
        an = (a * lax.rsqrt(ms + EPS) * ag_ref[...]).astype(BF16)
        o = jnp.dot(an, wo_ref[:ATTN_WIDTH, :], preferred_element_type=F32)
        outs.append(o + jnp.dot(sgu_ref[r0:r1, :], wo_ref[ATTN_WIDTH:, :], preferred_element_type=F32))
    gate1 = mod_ref[2:3, :]
    shift2 = mod_ref[3:4, :]
    scale2 = mod_ref[4:5, :]
    first_group = pl.program_id(0) < n0
    for (r0, r1), o in zip(spans, outs):
        x1 = jnp.where(first_group, xa_ref[r0:r1, :], xb_ref[r0:r1, :]) + gate1 * o
        x1_ref[r0:r1, :] = x1
        ms2 = jnp.mean(x1 * x1, axis=-1, keepdims=True)
        h2 = (x1 * lax.rsqrt(ms2 + EPS) * g2_ref[...]) * (1.0 + scale2) + shift2
        _store_slabs(h2_ref, h2, r0)
        lgt_ref[:, r0:r1] = lax.dot_general(wrt_ref[...], h2.astype(BF16), NT_DIMS,
                                            preferred_element_type=F32)


def _outproj(attn, sgun, xa, xb, modp, ag, wo, g2, wrt, grp, tm):
    n, d = grp.n_tokens, xa.shape[1]
    bidx, _ = grp.tile_maps(tm)
    tok = lambda w: pl.BlockSpec((tm, w), lambda i: (i, 0))
    xa_spec, xb_spec, n0 = grp.split_specs(tm, d)
    return pl.pallas_call(
        functools.partial(_outproj_kernel, n0=n0),
        grid=(n // tm,),
        in_specs=[tok(ATTN_WIDTH), tok(A_WIDTH), xa_spec, xb_spec,
                  pl.BlockSpec((None, 8, d), lambda i: (bidx(i), 0, 0)),
                  _const_spec((1, ATTN_WIDTH)), _const_spec(wo.shape), _const_spec((1, d)),
                  _const_spec(wrt.shape)],
        out_specs=[tok(d), pl.BlockSpec((tm * _slab_rows(d), LANES), lambda i: (i, 0)),
                   pl.BlockSpec((N_EXPERTS, tm), lambda i: (0, i))],
        out_shape=[jax.ShapeDtypeStruct((n, d), F32),
                   jax.ShapeDtypeStruct((n * _slab_rows(d), LANES), U32),
                   jax.ShapeDtypeStruct((N_EXPERTS, n), F32)],
        compiler_params=_cparams(1),
        name="outproj",
    )(attn, sgun, xa, xb, modp, ag, wo, g2, wrt)


def _first_index(hit, iota, sentinel):
    return jnp.min(jnp.where(hit, iota, sentinel), axis=0, keepdims=True)


def _router_kernel(lg_ref, bias_ref, idx_ref, w_ref, rank_ref, cnt_ref, run_sc):
    i = pl.program_id(0)
    tt = lg_ref.shape[1]

    @pl.when(i == 0)
    def _init():
        run_sc[...] = jnp.zeros(run_sc.shape, F32)

    scores = jax.nn.sigmoid(lg_ref[...])
    biased = scores + bias_ref[...]
    neg = -jnp.inf
    iota_g = lax.broadcasted_iota(I32, (GROUP_SIZE, tt), 0).astype(F32)

    gs_rows = []
    for g in range(N_GROUPS):
        xg = biased[g * GROUP_SIZE:(g + 1) * GROUP_SIZE, :]
        m1 = jnp.max(xg, axis=0, keepdims=True)
        f1 = _first_index(xg == m1, iota_g, float(GROUP_SIZE))
        m2 = jnp.max(jnp.where(iota_g == f1, neg, xg), axis=0, keepdims=True)
        gs_rows.append(m1 + m2)
    gs = jnp.concatenate(gs_rows, axis=0)

    iota_n = lax.broadcasted_iota(I32, (N_GROUPS, tt), 0).astype(F32)
    gsel = jnp.zeros((N_GROUPS, tt), F32)
    cur = gs
    for _ in range(TOPK_GROUPS):
        m = jnp.max(cur, axis=0, keepdims=True)
        f = _first_index(cur == m, iota_n, float(N_GROUPS))
        hit = iota_n == f
        gsel = jnp.where(hit, 1.0, gsel)
        cur = jnp.where(hit, neg, cur)

    masked = jnp.concatenate(
        [jnp.where(gsel[g:g + 1, :] > 0.5, biased[g * GROUP_SIZE:(g + 1) * GROUP_SIZE, :], neg)
         for g in range(N_GROUPS)], axis=0)

    iota_e = lax.broadcasted_iota(I32, (N_EXPERTS, tt), 0).astype(F32)
    sel = jnp.zeros((N_EXPERTS, tt), F32)
    idx_rows, w_rows = [], []
    for _ in range(TOP_K):
        m = jnp.max(masked, axis=0, keepdims=True)
        f = _first_index(masked == m, iota_e, float(N_EXPERTS))
        hit = iota_e == f
        idx_rows.append(f)
        w_rows.append(jnp.sum(jnp.where(hit, scores, 0.0), axis=0, keepdims=True))
        sel = jnp.where(hit, 1.0, sel)
        masked = jnp.where(hit, neg, masked)
    idx = jnp.concatenate(idx_rows, axis=0)
    wk = jnp.concatenate(w_rows, axis=0)
    wk = wk / (jnp.sum(wk, axis=0, keepdims=True) + 1e-20) * ROUTED_SCALE

    r = lax.broadcasted_iota(I32, (tt, tt), 0)
    c = lax.broadcasted_iota(I32, (tt, tt), 1)
    upper = jnp.where(r < c, 1.0, 0.0).astype(BF16)
    prefix = jnp.dot(sel.astype(BF16), upper, preferred_element_type=F32) + run_sc[...]
    rank_rows = [jnp.sum(jnp.where(iota_e == idx[k:k + 1, :], prefix, 0.0), axis=0, keepdims=True)
                 for k in range(TOP_K)]
    run_sc[...] = run_sc[...] + jnp.sum(sel, axis=1, keepdims=True)

    idx_ref[...] = idx.astype(I32)
    w_ref[...] = wk
    rank_ref[...] = jnp.concatenate(rank_rows, axis=0).astype(I32)
    cnt_ref[...] = jnp.broadcast_to(run_sc[...], cnt_ref.shape).astype(I32)


def _router(lgt, bias, tt):
    n = lgt.shape[1]
    tokk = pl.BlockSpec((TOP_K, tt), lambda i: (0, i))
    return pl.pallas_call(
        _router_kernel,
        grid=(n // tt,),
        in_specs=[pl.BlockSpec((N_EXPERTS, tt), lambda i: (0, i)), _const_spec((N_EXPERTS, 1))],
        out_specs=[tokk, tokk, tokk, _const_spec((N_EXPERTS, 128))],
        out_shape=[jax.ShapeDtypeStruct((TOP_K, n), I32), jax.ShapeDtypeStruct((TOP_K, n), F32),
                   jax.ShapeDtypeStruct((TOP_K, n), I32), jax.ShapeDtypeStruct((N_EXPERTS, 128), I32)],
        scratch_shapes=[pltpu.VMEM((N_EXPERTS, 1), F32)],
        compiler_params=_cparams(1),
        name="router",
    )(lgt, bias)


def _dest_kernel(idx_ref, rank_ref, start_ref, o_ref):
    tt = idx_ref.shape[1]
    iota_e = lax.broadcasted_iota(I32, (N_EXPERTS, tt), 0)
    start = start_ref[...]
    rows = []
    for k in range(TOP_K):
        hit = iota_e == idx_ref[k:k + 1, :]
        rows.append(jnp.sum(jnp.where(hit, start, 0), axis=0, keepdims=True))
    o_ref[...] = jnp.concatenate(rows, axis=0) + rank_ref[...]


def _dest(idx, rank, pad_start, tt):
    n = idx.shape[1]
    tokk = pl.BlockSpec((TOP_K, tt), lambda i: (0, i))
    return pl.pallas_call(
        _dest_kernel,
        grid=(n // tt,),
        in_specs=[tokk, tokk, _const_spec((N_EXPERTS, 1))],
        out_specs=tokk,
        out_shape=jax.ShapeDtypeStruct((TOP_K, n), I32),
        compiler_params=_cparams(1),
        name="dest",
    )(idx, rank, pad_start.reshape(N_EXPERTS, 1))


def _slab(ref, tok, rows):
    return ref.at[pl.ds(pl.multiple_of(tok * rows, SUBLANES), rows)]


DISPATCH_RING = 3


def _dispatch_kernel(cnt_s, start_s, pend_s, dest_ref, h_hbm, wgu_ref, wd_ref, xs_ref, sh_ref,
                     hbuf, zero_sc, in_sems, out_sems, sem,
                     *, td, bm, n_blocks):
    i = pl.program_id(0)
    n_steps = pl.num_programs(0)
    rows = zero_sc.shape[0] // bm
    t_rows = td * rows
    slot = i % DISPATCH_RING
    nxt = (i + 1) % DISPATCH_RING

    def load(tile, s):
        return pltpu.make_async_copy(h_hbm.at[pl.ds(pl.multiple_of(tile * t_rows, SUBLANES), t_rows)],
                                     hbuf.at[s], in_sems.at[s])

    def drain(s):
        n_rows = TOP_K * t_rows
        pltpu.make_async_copy(xs_ref.at[pl.ds(0, n_rows)], xs_ref.at[pl.ds(0, n_rows)], out_sems.at[s]).wait()

    @pl.when(i == 0)
    def _():
        load(0, 0).start()

    @pl.when(i + 1 < n_steps)
    def _():
        @pl.when(i + 1 >= DISPATCH_RING)
        def _():
            drain(nxt)

        load(i + 1, nxt).start()

    load(i, slot).wait()

    def issue(t, c):
        for k in range(TOP_K):
            pltpu.make_async_copy(_slab(hbuf.at[slot], t, rows), _slab(xs_ref, dest_ref[0, t * TOP_K + k], rows),
                                  out_sems.at[slot]).start(priority=k % 2)
        return c

    lax.fori_loop(0, td, issue, 0)

    _swiglu_slabs(hbuf.at[slot], wgu_ref, wd_ref, sh_ref, packed_out=False)

    @pl.when(i == n_steps - 1)
    def _pad():
        for back in range(DISPATCH_RING):
            @pl.when(i - back >= 0)
            def _():
                drain((i - back) % DISPATCH_RING)

        zero_sc[...] = jnp.zeros(zero_sc.shape, U32)

        def pad_copy(p):
            return pltpu.make_async_copy(zero_sc.at[pl.ds(0, rows)], _slab(xs_ref, p, rows), sem)

        def tail_copy(b):
            return pltpu.make_async_copy(zero_sc, _slab(xs_ref, b, bm * rows), sem)

        first_unused = pend_s[N_EXPERTS - 1] // bm

        def tail_go(b, c):
            tail_copy(b).start()
            return c

        lax.fori_loop(first_unused, n_blocks, tail_go, 0)

        def tail_done(b, c):
            tail_copy(b).wait()
            return c

        lax.fori_loop(first_unused, n_blocks, tail_done, 0)

        def per_expert(e, c):
            lo = start_s[e] + cnt_s[e]
            hi = pend_s[e]

            def go(p, c2):
                pad_copy(p).start()
                return c2

            lax.fori_loop(lo, hi, go, 0)

            def done(p, c2):
                pad_copy(p).wait()
                return c2

            lax.fori_loop(lo, hi, done, 0)
            return c

        lax.fori_loop(0, N_EXPERTS, per_expert, 0)


def _dispatch(counts, pad_start, pad_end, dest3, h2s, wgu, wd, n, p_tok, td, bm):
    rows = h2s.shape[0] // n
    d = wd.shape[1]
    gs = pltpu.PrefetchScalarGridSpec(
        num_scalar_prefetch=3,
        grid=(n // td,),
        in_specs=[pl.BlockSpec((None, 1, TOP_K * td), lambda i, *_: (i, 0, 0), memory_space=pltpu.SMEM),
                  pl.BlockSpec(memory_space=pl.ANY),
                  _const_spec(wgu.shape), _const_spec(wd.shape)],
        out_specs=[pl.BlockSpec(memory_space=pl.ANY), pl.BlockSpec((td, d), lambda i, *_: (i, 0))],
        scratch_shapes=[pltpu.VMEM((DISPATCH_RING, td * rows, LANES), U32),
                        pltpu.VMEM((bm * rows, LANES), U32),
                        pltpu.SemaphoreType.DMA((DISPATCH_RING,)), pltpu.SemaphoreType.DMA((DISPATCH_RING,)),
                        pltpu.SemaphoreType.DMA(())])
    return pl.pallas_call(
        functools.partial(_dispatch_kernel, td=td, bm=bm, n_blocks=p_tok // bm),
        grid_spec=gs,
        out_shape=[jax.ShapeDtypeStruct((p_tok * rows, LANES), U32), jax.ShapeDtypeStruct((n, d), F32)],
        compiler_params=_cparams(1),
        name="dispatch",
    )(counts, pad_start, pad_end, dest3, h2s, wgu, wd)


def _swiglu_slabs(x_ref, wgu_ref, wd_ref, o_ref, *, packed_out):
    de, d = wd_ref.shape
    rows = _slab_rows(d)
    half = x_ref.shape[0] // rows // 2
    ups = [jnp.dot(_load_slabs(x_ref, rows, t0, half), wgu_ref[...],
                   preferred_element_type=F32) for t0 in (0, half)]
    for t0, r in zip((0, half), ups):
        a = r[:, :de]
        act = (a * jax.nn.sigmoid(a) * r[:, de:]).astype(BF16)
        out = jnp.dot(act, wd_ref[...], preferred_element_type=F32)
        if packed_out:
            _store_slabs(o_ref, out, t0)
        else:
            o_ref[t0:t0 + half, :] = out


def _shared_kernel(h_ref, wgu_ref, wd_ref, o_ref):
    _swiglu_slabs(h_ref, wgu_ref, wd_ref, o_ref, packed_out=False)


def _shared(h2s, wgu, wd, tm):
    d = wd.shape[1]
    rows = _slab_rows(d)
    n = h2s.shape[0] // rows
    return pl.pallas_call(
        _shared_kernel,
        grid=(n // tm,),
        in_specs=[pl.BlockSpec((tm * rows, LANES), lambda i: (i, 0)),
                  _const_spec(wgu.shape), _const_spec(wd.shape)],
        out_specs=pl.BlockSpec((tm, d), lambda i: (i, 0)),
        out_shape=jax.ShapeDtypeStruct((n, d), F32),
        compiler_params=_cparams(1),
        name="shared",
    )(h2s, wgu, wd)


def _experts_kernel(be_s, nu_s, x_ref, wg_ref, wu_ref, wd_ref, o_ref, wgu_sc, wd_sc):
    i = pl.program_id(0)
    used = i < nu_s[0]
    prev = be_s[jnp.maximum(i - 1, 0)]
    fresh = jnp.logical_or(i == 0, be_s[i] != prev)
    de = wd_sc.shape[0]

    @pl.when(jnp.logical_and(used, fresh))
    def _cast():
        wgu_sc[:, :de] = wg_ref[...].astype(BF16)
        wgu_sc[:, de:] = wu_ref[...].astype(BF16)
        wd_sc[...] = wd_ref[...].astype(BF16)

    @pl.when(used)
    def _run():
        _swiglu_slabs(x_ref, wgu_sc, wd_sc, o_ref, packed_out=True)

    @pl.when(jnp.logical_not(used))
    def _unused():
        o_ref[...] = jnp.zeros(o_ref.shape, U32)


def _experts(block_exp, n_used, xs, w_gate, w_up, w_down, bm):
    d, de = w_gate.shape[1], w_gate.shape[2]
    rows = _slab_rows(d)
    p_tok = xs.shape[0] // rows
    n_blocks = p_tok // bm
    row = lambda i, be, nu: (jnp.minimum(i, nu[0] - 1), 0)
    wsel = lambda i, be, nu: (be[jnp.minimum(i, nu[0] - 1)], 0, 0)
    gs = pltpu.PrefetchScalarGridSpec(
        num_scalar_prefetch=2,
        grid=(n_blocks,),
        in_specs=[pl.BlockSpec((bm * rows, LANES), row),
                  pl.BlockSpec((None, d, de), wsel),
                  pl.BlockSpec((None, d, de), wsel),
                  pl.BlockSpec((None, de, d), wsel)],
        out_specs=pl.BlockSpec((bm * rows, LANES), lambda i, be, nu: (i, 0)),
        scratch_shapes=[pltpu.VMEM((d, 2 * de), BF16), pltpu.VMEM((de, d), BF16)])
    return pl.pallas_call(
        _experts_kernel,
        grid_spec=gs,
        out_shape=jax.ShapeDtypeStruct(xs.shape, U32),
        compiler_params=_cparams(1),
        name="experts",
    )(block_exp, n_used, xs, w_gate, w_up, w_down)


def _combine_kernel(dest_ref, nxt_ref, w_ref, x1_ref, sh_ref, mod_ref, ys_ref, oa_ref, ob_ref,
                    buf, lo_sc, hi_sc, sems, *, tc, n0):
    i = pl.program_id(0)
    rows = lo_sc.shape[0] // tc
    cur = i % 2

    def gather(d_ref, b):
        def issue(t, c):
            for k in range(TOP_K):
                pltpu.make_async_copy(_slab(ys_ref, d_ref[0, t * TOP_K + k], rows),
                                      _slab(buf.at[b, k], t, rows),
                                      sems.at[b]).start(priority=k % 2)
            return c

        lax.fori_loop(0, tc, issue, 0)

    @pl.when(i == 0)
    def _():
        gather(dest_ref, 0)

    @pl.when(i + 1 < pl.num_programs(0))
    def _():
        gather(nxt_ref, 1 - cur)

    pltpu.make_async_copy(buf.at[cur], buf.at[cur], sems.at[cur]).wait()

    def token_sum(t, c):
        r0 = pl.multiple_of(t * rows, SUBLANES)
        lo = hi = None
        for k in range(TOP_K):
            lo_k, hi_k = _unpack_bf16_pair(buf[cur, k, pl.ds(r0, rows), :])
            wk = w_ref[0, t * TOP_K + k]
            lo = wk * lo_k if lo is None else lo + wk * lo_k
            hi = wk * hi_k if hi is None else hi + wk * hi_k
        lo_sc[pl.ds(r0, rows), :] = lo
        hi_sc[pl.ds(r0, rows), :] = hi
        return c

    lax.fori_loop(0, tc, token_sum, 0, unroll=4)

    routed = jnp.concatenate([lo_sc[pl.ds(r, tc, stride=rows), :] for r in range(rows)]
                             + [hi_sc[pl.ds(r, tc, stride=rows), :] for r in range(rows)], axis=1)
    y = x1_ref[...] + mod_ref[5:6, :] * (sh_ref[...] + routed)

    @pl.when(pl.program_id(0) < n0)
    def _():
        oa_ref[...] = y

    @pl.when(pl.program_id(0) >= n0)
    def _():
        ob_ref[...] = y


def _combine(dest3, w3, x1, shared, modp, ys, grp, tc):
    n, d = x1.shape
    rows = _slab_rows(d)
    bidx, _ = grp.tile_maps(tc)
    tok = pl.BlockSpec((tc, d), lambda i: (i, 0))
    oa_spec, ob_spec, n0 = grp.split_specs(tc, d)
    (b0, s0), (b1, s1) = grp.groups
    return pl.pallas_call(
        functools.partial(_combine_kernel, tc=tc, n0=n0),
        grid=(n // tc,),
        in_specs=[pl.BlockSpec((None, 1, TOP_K * tc), lambda i: (i, 0, 0), memory_space=pltpu.SMEM),
                  pl.BlockSpec((None, 1, TOP_K * tc), lambda i: (jnp.minimum(i + 1, n // tc - 1), 0, 0),
                               memory_space=pltpu.SMEM),
                  pl.BlockSpec((None, 1, TOP_K * tc), lambda i: (i, 0, 0), memory_space=pltpu.SMEM),
                  tok, tok,
                  pl.BlockSpec((None, 8, d), lambda i: (bidx(i), 0, 0)),
                  pl.BlockSpec(memory_space=pl.ANY)],
        out_specs=[oa_spec, ob_spec],
        out_shape=[jax.ShapeDtypeStruct((b0 * s0, d), F32), jax.ShapeDtypeStruct((b1 * s1, d), F32)],
        scratch_shapes=[pltpu.VMEM((2, TOP_K, tc * rows, LANES), U32),
                        pltpu.VMEM((tc * rows, LANES), F32), pltpu.VMEM((tc * rows, LANES), F32),
                        pltpu.SemaphoreType.DMA((2,))],
        compiler_params=_cparams(1),
        name="combine",
    )(dest3, dest3, w3, x1, shared, modp, ys)


def _tok3(a, t):
    k, n = a.shape
    return a.reshape(k, n // t, t).transpose(1, 2, 0).reshape(n // t, 1, t * k)


def _layer(xa, xb, c, grp, norm1_g, w_mod, b_mod, w_in, q_norm_g, k_norm_g, sgu_ln_g, sgu_ln_b, w_s, b_s,
           attn_out_g, sgu_out_g, w_out, norm2_g, w_router, router_bias, w_gate, w_up, w_down,
           ws_gate, ws_up, ws_down):
    n, d = grp.n_tokens, xa.shape[1]
    nb = c.shape[0]
    s_min = min(s for _, s in grp.groups)
    s_max = max(s for _, s in grp.groups)

    c8 = jnp.pad(c, ((0, 8 - nb), (0, 0)))
    mod = _mod(c8, w_mod, b_mod)[:nb].reshape(nb, N_MOD, d)
    modp = jnp.pad(mod, ((0, 0), (0, 8 - N_MOD), (0, 0)))

    perm = np.concatenate([np.arange(0, HEAD_DIM, 2), np.arange(1, HEAD_DIM, 2)])
    c0, c1, c2, c3 = ATTN_WIDTH, ATTN_WIDTH + KV_WIDTH, ATTN_WIDTH + 2 * KV_WIDTH, ATTN_WIDTH + 2 * KV_WIDTH + A_WIDTH
    wq = w_in[:, :c0].reshape(d, N_Q_HEADS, HEAD_DIM)[:, :, perm].reshape(d, ATTN_WIDTH)
    wk = w_in[:, c0:c1].reshape(d, N_KV_HEADS, HEAD_DIM)[:, :, perm].reshape(d, KV_WIDTH)
    wqt = wq.T.astype(BF16)
    wkb = wk.astype(BF16)
    wvt = w_in[:, c1:c2].T.astype(BF16)
    wu = w_in[:, c2:c3].astype(BF16)
    wgv = w_in[:, c3:].astype(BF16)
    qg = q_norm_g[perm].reshape(HEAD_DIM, 1)
    kg = k_norm_g[perm].reshape(1, HEAD_DIM)

    cos2, sin2 = _rope_tables(s_max)
    scale = HEAD_DIM ** -0.5 * LOG2E
    tabs = (jnp.asarray((cos2 * scale).T, F32), jnp.asarray((sin2 * scale).T, F32),
            jnp.asarray(cos2, F32), jnp.asarray(sin2, F32))

    tm = _tile(s_min, 512)
    qt, k, vt, u, gv = _inproj(xa, xb, modp, norm1_g.reshape(1, d), wqt, wkb, wvt, wu, wgv, qg, kg, tabs,
                               grp, tm)

    k_bound2 = HEAD_DIM * jnp.max(jnp.square(k_norm_g)) * BOUND_SLACK
    q_bound = math.sqrt(HEAD_DIM) * scale * BOUND_SLACK * jnp.max(jnp.abs(q_norm_g))
    safe = (q_bound * jnp.sqrt(k_bound2) * BOUND_SLACK < SAFE_SCORE_BOUND).astype(I32).reshape(1)
    attn = _flash(qt, k, vt, jnp.full((8, 128), k_bound2, F32), safe, grp,
                  _tile(s_min, 1024), _tile(s_min, 2048))
    sgun = _sgu(u, gv, sgu_ln_g.reshape(1, A_WIDTH), sgu_ln_b.reshape(1, A_WIDTH), w_s.astype(BF16),
                b_s.T, sgu_out_g.reshape(1, A_WIDTH), _tile(s_min, 512))

    x1, h2, lgt = _outproj(attn, sgun, xa, xb, modp, attn_out_g.reshape(1, ATTN_WIDTH), w_out.astype(BF16),
                           norm2_g.reshape(1, d), w_router.T.astype(BF16), grp, _tile(s_min, 512))

    idx, wts, rank, cnt = _router(lgt, router_bias.reshape(N_EXPERTS, 1), _tile(s_min, 512))

    bm = 512
    counts = cnt[:, 0]
    padded = (counts + bm - 1) // bm * bm
    pad_end = jnp.cumsum(padded).astype(I32)
    pad_start = pad_end - padded
    n_blocks = n * TOP_K // bm + N_EXPERTS
    p_tok = n_blocks * bm
    n_used = (pad_end[-1] // bm).reshape(1).astype(I32)
    block_start = jnp.arange(n_blocks, dtype=I32) * bm
    block_exp = jnp.minimum(jnp.sum((pad_end[None, :] <= block_start[:, None]).astype(I32), axis=1),
                            N_EXPERTS - 1)

    dest = _dest(idx, rank, pad_start, _tile(s_min, 2048))

    td = _tile(s_min, 256)
    xs, shared = _dispatch(counts, pad_start, pad_end, _tok3(dest, td), h2,
                           jnp.concatenate([ws_gate, ws_up], axis=1).astype(BF16), ws_down.astype(BF16),
                           n, p_tok, td, bm)
    ys = _experts(block_exp, n_used, xs, w_gate, w_up, w_down, bm)

    tc = _tile(s_min, 128)
    return _combine(_tok3(dest, tc), _tok3(wts, tc), x1, shared, modp, ys, grp, tc)


def kernel(x_prompt, x_sample, c_prompt, c_sample, norm1_g, w_mod, b_mod, w_in, q_norm_g, k_norm_g, sgu_ln_g, sgu_ln_b, w_s, b_s, attn_out_g, sgu_out_g, w_out, norm2_g, w_router, router_bias, w_gate, w_up, w_down, ws_gate, ws_up, ws_down):
    depth = norm1_g.shape[0]
    bp, sp, d = x_prompt.shape
    bs, ss, _ = x_sample.shape
    grp = _Groups([(bp, sp), (bs, ss)])
    xa, xb = x_prompt.reshape(bp * sp, d), x_sample.reshape(bs * ss, d)
    c = jnp.concatenate([c_prompt, c_sample], axis=0)
    for l in range(depth):
        xa, xb = _layer(xa, xb, c, grp, norm1_g[l], w_mod[l], b_mod[l], w_in[l], q_norm_g[l], k_norm_g[l],
                   sgu_ln_g[l], sgu_ln_b[l], w_s[l], b_s[l], attn_out_g[l], sgu_out_g[l], w_out[l],
                   norm2_g[l], w_router[l], router_bias[l], w_gate[l], w_up[l], w_down[l],
                   ws_gate[l], ws_up[l], ws_down[l])
    return (xa.reshape(bp, sp, d), xb.reshape(bs, ss, d))
```
